```python
import jax, jax.numpy as jnp
from jax import lax
import numpy as np

D_MODEL = 1024
BATCH = 4
SEQ = 4096
DEPTH = 4

CTX_LEN = 256
GRID_W = 64

POOL_GROUPS = 4
POOL_WINDOWS = (2, 4, 8, 16)
POOL_GROUP_DIM = 64
POOL_WIDTH = POOL_GROUPS * POOL_GROUP_DIM
NA_HEADS = 4
NA_HEAD_DIM = 64
NA_WIDTH = NA_HEADS * NA_HEAD_DIM
NA_WIN_ROWS = 8
NA_WIN_COLS = 16
NA_QCOLS = 16
NA_KCOLS = NA_QCOLS + NA_WIN_COLS
SGU_GROUPS = 4
SGU_GROUP_DIM = 64
SGU_WIDTH = SGU_GROUPS * SGU_GROUP_DIM
SGU_CHUNK = 128
MLA_HEADS = 4
MLA_Q_RANK = 256
MLA_KV_RANK = 128
MLA_NOPE_DIM = 64
MLA_ROPE_DIM = 32
MLA_V_DIM = 64
MLA_WIDTH = MLA_HEADS * MLA_V_DIM
MLA_SCALE = (MLA_NOPE_DIM + MLA_ROPE_DIM) ** -0.5
ROPE_BASE = 10000.0
ATTN_Q_BLOCK = 128

MIX_WIDTH = POOL_WIDTH + NA_WIDTH + SGU_WIDTH + MLA_WIDTH
IN_A = POOL_WIDTH
IN_B = 3 * NA_WIDTH
IN_C = 2 * SGU_WIDTH
IN_D = MLA_Q_RANK + MLA_KV_RANK + MLA_ROPE_DIM
OFF_B = IN_A
OFF_C = OFF_B + IN_B
OFF_D = OFF_C + IN_C
IN_WIDTH = OFF_D + IN_D

N_EXPERTS = 16
N_EXPERT_GROUPS = 4
EXPERTS_PER_GROUP = N_EXPERTS // N_EXPERT_GROUPS
TOP_K = 2
D_EXPERT = 256

N_MOD = 6
EPS = 1e-6
NEG_INF = -1e30

kernel_name = 'hybrid_parallel_group_diffusion_block'


def rms_norm(x, g):
    xf = x.astype(jnp.float32)
    y = xf * lax.rsqrt(jnp.mean(xf * xf, axis=-1, keepdims=True) + EPS)
    return (y * g.astype(jnp.float32)).astype(x.dtype)


def modulate(h, shift, scale):
    return h * (1 + scale) + shift


def axial_rope_tables(S):
    t = jnp.arange(S)
    row = (t // GRID_W).astype(jnp.float32)
    col = (t % GRID_W).astype(jnp.float32)
    nf = MLA_ROPE_DIM // 4
    inv = ROPE_BASE ** (-jnp.arange(nf, dtype=jnp.float32) / nf)
    ang = jnp.concatenate([row[:, None] * inv, col[:, None] * inv], axis=-1)
    return jnp.cos(ang), jnp.sin(ang)


def rope_2d(x, cos, sin):
    r1, r2, c1, c2 = jnp.split(x, 4, axis=-1)
    x1 = jnp.concatenate([r1, c1], axis=-1)
    x2 = jnp.concatenate([r2, c2], axis=-1)
    cos = cos.astype(x.dtype)
    sin = sin.astype(x.dtype)
    o1 = x1 * cos - x2 * sin
    o2 = x1 * sin + x2 * cos
    nf = o1.shape[-1] // 2
    return jnp.concatenate([o1[..., :nf], o2[..., :nf], o1[..., nf:], o2[..., nf:]], axis=-1)


def multiscale_pool(u, w, s):
    B, L, _ = u.shape
    ug = u.reshape(B, L, POOL_GROUPS, POOL_GROUP_DIM).astype(jnp.float32)
    csum = jnp.concatenate([jnp.zeros_like(ug[:, :1]), jnp.cumsum(ug, axis=1)], axis=1)
    t = jnp.arange(L)
    diffs = []
    for g, win in enumerate(POOL_WINDOWS):
        lo = jnp.clip(t - win // 2, 0, L)
        hi = jnp.clip(t - win // 2 + win, 0, L)
        cg = csum[:, :, g]
        mean = (cg[:, hi] - cg[:, lo]) / (hi - lo).astype(jnp.float32)[:, None]
        diffs.append(mean - ug[:, :, g])
    d = jnp.stack(diffs, axis=2).astype(u.dtype)
    y = jnp.einsum('blgc,gcd->blgd', d, w).reshape(B, L, POOL_WIDTH)
    return y * s


def spatial_gating(uv, g_norm, w_s, b_s):
    B, L, _ = uv.shape
    uv = jax.nn.gelu(uv)
    u, v = uv[..., :SGU_WIDTH], uv[..., SGU_WIDTH:]
    n = L // SGU_CHUNK
    vg = rms_norm(v.reshape(B, n, SGU_CHUNK, SGU_GROUPS, SGU_GROUP_DIM),
                  g_norm.reshape(SGU_GROUPS, SGU_GROUP_DIM))
    mixed = jnp.einsum('gpq,bnqgc->bnpgc', w_s, vg) + b_s.T[:, :, None]
    return u * mixed.reshape(B, L, SGU_WIDTH)


def dense_attention(q, k, v):
    s = jnp.einsum('bqhe,bkhe->bhqk', q, k).astype(jnp.float32) * (q.shape[-1] ** -0.5)
    p = jax.nn.softmax(s, axis=-1).astype(v.dtype)
    return jnp.einsum('bhqk,bkhe->bqhe', p, v)


def neighborhood_attention(q, k, v, kc, vc, rpb, rows):
    B, S, H, E = q.shape
    kr = min(NA_WIN_ROWS, rows)
    ncb = GRID_W // NA_QCOLS
    r = jnp.arange(rows)
    row_idx = jnp.clip(r - kr // 2, 0, rows - kr)[:, None] + jnp.arange(kr)
    qcol = (jnp.arange(ncb) * NA_QCOLS)[:, None] + jnp.arange(NA_QCOLS)
    ws = jnp.clip(qcol - NA_WIN_COLS // 2, 0, GRID_W - NA_WIN_COLS)
    cs = jnp.minimum(ws[:, 0], GRID_W - NA_KCOLS)
    col_idx = cs[:, None] + jnp.arange(NA_KCOLS)
    key_tok = (row_idx[:, None, :, None] * GRID_W + col_idx[None, :, None, :]).reshape(rows, ncb, kr * NA_KCOLS)
    kcol = col_idx[:, None, :]
    valid = (kcol >= ws[:, :, None]) & (kcol < ws[:, :, None] + NA_WIN_COLS)
    valid = jnp.broadcast_to(valid[:, :, None, :], (ncb, NA_QCOLS, kr, NA_KCOLS)).reshape(ncb, NA_QCOLS, kr * NA_KCOLS)
    dr = row_idx - r[:, None] + NA_WIN_ROWS - 1
    dc = jnp.clip(kcol - qcol[:, :, None] + NA_WIN_COLS - 1, 0, 2 * NA_WIN_COLS - 2)
    bias = rpb[:, dr[:, :, None, None, None], dc[None, None]]
    bias = bias.transpose(0, 1, 3, 4, 2, 5).reshape(H, rows, ncb, NA_QCOLS, kr * NA_KCOLS).astype(jnp.float32)

    scale = E ** -0.5
    qg = q.reshape(B, rows, ncb, NA_QCOLS, H, E)
    kg = jnp.take(k, key_tok, axis=1)
    vg = jnp.take(v, key_tok, axis=1)
    s_win = jnp.einsum('brjqhe,brjkhe->bhrjqk', qg, kg).astype(jnp.float32) * scale + bias
    s_win = jnp.where(valid, s_win, NEG_INF)
    s_ctx = jnp.einsum('brjqhe,bche->bhrjqc', qg, kc).astype(jnp.float32) * scale
    p = jax.nn.softmax(jnp.concatenate([s_win, s_ctx], axis=-1), axis=-1).astype(v.dtype)
    nw = kr * NA_KCOLS
    o = (jnp.einsum('bhrjqk,brjkhe->brjqhe', p[..., :nw], vg)
         + jnp.einsum('bhrjqc,bche->brjqhe', p[..., nw:], vc))
    return o.reshape(B, S, H * E)


def mla_queries(part, gq, w_uq):
    B, L, _ = part.shape
    q = (rms_norm(part[..., :MLA_Q_RANK], gq) @ w_uq).reshape(B, L, MLA_HEADS, MLA_NOPE_DIM + MLA_ROPE_DIM)
    return q[..., :MLA_NOPE_DIM], q[..., MLA_NOPE_DIM:]


def mla_keys(part, gkv, w_ukv):
    B, L, _ = part.shape
    ckv = part[..., MLA_Q_RANK:MLA_Q_RANK + MLA_KV_RANK]
    k_rope = part[..., MLA_Q_RANK + MLA_KV_RANK:]
    kv = (rms_norm(ckv, gkv) @ w_ukv).reshape(B, L, MLA_HEADS, MLA_NOPE_DIM + MLA_V_DIM)
    return kv[..., :MLA_NOPE_DIM], k_rope, kv[..., MLA_NOPE_DIM:]


def mla_attend(q_nope, q_rope, k_nope, k_rope, v):
    s = (jnp.einsum('bqhd,bkhd->bhqk', q_nope, k_nope)
         + jnp.einsum('bqhr,bkr->bhqk', q_rope, k_rope)).astype(jnp.float32) * MLA_SCALE
    p = jax.nn.softmax(s, axis=-1).astype(v.dtype)
    return jnp.einsum('bhqk,bkhd->bqhd', p, v)


def blocked_mla(q_nope, q_rope, k_nope, k_rope, v):
    B, S, H, _ = q_nope.shape
    nb = S // ATTN_Q_BLOCK

    def to_blocks(a):
        return jnp.moveaxis(a.reshape(B, nb, ATTN_Q_BLOCK, *a.shape[2:]), 1, 0)

    out = lax.map(lambda qb: mla_attend(qb[0], qb[1], k_nope, k_rope, v),
                  (to_blocks(q_nope), to_blocks(q_rope)))
    return jnp.moveaxis(out, 0, 1).reshape(B, S, H * MLA_V_DIM)


def split_na(part):
    B, L, _ = part.shape
    return [t.reshape(B, L, NA_HEADS, NA_HEAD_DIM) for t in jnp.split(part, 3, axis=-1)]


def token_mixing(pl, pc, cos, sin, rows, pool_w, pool_s, rpb, sgu_g, sgu_w, sgu_b,
                 gq, w_uq, gkv, w_ukv, ctx_out):
    B = pl.shape[0]
    Lc = pc.shape[1]
    y_a = multiscale_pool(pl[..., :OFF_B], pool_w, pool_s)
    ql, kl, vl = split_na(pl[..., OFF_B:OFF_C])
    qc, kc, vc = split_na(pc[..., OFF_B:OFF_C])
    y_b = neighborhood_attention(ql, kl, vl, kc, vc, rpb, rows)
    y_c = spatial_gating(pl[..., OFF_C:OFF_D], sgu_g, sgu_w, sgu_b)
    dl, dc = pl[..., OFF_D:], pc[..., OFF_D:]
    qn_l, qr_l = mla_queries(dl, gq, w_uq)
    kn_l, kr_l, v_l = mla_keys(dl, gkv, w_ukv)
    kn_c, kr_c, v_c = mla_keys(dc, gkv, w_ukv)
    qr_l = rope_2d(qr_l, cos[:, None, :], sin[:, None, :])
    kr_l = rope_2d(kr_l, cos, sin)
    y_d = blocked_mla(qn_l, qr_l,
                      jnp.concatenate([kn_c, kn_l], axis=1),
                      jnp.concatenate([kr_c, kr_l], axis=1),
                      jnp.concatenate([v_c, v_l], axis=1))
    ml = jnp.concatenate([y_a, y_b, y_c, y_d], axis=-1)
    if not ctx_out:
        return ml, None
    yc_a = multiscale_pool(pc[..., :OFF_B], pool_w, pool_s)
    yc_b = dense_attention(qc, kc, vc).reshape(B, Lc, NA_WIDTH)
    yc_c = spatial_gating(pc[..., OFF_C:OFF_D], sgu_g, sgu_w, sgu_b)
    qn_c, qr_c = mla_queries(dc, gq, w_uq)
    yc_d = mla_attend(qn_c, qr_c, kn_c, kr_c, v_c).reshape(B, Lc, MLA_WIDTH)
    mc = jnp.concatenate([yc_a, yc_b, yc_c, yc_d], axis=-1)
    return ml, mc


def grouped_moe(h, router_w, router_b, w_gate, w_up, w_down):
    shp = h.shape
    ht = h.reshape(-1, shp[-1])
    scores = jax.nn.sigmoid((ht @ router_w).astype(jnp.float32))
    biased = scores + router_b.astype(jnp.float32)
    grp_score = lax.top_k(biased.reshape(-1, N_EXPERT_GROUPS, EXPERTS_PER_GROUP), TOP_K)[0].sum(-1)
    sel_group = jnp.argmax(grp_score, axis=-1)
    in_group = (jnp.arange(N_EXPERTS) // EXPERTS_PER_GROUP)[None, :] == sel_group[:, None]
    _, idx = lax.top_k(jnp.where(in_group, biased, NEG_INF), TOP_K)
    w = jnp.take_along_axis(scores, idx, axis=-1)
    w = w / jnp.sum(w, axis=-1, keepdims=True)
    gates = jnp.sum(jax.nn.one_hot(idx, N_EXPERTS, dtype=jnp.float32) * w[..., None], axis=1).astype(h.dtype)
    hid = jax.nn.silu(jnp.einsum('td,edf->tef', ht, w_gate)) * jnp.einsum('td,edf->tef', ht, w_up)
    y = jnp.einsum('tef,efd->td', hid * gates[:, :, None], w_down)
    return y.reshape(shp)


def setup_inputs(seed: int = 0) -> dict:
    key = jax.random.key(seed)
    ks = jax.random.split(key, 32)
    D = D_MODEL
    L = DEPTH

    def nrm(k, shape, s):
        return jax.random.normal(k, shape, jnp.float32) * s

    return {
        'x': nrm(ks[0], (BATCH, SEQ, D), 1.0),
        'c': nrm(ks[1], (BATCH, D), 1.0),
        'ctx': nrm(ks[2], (BATCH, CTX_LEN, D), 1.0),
        'c_ctx': nrm(ks[3], (D,), 1.0),
        'ada_w': nrm(ks[4], (L, D, N_MOD * D), 0.5 * D ** -0.5),
        'ada_b': nrm(ks[5], (L, N_MOD * D), 0.02),
        'norm1_g': 1.0 + nrm(ks[6], (L, D), 0.02),
        'norm2_g': 1.0 + nrm(ks[7], (L, D), 0.02),
        'w_in': nrm(ks[8], (L, D, IN_WIDTH), D ** -0.5),
        'w_out': nrm(ks[9], (L, MIX_WIDTH, D), MIX_WIDTH ** -0.5),
        'pool_w': nrm(ks[10], (L, POOL_GROUPS, POOL_GROUP_DIM, POOL_GROUP_DIM), POOL_GROUP_DIM ** -0.5),
        'pool_s': 1.0 + nrm(ks[11], (L, POOL_WIDTH), 0.1),
        'na_rpb': nrm(ks[12], (L, NA_HEADS, 2 * NA_WIN_ROWS - 1, 2 * NA_WIN_COLS - 1), 0.1),
        'sgu_norm_g': 1.0 + nrm(ks[13], (L, SGU_WIDTH), 0.02),
        'sgu_w': nrm(ks[14], (L, SGU_GROUPS, SGU_CHUNK, SGU_CHUNK), SGU_CHUNK ** -0.5),
        'sgu_b': 1.0 + nrm(ks[15], (L, SGU_GROUPS, SGU_CHUNK), 0.02),
        'mla_q_norm_g': 1.0 + nrm(ks[16], (L, MLA_Q_RANK), 0.02),
        'mla_w_uq': nrm(ks[17], (L, MLA_Q_RANK, MLA_HEADS * (MLA_NOPE_DIM + MLA_ROPE_DIM)), MLA_Q_RANK ** -0.5),
        'mla_kv_norm_g': 1.0 + nrm(ks[18], (L, MLA_KV_RANK), 0.02),
        'mla_w_ukv': nrm(ks[19], (L, MLA_KV_RANK, MLA_HEADS * (MLA_NOPE_DIM + MLA_V_DIM)), MLA_KV_RANK ** -0.5),
        'router_w': nrm(ks[20], (D, N_EXPERTS), D ** -0.5),
        'router_b': nrm(ks[21], (N_EXPERTS,), 0.01),
        'moe_w_gate': nrm(ks[22], (L, N_EXPERTS, D, D_EXPERT), D ** -0.5),
        'moe_w_up': nrm(ks[23], (L, N_EXPERTS, D, D_EXPERT), D ** -0.5),
        'moe_w_down': nrm(ks[24], (L, N_EXPERTS, D_EXPERT, D), D_EXPERT ** -0.5),
        'final_g': 1.0 + nrm(ks[25], (D,), 0.02),
    }


def reference(x, c, ctx, c_ctx, ada_w, ada_b, norm1_g, norm2_g, w_in, w_out, pool_w, pool_s, na_rpb,
              sgu_norm_g, sgu_w, sgu_b, mla_q_norm_g, mla_w_uq, mla_kv_norm_g, mla_w_ukv,
              router_w, router_b, moe_w_gate, moe_w_up, moe_w_down, final_g):
    B, S, _ = x.shape
    rows = S // GRID_W
    cos, sin = axial_rope_tables(S)
    xl, xc = x, ctx
    for l in range(DEPTH):
        ctx_out = l < DEPTH - 1
        mod_l = jnp.split((jax.nn.silu(c) @ ada_w[l] + ada_b[l])[:, None, :], N_MOD, axis=-1)
        mod_c = jnp.split(jax.nn.silu(c_ctx) @ ada_w[l] + ada_b[l], N_MOD, axis=-1)
        pl = modulate(rms_norm(xl, norm1_g[l]), mod_l[0], mod_l[1]) @ w_in[l]
        pc = modulate(rms_norm(xc, norm1_g[l]), mod_c[0], mod_c[1]) @ w_in[l]
        ml, mc = token_mixing(pl, pc, cos, sin, rows, pool_w[l], pool_s[l], na_rpb[l],
                              sgu_norm_g[l], sgu_w[l], sgu_b[l],
                              mla_q_norm_g[l], mla_w_uq[l], mla_kv_norm_g[l], mla_w_ukv[l], ctx_out)
        xl = xl + mod_l[2] * (ml @ w_out[l])
        hl = modulate(rms_norm(xl, norm2_g[l]), mod_l[3], mod_l[4])
        xl = xl + mod_l[5] * grouped_moe(hl, router_w, router_b, moe_w_gate[l], moe_w_up[l], moe_w_down[l])
        if ctx_out:
            xc = xc + mod_c[2] * (mc @ w_out[l])
            hc = modulate(rms_norm(xc, norm2_g[l]), mod_c[3], mod_c[4])
            xc = xc + mod_c[5] * grouped_moe(hc, router_w, router_b, moe_w_gate[l], moe_w_up[l], moe_w_down[l])
    return rms_norm(xl, final_g)
```

```python
import functools

import numpy as np
import jax
import jax.numpy as jnp
from jax import lax
from jax.experimental import pallas as pl
from jax.experimental.pallas import tpu as pltpu

F32 = jnp.float32
BF16 = jnp.bfloat16
HIGHEST = lax.Precision.HIGHEST

GRID_W = 64
POOL_WINDOWS = (2, 4, 8, 16)
GROUP_DIM = 64
N_GROUPS = 4
MIX_PART = N_GROUPS * GROUP_DIM
NA_WIN_ROWS = 8
NA_WIN_COLS = 16
NA_Q_ROWS = 4
NA_K_ROWS = NA_Q_ROWS + NA_WIN_ROWS
SGU_CHUNK = 128
MLA_Q_RANK = 256
MLA_KV_RANK = 128
MLA_NOPE = 64
MLA_ROPE = 32
MLA_V = 64
MLA_SLOT = 128
MLA_SCALE = (MLA_NOPE + MLA_ROPE) ** -0.5
ROPE_BASE = 10000.0
N_EXPERTS = 16
N_EXPERT_GROUPS = 4
EXPERTS_PER_GROUP = 4
D_EXPERT = 256
N_MOD = 6
EPS = 1e-6
NEG_INF = -1e30

OFF_B = MIX_PART
OFF_C = OFF_B + 3 * MIX_PART
OFF_D = OFF_C + 2 * MIX_PART
OFF_KR = OFF_D + MLA_Q_RANK + MLA_KV_RANK
W_IN_COLS = OFF_KR + 2 * MLA_SLOT

TOKEN_TILE = 512
ATTN_TILE = 256
MLA_KV_TILE = 512
MOE_TILE = 512
VMEM_LIMIT = 56 * 1024 * 1024


def _cparams(*sem):
    return pltpu.CompilerParams(dimension_semantics=sem, vmem_limit_bytes=VMEM_LIMIT)


def _dot(a, b):
    return jnp.dot(a, b, preferred_element_type=F32)


def _dot_nt(a, b, precision=None):
    return lax.dot_general(a, b, (((1,), (1,)), ((), ())), precision=precision,
                           preferred_element_type=F32)


def _rms(x, g):
    return x * lax.rsqrt(jnp.mean(x * x, axis=-1, keepdims=True) + EPS) * g


def _silu(x):
    return x * jax.nn.sigmoid(x)


def _mod_body(c_ref, w_ref, b_ref, o_ref):
    o_ref[0] = jnp.dot(_silu(c_ref[...]), w_ref[0], precision=HIGHEST,
                       preferred_element_type=F32) + b_ref[0]


def _modulation(c_all, ada_w, ada_b):
    L, D, ND = ada_w.shape
    tn = 1536
    return pl.pallas_call(
        _mod_body,
        grid=(L, ND // tn),
        in_specs=[pl.BlockSpec((8, D), lambda l, j: (0, 0)),
                  pl.BlockSpec((1, D, tn), lambda l, j: (l, 0, j)),
                  pl.BlockSpec((1, 1, tn), lambda l, j: (l, 0, j))],
        out_specs=pl.BlockSpec((1, 8, tn), lambda l, j: (l, 0, j)),
        out_shape=jax.ShapeDtypeStruct((L, 8, ND), F32),
        compiler_params=_cparams("arbitrary", "arbitrary"),
        name="modulation",
    )(c_all, ada_w, ada_b.reshape(L, 1, ND))


def _inproj_body(has_res, *refs):
    if has_res:
        (x_ref, y_ref, mp_ref, m_ref, g1_ref, w_ref, wq_ref, gq_ref, wkv_ref, gkv_ref, cos_ref, sin_ref,
         xo_ref, pa_ref, qb_ref, kb_ref, vb_ref, pc_ref, qd_ref, kd_ref, vd_ref) = refs
        x = x_ref[...] + mp_ref[0, 5:6, :] * y_ref[...].astype(F32)
        xo_ref[...] = x
    else:
        (x_ref, m_ref, g1_ref, w_ref, wq_ref, gq_ref, wkv_ref, gkv_ref, cos_ref, sin_ref,
         pa_ref, qb_ref, kb_ref, vb_ref, pc_ref, qd_ref, kd_ref, vd_ref) = refs
        x = x_ref[...]
    m = m_ref[0]
    hb = (_rms(x, g1_ref[...]) * (1.0 + m[1:2]) + m[0:1]).astype(BF16)

    def proj(a, b):
        return _dot(hb, w_ref[:, a:b])

    pa_ref[...] = proj(0, OFF_B)
    qb_ref[...] = (proj(OFF_B, OFF_B + MIX_PART) * (GROUP_DIM ** -0.5)).astype(BF16)
    kb_ref[...] = proj(OFF_B + MIX_PART, OFF_B + 2 * MIX_PART).astype(BF16)
    vb_ref[...] = proj(OFF_B + 2 * MIX_PART, OFF_C).astype(BF16)
    pc_ref[...] = proj(OFF_C, OFF_D)

    cosp = cos_ref[...]
    sinp = sin_ref[...]
    qn = _rms(proj(OFF_D, OFF_D + MLA_Q_RANK), gq_ref[...]).astype(BF16)
    qq = _dot(qn, wq_ref[...])
    half = N_GROUPS * MLA_SLOT
    for h in range(N_GROUPS):
        a = h * MLA_SLOT
        q = qq[:, a:a + MLA_SLOT] * cosp + qq[:, half + a:half + a + MLA_SLOT] * sinp
        qd_ref[:, a:a + MLA_SLOT] = (q * MLA_SCALE).astype(BF16)
    kvn = _rms(proj(OFF_D + MLA_Q_RANK, OFF_KR), gkv_ref[...]).astype(BF16)
    kk = _dot(kvn, wkv_ref[...])
    kr = proj(OFF_KR, OFF_KR + MLA_SLOT) * cosp + proj(OFF_KR + MLA_SLOT, W_IN_COLS) * sinp
    for h in range(N_GROUPS):
        a = h * MLA_SLOT
        kd_ref[:, a:a + MLA_SLOT] = (kk[:, a:a + MLA_SLOT] + kr).astype(BF16)
    vd_ref[...] = kk[:, half:].astype(BF16)


def _in_projection(l, n_lat_tiles_per_batch, n_batch, x, y, mod_r, g1, w, wq, gq, wkv, gkv, cosp, sinp):
    n, d = x.shape
    tm = TOKEN_TILE
    has_res = y is not None

    def row(i):
        return (i, 0)

    def modrow(layer):
        return lambda i: (layer * 8 + jnp.minimum(i // n_lat_tiles_per_batch, n_batch), 0, 0)

    def const2(i):
        return (0, 0)

    tok = lambda c: pl.BlockSpec((tm, c), row)
    mod_spec = lambda layer: pl.BlockSpec((1, N_MOD, d), modrow(layer))
    in_specs = [tok(d)]
    args = [x]
    if has_res:
        in_specs += [tok(d), mod_spec(l - 1)]
        args += [y, mod_r]
    in_specs += [mod_spec(l), pl.BlockSpec((1, d), const2), pl.BlockSpec(w.shape, const2),
                 pl.BlockSpec(wq.shape, const2), pl.BlockSpec(gq.shape, const2),
                 pl.BlockSpec(wkv.shape, const2), pl.BlockSpec(gkv.shape, const2),
                 tok(MLA_SLOT), tok(MLA_SLOT)]
    args += [mod_r, g1, w, wq, gq, wkv, gkv, cosp, sinp]
    outs = [(MIX_PART, F32), (MIX_PART, BF16), (MIX_PART, BF16), (MIX_PART, BF16), (2 * MIX_PART, F32),
            (N_GROUPS * MLA_SLOT, BF16), (N_GROUPS * MLA_SLOT, BF16), (N_GROUPS * MLA_V, BF16)]
    if has_res:
        outs = [(d, F32)] + outs
    res = pl.pallas_call(
        functools.partial(_inproj_body, has_res),
        grid=(n // tm,),
        in_specs=in_specs,
        out_specs=[tok(c) for c, _ in outs],
        out_shape=[jax.ShapeDtypeStruct((n, c), t) for c, t in outs],
        compiler_params=_cparams("arbitrary"),
        name="in_projection",
    )(*args)
    if not has_res:
        res = [x] + list(res)
    return res


def _pool_body(n_lat, seq, ctx_len, prev_ref, cur_ref, next_ref, w_ref, s_ref, o_ref):
    tb = cur_ref.shape[0]
    ext = jnp.concatenate([prev_ref[...], cur_ref[...], next_ref[...]], axis=0)
    n = tb + 16
    g = pl.program_id(0) * tb - 8 + lax.broadcasted_iota(jnp.int32, (n, 1), 0)
    is_lat = g < n_lat
    length = jnp.where(is_lat, seq, ctx_len)
    p = jnp.where(is_lat, g & (seq - 1), (g - n_lat) & (ctx_len - 1))

    def shifted(a, j):
        r = pltpu.roll(a, (-j) % n, axis=0)
        ok = (p + j >= 0) & (p + j < length)
        return jnp.where(ok, r, 0.0)

    before1 = shifted(ext, -1)
    before2 = before1 + shifted(before1, -1)
    before4 = before2 + shifted(before2, -2)
    before8 = before4 + shifted(before4, -4)
    after2 = ext + shifted(ext, 1)
    after4 = after2 + shifted(after2, 2)
    after8 = after4 + shifted(after4, 4)
    lane_grp = lax.broadcasted_iota(jnp.int32, (1, MIX_PART), 1) // GROUP_DIM
    sl = slice(8, 8 + tb)
    tot = jnp.where(lane_grp == 0, (before1 + ext)[sl],
                    jnp.where(lane_grp == 1, (before2 + after2)[sl],
                              jnp.where(lane_grp == 2, (before4 + after4)[sl], (before8 + after8)[sl])))
    half = jnp.where(lane_grp == 0, POOL_WINDOWS[0] // 2,
                     jnp.where(lane_grp == 1, POOL_WINDOWS[1] // 2,
                               jnp.where(lane_grp == 2, POOL_WINDOWS[2] // 2, POOL_WINDOWS[3] // 2)))
    pc = p[sl]
    cnt = jnp.minimum(pc + half, length[sl]) - jnp.maximum(pc - half, 0)
    dlt = tot / cnt.astype(F32) - ext[sl]
    o_ref[...] = (_dot(dlt.astype(BF16), w_ref[...]) * s_ref[...]).astype(BF16)


def _pool_mixer(n_lat, seq, ctx_len, pa, w_bd, s):
    n = pa.shape[0]
    tb = 1024
    assert seq & (seq - 1) == 0 and ctx_len & (ctx_len - 1) == 0
    assert n_lat % tb == 0 and n % tb == 0
    nb8 = n // 8
    return pl.pallas_call(
        functools.partial(_pool_body, n_lat, seq, ctx_len),
        grid=(n // tb,),
        in_specs=[pl.BlockSpec((8, MIX_PART), lambda i: (jnp.maximum(i * (tb // 8) - 1, 0), 0)),
                  pl.BlockSpec((tb, MIX_PART), lambda i: (i, 0)),
                  pl.BlockSpec((8, MIX_PART), lambda i: (jnp.minimum((i + 1) * (tb // 8), nb8 - 1), 0)),
                  pl.BlockSpec(w_bd.shape, lambda i: (0, 0)),
                  pl.BlockSpec(s.shape, lambda i: (0, 0))],
        out_specs=pl.BlockSpec((tb, MIX_PART), lambda i: (i, 0)),
        out_shape=jax.ShapeDtypeStruct((n, MIX_PART), BF16),
        compiler_params=_cparams("arbitrary"),
        name="pool_mixer",
    )(pa, pa, pa, w_bd, s)


def _softmax_pv(parts):
    m = parts[0][0].max(axis=-1, keepdims=True)
    for s, _ in parts[1:]:
        m = jnp.maximum(m, s.max(axis=-1, keepdims=True))
    den = 0.0
    out = 0.0
    for s, v in parts:
        p = jnp.exp(s - m)
        den = den + p.sum(axis=-1, keepdims=True)
        out = out + _dot(p.astype(BF16), v)
    return out / den


def _na_body(rows, n_qt, q_ref, kl_ref, kc_ref, vl_ref, vc_ref, bias_ref, o_ref):
    qt = pl.program_id(1)
    q = q_ref[...]
    kc = kc_ref[...]
    vc = vc_ref[...]
    lane_grp = lax.broadcasted_iota(jnp.int32, (1, MIX_PART), 1) // GROUP_DIM
    zero = jnp.zeros_like(q)

    @pl.when(qt < n_qt)
    def _():
        k0 = jnp.clip(NA_Q_ROWS * qt - NA_WIN_ROWS // 2, 0, rows - NA_K_ROWS)
        off = pl.multiple_of(k0 * GRID_W, NA_Q_ROWS * GRID_W)
        kw = kl_ref[pl.ds(off, NA_K_ROWS * GRID_W), :]
        vw = vl_ref[pl.ds(off, NA_K_ROWS * GRID_W), :]
        o = jnp.zeros(q.shape, F32)
        for h in range(N_GROUPS):
            qh = jnp.where(lane_grp == h, q, zero)
            oh = _softmax_pv([(_dot_nt(qh, kw) + bias_ref[0, h], vw), (_dot_nt(qh, kc), vc)])
            o = jnp.where(lane_grp == h, oh, o)
        o_ref[...] = o.astype(BF16)

    @pl.when(qt == n_qt)
    def _():
        o = jnp.zeros(q.shape, F32)
        for h in range(N_GROUPS):
            qh = jnp.where(lane_grp == h, q, zero)
            oh = _softmax_pv([(_dot_nt(qh, kc), vc)])
            o = jnp.where(lane_grp == h, oh, o)
        o_ref[...] = o.astype(BF16)


def _attn_specs(n_batch, seq, ctx_len, width_q, width_k, width_v):
    n_qt = seq // ATTN_TILE
    lat_blocks = n_batch * n_qt
    assert ctx_len == ATTN_TILE

    def qmap(b, t):
        return (jnp.where(t < n_qt, b * n_qt + t, lat_blocks + b), 0)

    q_spec = pl.BlockSpec((ATTN_TILE, width_q), qmap)
    kl_spec = pl.BlockSpec((seq, width_k), lambda b, t: (b, 0))
    kc_spec = pl.BlockSpec((ctx_len, width_k), lambda b, t: (lat_blocks + b, 0))
    vl_spec = pl.BlockSpec((seq, width_v), lambda b, t: (b, 0))
    vc_spec = pl.BlockSpec((ctx_len, width_v), lambda b, t: (lat_blocks + b, 0))
    o_spec = pl.BlockSpec((ATTN_TILE, width_v), qmap)
    return n_qt, [q_spec, kl_spec, kc_spec, vl_spec, vc_spec], o_spec


def _na_mixer(l, n_batch, seq, ctx_len, qb, kb, vb, bias_tab):
    n = qb.shape[0]
    rows = seq // GRID_W
    n_qt, in_specs, o_spec = _attn_specs(n_batch, seq, ctx_len, MIX_PART, MIX_PART, MIX_PART)
    assert ATTN_TILE == NA_Q_ROWS * GRID_W and rows >= NA_K_ROWS + NA_Q_ROWS

    def bias_map(b, t):
        kind = jnp.where(t == 0, 0, jnp.where(t >= n_qt - 1, 2, 1))
        return (l * 3 + kind, 0, 0, 0)

    in_specs.append(pl.BlockSpec((1,) + bias_tab.shape[1:], bias_map))
    return pl.pallas_call(
        functools.partial(_na_body, rows, n_qt),
        grid=(n_batch, n_qt + 1),
        in_specs=in_specs,
        out_specs=o_spec,
        out_shape=jax.ShapeDtypeStruct((n, MIX_PART), BF16),
        compiler_params=_cparams("arbitrary", "arbitrary"),
        name="neighborhood_attention",
    )(qb, kb, kb, vb, vb, bias_tab)


def _na_bias_tables(na_rpb, rows):
    L, H = na_rpb.shape[:2]
    kinds = ((0, 0), (NA_Q_ROWS, 0), (rows - NA_Q_ROWS, rows - NA_K_ROWS))
    qc = np.arange(GRID_W)
    kc = np.arange(GRID_W)
    wsc = np.clip(qc - NA_WIN_COLS // 2, 0, GRID_W - NA_WIN_COLS)
    col_ok = (kc[None, :] >= wsc[:, None]) & (kc[None, :] < wsc[:, None] + NA_WIN_COLS)
    dc = np.clip(kc[None, :] - qc[:, None] + NA_WIN_COLS - 1, 0, 2 * NA_WIN_COLS - 2)
    tabs = []
    for r0, k0 in kinds:
        r = r0 + np.arange(NA_Q_ROWS)
        kr = k0 + np.arange(NA_K_ROWS)
        wsr = np.clip(r - NA_WIN_ROWS // 2, 0, rows - NA_WIN_ROWS)
        row_ok = (kr[None, :] >= wsr[:, None]) & (kr[None, :] < wsr[:, None] + NA_WIN_ROWS)
        dr = np.clip(kr[None, :] - r[:, None] + NA_WIN_ROWS - 1, 0, 2 * NA_WIN_ROWS - 2)
        ok = row_ok[:, None, :, None] & col_ok[None, :, None, :]
        dr4 = np.broadcast_to(dr[:, None, :, None], ok.shape)
        dc4 = np.broadcast_to(dc[None, :, None, :], ok.shape)
        t = jnp.where(ok, na_rpb[:, :, dr4, dc4], NEG_INF)
        tabs.append(t.reshape(L, H, NA_Q_ROWS * GRID_W, NA_K_ROWS * GRID_W))
    return jnp.stack(tabs, axis=1).reshape(L * 3, H, NA_Q_ROWS * GRID_W, NA_K_ROWS * GRID_W)


def _gelu_tanh(x):
    return 0.5 * x * (1.0 + jnp.tanh(np.sqrt(2.0 / np.pi).astype(np.float32) * (x + 0.044715 * (x * x * x))))


def _sgu_body(pc_ref, gn_ref, ones_ref, w_ref, b_ref, o_ref):
    tm = pc_ref.shape[0]
    uv = _gelu_tanh(pc_ref[...])
    u = uv[:, :MIX_PART]
    v = uv[:, MIX_PART:]
    ms = jnp.dot(v * v, ones_ref[...], precision=HIGHEST, preferred_element_type=F32)
    vg = (v * lax.rsqrt(ms + EPS) * gn_ref[...]).astype(BF16)
    lane_grp = lax.broadcasted_iota(jnp.int32, (1, MIX_PART), 1) // GROUP_DIM
    w = w_ref[...]
    for c in range(tm // SGU_CHUNK):
        rs = slice(c * SGU_CHUNK, (c + 1) * SGU_CHUNK)
        r = _dot(w, vg[rs])
        mixed = r[:SGU_CHUNK]
        for g in range(1, N_GROUPS):
            mixed = jnp.where(lane_grp == g, r[g * SGU_CHUNK:(g + 1) * SGU_CHUNK], mixed)
        o_ref[rs, :] = (u[rs] * (mixed + b_ref[...])).astype(BF16)


def _sgu_mixer(pc, gn, ones_bd, w_all, b_exp):
    n = pc.shape[0]
    tm = TOKEN_TILE
    const = lambda i: (0, 0)
    return pl.pallas_call(
        _sgu_body,
        grid=(n // tm,),
        in_specs=[pl.BlockSpec((tm, 2 * MIX_PART), lambda i: (i, 0)),
                  pl.BlockSpec(gn.shape, const), pl.BlockSpec(ones_bd.shape, const),
                  pl.BlockSpec(w_all.shape, const), pl.BlockSpec(b_exp.shape, const)],
        out_specs=pl.BlockSpec((tm, MIX_PART), lambda i: (i, 0)),
        out_shape=jax.ShapeDtypeStruct((n, MIX_PART), BF16),
        compiler_params=_cparams("arbitrary"),
        name="spatial_gating",
    )(pc, gn, ones_bd, w_all, b_exp)


def _mla_body(n_qt, n_kv, q_ref, kl_ref, kc_ref, vl_ref, vc_ref, o_ref):
    qt = pl.program_id(1)
    n_lat = jnp.where(qt < n_qt, n_kv, 0)
    lane_grp = lax.broadcasted_iota(jnp.int32, (1, N_GROUPS * MLA_V), 1) // MLA_V
    vc = vc_ref[...]
    o = jnp.zeros((ATTN_TILE, N_GROUPS * MLA_V), F32)
    for h in range(N_GROUPS):
        hs = slice(h * MLA_SLOT, (h + 1) * MLA_SLOT)
        q = q_ref[:, hs]
        s = _dot_nt(q, kc_ref[:, hs])
        m = s.max(axis=-1, keepdims=True)
        p = jnp.exp(s - m)
        den = p.sum(axis=-1, keepdims=True)
        acc = _dot(p.astype(BF16), vc)

        def step(j, carry):
            m, den, acc = carry
            off = pl.multiple_of(j * MLA_KV_TILE, MLA_KV_TILE)
            s = _dot_nt(q, kl_ref[pl.ds(off, MLA_KV_TILE), hs])
            mn = jnp.maximum(m, s.max(axis=-1, keepdims=True))
            a = jnp.exp(m - mn)
            p = jnp.exp(s - mn)
            den = a * den + p.sum(axis=-1, keepdims=True)
            acc = a * acc + _dot(p.astype(BF16), vl_ref[pl.ds(off, MLA_KV_TILE), :])
            return mn, den, acc

        m, den, acc = lax.fori_loop(0, n_lat, step, (m, den, acc))
        o = jnp.where(lane_grp == h, acc / den, o)
    o_ref[...] = o.astype(BF16)


def _mla_mixer(n_batch, seq, ctx_len, qd, kd, vd):
    n = qd.shape[0]
    n_qt, in_specs, o_spec = _attn_specs(n_batch, seq, ctx_len, N_GROUPS * MLA_SLOT, N_GROUPS * MLA_SLOT,
                                         N_GROUPS * MLA_V)
    assert seq % MLA_KV_TILE == 0
    return pl.pallas_call(
        functools.partial(_mla_body, n_qt, seq // MLA_KV_TILE),
        grid=(n_batch, n_qt + 1),
        in_specs=in_specs,
        out_specs=o_spec,
        out_shape=jax.ShapeDtypeStruct((n, N_GROUPS * MLA_V), BF16),
        compiler_params=_cparams("arbitrary", "arbitrary"),
        name="latent_attention",
    )(qd, kd, kd, vd, vd)


def _outproj_body(ya_ref, yb_ref, yc_ref, yd_ref, x_ref, m_ref, g2_ref, w_ref, rw_ref, rb_ref,
                  xo_ref, h_ref, r_ref):
    tm = x_ref.shape[0]
    m = m_ref[0]
    y = _dot(ya_ref[...], w_ref[0:MIX_PART, :])
    for k, ref in enumerate((yb_ref, yc_ref, yd_ref), start=1):
        y = y + _dot(ref[...], w_ref[k * MIX_PART:(k + 1) * MIX_PART, :])
    x = x_ref[...] + m[2:3] * y
    xo_ref[...] = x
    h = _rms(x, g2_ref[...]) * (1.0 + m[4:5]) + m[3:4]
    h_ref[...] = h.astype(BF16)

    scores = jax.nn.sigmoid(_dot_nt(rw_ref[...], h, precision=HIGHEST))
    biased = scores + rb_ref[...]
    E = EXPERTS_PER_GROUP
    bk = [biased[8 * k:8 * k + 8] for k in range(E)]
    sk = [scores[8 * k:8 * k + 8] for k in range(E)]
    gs = None
    for a in range(E):
        for b in range(a + 1, E):
            pair = bk[a] + bk[b]
            gs = pair if gs is None else jnp.maximum(gs, pair)
    best = gs[0:1]
    idx = jnp.zeros((1, tm), jnp.int32)
    for g in range(1, N_EXPERT_GROUPS):
        better = gs[g:g + 1] > best
        idx = jnp.where(better, g, idx)
        best = jnp.where(better, gs[g:g + 1], best)
    in_grp = lax.broadcasted_iota(jnp.int32, (8, tm), 0) == idx
    wk = []
    for k in range(E):
        rank = jnp.zeros((8, tm), jnp.int32)
        for j in range(E):
            if j != k:
                ahead = (bk[j] > bk[k]) | ((bk[j] == bk[k]) & (j < k))
                rank = rank + ahead.astype(jnp.int32)
        wk.append(jnp.where((rank < 2) & in_grp, sk[k], 0.0).sum(axis=0, keepdims=True))
    den = wk[0] + wk[1] + wk[2] + wk[3]
    ri = lax.broadcasted_iota(jnp.int32, (8, tm), 0)
    out = jnp.where(ri == E, idx.astype(F32), 0.0)
    for k in range(E):
        out = jnp.where(ri == k, wk[k] / den, out)
    r_ref[...] = out


def _out_projection(l, n_lat_tiles_per_batch, n_batch, ys, x, mod_r, g2, w, rw, rb):
    n, d = x.shape
    tm = TOKEN_TILE
    row = lambda i: (i, 0)
    const = lambda i: (0, 0)
    modrow = lambda i: (l * 8 + jnp.minimum(i // n_lat_tiles_per_batch, n_batch), 0, 0)
    in_specs = [pl.BlockSpec((tm, MIX_PART), row)] * 4 + [
        pl.BlockSpec((tm, d), row), pl.BlockSpec((1, N_MOD, d), modrow), pl.BlockSpec((1, d), const),
        pl.BlockSpec(w.shape, const), pl.BlockSpec(rw.shape, const), pl.BlockSpec(rb.shape, const)]
    return pl.pallas_call(
        _outproj_body,
        grid=(n // tm,),
        in_specs=in_specs,
        out_specs=[pl.BlockSpec((tm, d), row), pl.BlockSpec((tm, d), row), pl.BlockSpec((8, tm), lambda i: (0, i))],
        out_shape=[jax.ShapeDtypeStruct((n, d), F32), jax.ShapeDtypeStruct((n, d), BF16),
                   jax.ShapeDtypeStruct((8, n), F32)],
        compiler_params=_cparams("arbitrary"),
        name="out_projection_routing",
    )(*ys, x, mod_r, g2, w, rw, rb)


def _moe_body(tg_ref, tv_ref, h_ref, g_ref, wg_ref, wu_ref, wd_ref, o_ref, wgb, wub, wdb):
    i = pl.program_id(0)
    grp = tg_ref[i]
    prev = tg_ref[jnp.maximum(i - 1, 0)]

    @pl.when((i == 0) | (grp != prev))
    def _():
        wgb[...] = wg_ref[0].astype(BF16)
        wub[...] = wu_ref[0].astype(BF16)
        wdb[...] = wd_ref[0].astype(BF16)

    @pl.when(tv_ref[i] > 0)
    def _():
        h = h_ref[...]
        acc = jnp.zeros(o_ref.shape, F32)
        for k in range(EXPERTS_PER_GROUP):
            hid = _silu(_dot(h, wgb[k])) * _dot(h, wub[k]) * g_ref[:, k:k + 1]
            acc = acc + _dot(hid.astype(BF16), wdb[k])
        o_ref[...] = acc.astype(o_ref.dtype)

    @pl.when(tv_ref[i] == 0)
    def _():
        o_ref[...] = jnp.zeros(o_ref.shape, o_ref.dtype)


def _grouped_experts(l, tile_grp, tile_ok, hs, gs, w_gate, w_up, w_down):
    npad, d = hs.shape
    tm = MOE_TILE
    E = EXPERTS_PER_GROUP
    wmap = lambda i, tg, tv: (l, tg[i], 0, 0)
    grid_spec = pltpu.PrefetchScalarGridSpec(
        num_scalar_prefetch=2,
        grid=(npad // tm,),
        in_specs=[pl.BlockSpec((tm, d), lambda i, tg, tv: (i, 0)),
                  pl.BlockSpec((tm, E), lambda i, tg, tv: (i, 0)),
                  pl.BlockSpec((1, E, d, D_EXPERT), wmap),
                  pl.BlockSpec((1, E, d, D_EXPERT), wmap),
                  pl.BlockSpec((1, E, D_EXPERT, d), wmap)],
        out_specs=pl.BlockSpec((tm, d), lambda i, tg, tv: (i, 0)),
        scratch_shapes=[pltpu.VMEM((E, d, D_EXPERT), BF16), pltpu.VMEM((E, d, D_EXPERT), BF16),
                        pltpu.VMEM((E, D_EXPERT, d), BF16)])
    return pl.pallas_call(
        _moe_body,
        grid_spec=grid_spec,
        out_shape=jax.ShapeDtypeStruct((npad, d), BF16),
        compiler_params=_cparams("arbitrary"),
        name="grouped_experts",
    )(tile_grp, tile_ok, hs, gs, w_gate, w_up, w_down)


def _dispatch_plan(route, n_pad):
    n = route.shape[1]
    tm = MOE_TILE
    grp = route[EXPERTS_PER_GROUP].astype(jnp.int32)
    onehot = (grp[:, None] == jnp.arange(N_EXPERT_GROUPS)[None, :]).astype(jnp.int32)
    csum = jnp.cumsum(onehot, axis=0)
    counts = csum[-1]
    rank = jnp.take_along_axis(csum, grp[:, None], axis=1)[:, 0] - 1
    padded = ((counts + tm - 1) // tm) * tm
    ends = jnp.cumsum(padded)
    pos = (ends - padded)[grp] + rank
    perm = jnp.zeros((n_pad,), jnp.int32).at[pos].set(jnp.arange(n, dtype=jnp.int32))
    tile_start = jnp.arange(n_pad // tm, dtype=jnp.int32) * tm
    tile_grp = jnp.minimum(jnp.sum(tile_start[:, None] >= ends[None, :], axis=1), N_EXPERT_GROUPS - 1)
    tile_ok = (tile_start < ends[-1]).astype(jnp.int32)
    gates = jnp.take(route[:EXPERTS_PER_GROUP].T, perm, axis=0)
    return pos, perm, tile_grp.astype(jnp.int32), tile_ok, gates


def _final_body(x_ref, y_ref, m_ref, g_ref, o_ref):
    x = x_ref[...] + m_ref[0, 5:6, :] * y_ref[...].astype(F32)
    o_ref[...] = _rms(x, g_ref[...])


def _final_norm(l, n_lat, n_lat_tiles_per_batch, x, y, mod_r, g):
    d = x.shape[1]
    tm = TOKEN_TILE
    row = lambda i: (i, 0)
    return pl.pallas_call(
        _final_body,
        grid=(n_lat // tm,),
        in_specs=[pl.BlockSpec((tm, d), row), pl.BlockSpec((tm, d), row),
                  pl.BlockSpec((1, N_MOD, d), lambda i: (l * 8 + i // n_lat_tiles_per_batch, 0, 0)),
                  pl.BlockSpec((1, d), lambda i: (0, 0))],
        out_specs=pl.BlockSpec((tm, d), row),
        out_shape=jax.ShapeDtypeStruct((n_lat, d), F32),
        compiler_params=_cparams("arbitrary"),
        name="final_norm",
    )(x, y, mod_r, g)


def _rope_tables(seq, n_batch, n_ctx_tokens):
    t = np.arange(seq)
    nf = MLA_ROPE // 4
    inv = jnp.asarray(ROPE_BASE, F32) ** (-jnp.arange(nf, dtype=F32) / nf)
    rang = jnp.asarray(t // GRID_W, F32)[:, None] * inv
    cang = jnp.asarray(t % GRID_W, F32)[:, None] * inv
    cr, sr, cc, sc = jnp.cos(rang), jnp.sin(rang), jnp.cos(cang), jnp.sin(cang)
    one = jnp.ones((seq, MLA_NOPE), F32)
    zero_tail = jnp.zeros((seq, MLA_SLOT - MLA_NOPE - MLA_ROPE), F32)
    cosp = jnp.concatenate([one, cr, cr, cc, cc, zero_tail], axis=1)
    sinp = jnp.concatenate([0 * one, -sr, sr, -sc, sc, zero_tail], axis=1)
    ctx_cos = jnp.concatenate([jnp.ones((n_ctx_tokens, MLA_NOPE + MLA_ROPE), F32),
                               jnp.zeros((n_ctx_tokens, MLA_SLOT - MLA_NOPE - MLA_ROPE), F32)], axis=1)
    cosp = jnp.concatenate([jnp.tile(cosp, (n_batch, 1)), ctx_cos], axis=0)
    sinp = jnp.concatenate([jnp.tile(sinp, (n_batch, 1)), jnp.zeros((n_ctx_tokens, MLA_SLOT), F32)], axis=0)
    return cosp, sinp


_ROPE_SWAP = np.concatenate([np.arange(8, 16), np.arange(0, 8), np.arange(24, 32), np.arange(16, 24)])


def _pad_last(a, before, after):
    return jnp.pad(a, [(0, 0)] * (a.ndim - 1) + [(before, after)])


def _layouts(w_in, w_out, pool_w, sgu_w, sgu_b, mla_w_uq, mla_w_ukv, router_w, router_b):
    L = w_in.shape[0]
    tail = MLA_SLOT - MLA_NOPE - MLA_ROPE
    kr = w_in[:, :, OFF_KR:]
    w = jnp.concatenate([w_in[:, :, :OFF_KR], _pad_last(kr, MLA_NOPE, tail),
                         _pad_last(kr[:, :, _ROPE_SWAP], MLA_NOPE, tail)], axis=-1).astype(BF16)
    uq = mla_w_uq.reshape(L, MLA_Q_RANK, N_GROUPS, MLA_NOPE + MLA_ROPE)
    q1 = _pad_last(uq, 0, tail).reshape(L, MLA_Q_RANK, N_GROUPS * MLA_SLOT)
    q2 = _pad_last(uq[..., MLA_NOPE:][..., _ROPE_SWAP], MLA_NOPE, tail).reshape(L, MLA_Q_RANK, N_GROUPS * MLA_SLOT)
    wq = jnp.concatenate([q1, q2], axis=-1).astype(BF16)
    ukv = mla_w_ukv.reshape(L, MLA_KV_RANK, N_GROUPS, MLA_NOPE + MLA_V)
    kn = _pad_last(ukv[..., :MLA_NOPE], 0, MLA_SLOT - MLA_NOPE).reshape(L, MLA_KV_RANK, N_GROUPS * MLA_SLOT)
    vv = ukv[..., MLA_NOPE:].reshape(L, MLA_KV_RANK, N_GROUPS * MLA_V)
    wkv = jnp.concatenate([kn, vv], axis=-1).astype(BF16)
    eye = jnp.eye(N_GROUPS, dtype=F32)
    pool_bd = jnp.einsum('lgcd,gh->lgchd', pool_w, eye).reshape(L, MIX_PART, MIX_PART).astype(BF16)
    ones_bd = jnp.asarray(np.kron(np.eye(N_GROUPS), np.full((GROUP_DIM, GROUP_DIM), 1.0 / GROUP_DIM)), F32)
    sgu_all = sgu_w.reshape(L, N_GROUPS * SGU_CHUNK, SGU_CHUNK).astype(BF16)
    sgu_bias = jnp.repeat(jnp.swapaxes(sgu_b, 1, 2), GROUP_DIM, axis=2)
    rw = router_w.T.reshape(N_EXPERT_GROUPS, EXPERTS_PER_GROUP, -1).transpose(1, 0, 2)
    rw = jnp.pad(rw, ((0, 0), (0, 8 - N_EXPERT_GROUPS), (0, 0))).reshape(8 * EXPERTS_PER_GROUP, -1)
    rb = router_b.reshape(N_EXPERT_GROUPS, EXPERTS_PER_GROUP).T
    rb = jnp.pad(rb, ((0, 0), (0, 8 - N_EXPERT_GROUPS))).reshape(8 * EXPERTS_PER_GROUP, 1)
    return w, wq, wkv, w_out.astype(BF16), pool_bd, ones_bd, sgu_all, sgu_bias, rw, rb


def kernel(x, c, ctx, c_ctx, ada_w, ada_b, norm1_g, norm2_g, w_in, w_out, pool_w, pool_s, na_rpb,
           sgu_norm_g, sgu_w, sgu_b, mla_q_norm_g, mla_w_uq, mla_kv_norm_g, mla_w_ukv,
           router_w, router_b, moe_w_gate, moe_w_up, moe_w_down, final_g):
    B, S, D = x.shape
    CTX = ctx.shape[1]
    L = ada_w.shape[0]
    n_lat, n_ctx = B * S, B * CTX
    n_all = n_lat + n_ctx
    assert B < 8 and S % TOKEN_TILE == 0 and n_ctx % TOKEN_TILE == 0
    tiles_per_batch = S // TOKEN_TILE
    n_pad = n_all + N_EXPERT_GROUPS * MOE_TILE

    c_all = jnp.concatenate([c, c_ctx[None, :], jnp.zeros((8 - B - 1, D), F32)], axis=0)
    mod_r = _modulation(c_all, ada_w, ada_b).reshape(L * 8, N_MOD, D)
    w, wq, wkv, wo, pool_bd, ones_bd, sgu_all, sgu_bias, rw, rb = _layouts(
        w_in, w_out, pool_w, sgu_w, sgu_b, mla_w_uq, mla_w_ukv, router_w, router_b)
    bias_tab = _na_bias_tables(na_rpb, S // GRID_W)
    cosp, sinp = _rope_tables(S, B, n_ctx)

    xs = jnp.concatenate([x.reshape(n_lat, D), ctx.reshape(n_ctx, D)], axis=0)
    y = None
    for l in range(L):
        xs, pa, qb, kb, vb, pc, qd, kd, vd = _in_projection(
            l, tiles_per_batch, B, xs, y, mod_r, norm1_g[l][None], w[l], wq[l], mla_q_norm_g[l][None],
            wkv[l], mla_kv_norm_g[l][None], cosp, sinp)
        ya = _pool_mixer(n_lat, S, CTX, pa, pool_bd[l], pool_s[l][None])
        yb = _na_mixer(l, B, S, CTX, qb, kb, vb, bias_tab)
        yc = _sgu_mixer(pc, sgu_norm_g[l][None], ones_bd, sgu_all[l], sgu_bias[l])
        yd = _mla_mixer(B, S, CTX, qd, kd, vd)
        xs, h, route = _out_projection(l, tiles_per_batch, B, (ya, yb, yc, yd), xs, mod_r, norm2_g[l][None],
                                       wo[l], rw, rb)
        pos, perm, tile_grp, tile_ok, gates = _dispatch_plan(route, n_pad)
        ysorted = _grouped_experts(l, tile_grp, tile_ok, jnp.take(h, perm, axis=0), gates,
                                   moe_w_gate, moe_w_up, moe_w_down)
        y = jnp.take(ysorted, pos, axis=0)
    out = _final_norm(L - 1, n_lat, tiles_per_batch, xs, y, mod_r, final_g[None])
    return out.reshape(B, S, D)
```

```python
import functools

import numpy as np
import jax
import jax.numpy as jnp
from jax import lax
from jax.experimental import pallas as pl
from jax.experimental.pallas import tpu as pltpu

F32 = jnp.float32
BF16 = jnp.bfloat16
HIGHEST = lax.Precision.HIGHEST

GRID_W = 64
POOL_WINDOWS = (2, 4, 8, 16)
GROUP_DIM = 64
N_GROUPS = 4
MIX_PART = N_GROUPS * GROUP_DIM
NA_WIN_ROWS = 8
NA_WIN_COLS = 16
NA_Q_ROWS = 4
NA_K_ROWS = NA_Q_ROWS + NA_WIN_ROWS
SGU_CHUNK = 128
MLA_Q_RANK = 256
MLA_KV_RANK = 128
MLA_NOPE = 64
MLA_ROPE = 32
MLA_V = 64
MLA_SLOT = 128
MLA_SCALE = (MLA_NOPE + MLA_ROPE) ** -0.5
LOG2_E = float(np.log2(np.e))
ROPE_BASE = 10000.0
N_EXPERTS = 16
N_EXPERT_GROUPS = 4
EXPERTS_PER_GROUP = 4
D_EXPERT = 256
N_MOD = 6
EPS = 1e-6
NEG_INF = -1e30

OFF_B = MIX_PART
OFF_C = OFF_B + 3 * MIX_PART
OFF_D = OFF_C + 2 * MIX_PART
OFF_KR = OFF_D + MLA_Q_RANK + MLA_KV_RANK
W_IN_COLS = OFF_KR + 2 * MLA_SLOT

TOKEN_TILE = 512
ATTN_TILE = 256
MLA_KV_TILE = 512
MOE_TILE = 512
VMEM_LIMIT = 56 * 1024 * 1024


def _cparams(*sem):
    return pltpu.CompilerParams(dimension_semantics=sem, vmem_limit_bytes=VMEM_LIMIT)


def _dot(a, b):
    return jnp.dot(a, b, preferred_element_type=F32)


def _dot_nt(a, b, precision=None):
    return lax.dot_general(a, b, (((1,), (1,)), ((), ())), precision=precision,
                           preferred_element_type=F32)


def _rms(x, g):
    return x * lax.rsqrt(jnp.mean(x * x, axis=-1, keepdims=True) + EPS) * g


def _silu(x):
    return x * jax.nn.sigmoid(x)


def _mod_body(c_ref, w_ref, b_ref, o_ref):
    o_ref[0] = jnp.dot(_silu(c_ref[...]), w_ref[0], precision=HIGHEST,
                       preferred_element_type=F32) + b_ref[0]


def _modulation(c_all, ada_w, ada_b):
    L, D, ND = ada_w.shape
    tn = 1536
    return pl.pallas_call(
        _mod_body,
        grid=(L, ND // tn),
        in_specs=[pl.BlockSpec((8, D), lambda l, j: (0, 0)),
                  pl.BlockSpec((1, D, tn), lambda l, j: (l, 0, j)),
                  pl.BlockSpec((1, 1, tn), lambda l, j: (l, 0, j))],
        out_specs=pl.BlockSpec((1, 8, tn), lambda l, j: (l, 0, j)),
        out_shape=jax.ShapeDtypeStruct((L, 8, ND), F32),
        compiler_params=_cparams("arbitrary", "arbitrary"),
        name="modulation",
    )(c_all, ada_w, ada_b.reshape(L, 1, ND))


def _inproj_body(has_res, *refs):
    if has_res:
        (x_ref, y_ref, mp_ref, m_ref, g1_ref, w_ref, wq_ref, gq_ref, wkv_ref, wvt_ref, gkv_ref, cos_ref, sin_ref,
         xo_ref, pa_ref, qb_ref, kb_ref, vb_ref, pc_ref, qd_ref, kd_ref, vd_ref) = refs
        x = x_ref[...] + mp_ref[0, 5:6, :] * y_ref[...].astype(F32)
        xo_ref[...] = x
    else:
        (x_ref, m_ref, g1_ref, w_ref, wq_ref, gq_ref, wkv_ref, wvt_ref, gkv_ref, cos_ref, sin_ref,
         pa_ref, qb_ref, kb_ref, vb_ref, pc_ref, qd_ref, kd_ref, vd_ref) = refs
        x = x_ref[...]
    m = m_ref[0]
    hb = (_rms(x, g1_ref[...]) * (1.0 + m[1:2]) + m[0:1]).astype(BF16)

    def proj(a, b):
        return _dot(hb, w_ref[:, a:b])

    pa_ref[...] = proj(0, OFF_B)
    qb_ref[...] = (proj(OFF_B, OFF_B + MIX_PART) * (GROUP_DIM ** -0.5)).astype(BF16)
    kb_ref[...] = proj(OFF_B + MIX_PART, OFF_B + 2 * MIX_PART).astype(BF16)
    vb_ref[...] = proj(OFF_B + 2 * MIX_PART, OFF_C).astype(BF16)
    pc_ref[...] = proj(OFF_C, OFF_D)

    cosp = cos_ref[...]
    sinp = sin_ref[...]
    qn = _rms(proj(OFF_D, OFF_D + MLA_Q_RANK), gq_ref[...]).astype(BF16)
    qq = _dot(qn, wq_ref[...])
    half = N_GROUPS * MLA_SLOT
    for h in range(N_GROUPS):
        a = h * MLA_SLOT
        q = qq[:, a:a + MLA_SLOT] * cosp + qq[:, half + a:half + a + MLA_SLOT] * sinp
        qd_ref[:, a:a + MLA_SLOT] = (q * (MLA_SCALE * LOG2_E)).astype(BF16)
    kvn = _rms(proj(OFF_D + MLA_Q_RANK, OFF_KR), gkv_ref[...]).astype(BF16)
    kk = _dot(kvn, wkv_ref[...])
    kr = proj(OFF_KR, OFF_KR + MLA_SLOT) * cosp + proj(OFF_KR + MLA_SLOT, W_IN_COLS) * sinp
    for h in range(N_GROUPS):
        a = h * MLA_SLOT
        kd_ref[:, a:a + MLA_SLOT] = (kk[:, a:a + MLA_SLOT] + kr).astype(BF16)
    ones_row = (lax.broadcasted_iota(jnp.int32, (half, 1), 0) % MLA_SLOT == MLA_V).astype(F32)
    vd_ref[...] = (_dot_nt(wvt_ref[...], kvn) + ones_row).astype(BF16)


def _in_projection(l, n_lat_tiles_per_batch, n_batch, x, y, mod_r, g1, w, wq, gq, wkv, wvt, gkv, cosp, sinp):
    n, d = x.shape
    tm = TOKEN_TILE
    has_res = y is not None

    def row(i):
        return (i, 0)

    def modrow(layer):
        return lambda i: (layer * 8 + jnp.minimum(i // n_lat_tiles_per_batch, n_batch), 0, 0)

    def const2(i):
        return (0, 0)

    tok = lambda c: pl.BlockSpec((tm, c), row)
    mod_spec = lambda layer: pl.BlockSpec((1, N_MOD, d), modrow(layer))
    in_specs = [tok(d)]
    args = [x]
    if has_res:
        in_specs += [tok(d), mod_spec(l - 1)]
        args += [y, mod_r]
    in_specs += [mod_spec(l), pl.BlockSpec((1, d), const2), pl.BlockSpec(w.shape, const2),
                 pl.BlockSpec(wq.shape, const2), pl.BlockSpec(gq.shape, const2),
                 pl.BlockSpec(wkv.shape, const2), pl.BlockSpec(wvt.shape, const2), pl.BlockSpec(gkv.shape, const2),
                 tok(MLA_SLOT), tok(MLA_SLOT)]
    args += [mod_r, g1, w, wq, gq, wkv, wvt, gkv, cosp, sinp]
    outs = [(MIX_PART, F32), (MIX_PART, BF16), (MIX_PART, BF16), (MIX_PART, BF16), (2 * MIX_PART, F32),
            (N_GROUPS * MLA_SLOT, BF16), (N_GROUPS * MLA_SLOT, BF16)]
    if has_res:
        outs = [(d, F32)] + outs
    slots = N_GROUPS * MLA_SLOT
    res = pl.pallas_call(
        functools.partial(_inproj_body, has_res),
        grid=(n // tm,),
        in_specs=in_specs,
        out_specs=[tok(c) for c, _ in outs] + [pl.BlockSpec((slots, tm), lambda i: (0, i))],
        out_shape=[jax.ShapeDtypeStruct((n, c), t) for c, t in outs] + [jax.ShapeDtypeStruct((slots, n), BF16)],
        compiler_params=_cparams("arbitrary"),
        name="in_projection",
    )(*args)
    if not has_res:
        res = [x] + list(res)
    return res


def _pool_body(n_lat, seq, ctx_len, prev_ref, cur_ref, next_ref, w_ref, s_ref, o_ref):
    tb = cur_ref.shape[0]
    ext = jnp.concatenate([prev_ref[...], cur_ref[...], next_ref[...]], axis=0)
    n = tb + 16
    g = pl.program_id(0) * tb - 8 + lax.broadcasted_iota(jnp.int32, (n, 1), 0)
    is_lat = g < n_lat
    length = jnp.where(is_lat, seq, ctx_len)
    p = jnp.where(is_lat, g & (seq - 1), (g - n_lat) & (ctx_len - 1))

    def shifted(a, j):
        r = pltpu.roll(a, (-j) % n, axis=0)
        ok = (p + j >= 0) & (p + j < length)
        return jnp.where(ok, r, 0.0)

    before1 = shifted(ext, -1)
    before2 = before1 + shifted(before1, -1)
    before4 = before2 + shifted(before2, -2)
    before8 = before4 + shifted(before4, -4)
    after2 = ext + shifted(ext, 1)
    after4 = after2 + shifted(after2, 2)
    after8 = after4 + shifted(after4, 4)
    lane_grp = lax.broadcasted_iota(jnp.int32, (1, MIX_PART), 1) // GROUP_DIM
    sl = slice(8, 8 + tb)
    tot = jnp.where(lane_grp == 0, (before1 + ext)[sl],
                    jnp.where(lane_grp == 1, (before2 + after2)[sl],
                              jnp.where(lane_grp == 2, (before4 + after4)[sl], (before8 + after8)[sl])))
    half = jnp.where(lane_grp == 0, POOL_WINDOWS[0] // 2,
                     jnp.where(lane_grp == 1, POOL_WINDOWS[1] // 2,
                               jnp.where(lane_grp == 2, POOL_WINDOWS[2] // 2, POOL_WINDOWS[3] // 2)))
    pc = p[sl]
    cnt = jnp.minimum(pc + half, length[sl]) - jnp.maximum(pc - half, 0)
    dlt = tot / cnt.astype(F32) - ext[sl]
    o_ref[...] = (_dot(dlt.astype(BF16), w_ref[...]) * s_ref[...]).astype(BF16)


def _pool_mixer(n_lat, seq, ctx_len, pa, w_bd, s):
    n = pa.shape[0]
    tb = 1024
    assert seq & (seq - 1) == 0 and ctx_len & (ctx_len - 1) == 0
    assert n_lat % tb == 0 and n % tb == 0
    nb8 = n // 8
    return pl.pallas_call(
        functools.partial(_pool_body, n_lat, seq, ctx_len),
        grid=(n // tb,),
        in_specs=[pl.BlockSpec((8, MIX_PART), lambda i: (jnp.maximum(i * (tb // 8) - 1, 0), 0)),
                  pl.BlockSpec((tb, MIX_PART), lambda i: (i, 0)),
                  pl.BlockSpec((8, MIX_PART), lambda i: (jnp.minimum((i + 1) * (tb // 8), nb8 - 1), 0)),
                  pl.BlockSpec(w_bd.shape, lambda i: (0, 0)),
                  pl.BlockSpec(s.shape, lambda i: (0, 0))],
        out_specs=pl.BlockSpec((tb, MIX_PART), lambda i: (i, 0)),
        out_shape=jax.ShapeDtypeStruct((n, MIX_PART), BF16),
        compiler_params=_cparams("arbitrary"),
        name="pool_mixer",
    )(pa, pa, pa, w_bd, s)


def _softmax_pv(parts):
    m = parts[0][0].max(axis=-1, keepdims=True)
    for s, _ in parts[1:]:
        m = jnp.maximum(m, s.max(axis=-1, keepdims=True))
    den = 0.0
    out = 0.0
    for s, v in parts:
        p = jnp.exp(s - m)
        den = den + p.sum(axis=-1, keepdims=True)
        out = out + _dot(p.astype(BF16), v)
    return out / den


def _na_body(rows, n_qt, q_ref, kl_ref, kc_ref, vl_ref, vc_ref, bias_ref, o_ref):
    qt = pl.program_id(1)
    q = q_ref[...]
    kc = kc_ref[...]
    vc = vc_ref[...]
    lane_grp = lax.broadcasted_iota(jnp.int32, (1, MIX_PART), 1) // GROUP_DIM
    zero = jnp.zeros_like(q)

    @pl.when(qt < n_qt)
    def _():
        k0 = jnp.clip(NA_Q_ROWS * qt - NA_WIN_ROWS // 2, 0, rows - NA_K_ROWS)
        off = pl.multiple_of(k0 * GRID_W, NA_Q_ROWS * GRID_W)
        kw = kl_ref[pl.ds(off, NA_K_ROWS * GRID_W), :]
        vw = vl_ref[pl.ds(off, NA_K_ROWS * GRID_W), :]
        o = jnp.zeros(q.shape, F32)
        for h in range(N_GROUPS):
            qh = jnp.where(lane_grp == h, q, zero)
            oh = _softmax_pv([(_dot_nt(qh, kw) + bias_ref[0, h], vw), (_dot_nt(qh, kc), vc)])
            o = jnp.where(lane_grp == h, oh, o)
        o_ref[...] = o.astype(BF16)

    @pl.when(qt == n_qt)
    def _():
        o = jnp.zeros(q.shape, F32)
        for h in range(N_GROUPS):
            qh = jnp.where(lane_grp == h, q, zero)
            oh = _softmax_pv([(_dot_nt(qh, kc), vc)])
            o = jnp.where(lane_grp == h, oh, o)
        o_ref[...] = o.astype(BF16)


def _attn_specs(n_batch, seq, ctx_len, width_q, width_k, width_v, width_o):
    n_qt = seq // ATTN_TILE
    lat_blocks = n_batch * n_qt
    assert ctx_len == ATTN_TILE

    def qmap(b, t):
        return (jnp.where(t < n_qt, b * n_qt + t, lat_blocks + b), 0)

    q_spec = pl.BlockSpec((ATTN_TILE, width_q), qmap)
    kl_spec = pl.BlockSpec((seq, width_k), lambda b, t: (b, 0))
    kc_spec = pl.BlockSpec((ctx_len, width_k), lambda b, t: (lat_blocks + b, 0))
    vl_spec = pl.BlockSpec((seq, width_v), lambda b, t: (b, 0))
    vc_spec = pl.BlockSpec((ctx_len, width_v), lambda b, t: (lat_blocks + b, 0))
    o_spec = pl.BlockSpec((ATTN_TILE, width_o), qmap)
    return n_qt, [q_spec, kl_spec, kc_spec, vl_spec, vc_spec], o_spec


def _na_mixer(l, n_batch, seq, ctx_len, qb, kb, vb, bias_tab):
    n = qb.shape[0]
    rows = seq // GRID_W
    n_qt, in_specs, o_spec = _attn_specs(n_batch, seq, ctx_len, MIX_PART, MIX_PART, MIX_PART, MIX_PART)
    assert ATTN_TILE == NA_Q_ROWS * GRID_W and rows >= NA_K_ROWS + NA_Q_ROWS

    def bias_map(b, t):
        kind = jnp.where(t == 0, 0, jnp.where(t >= n_qt - 1, 2, 1))
        return (l * 3 + kind, 0, 0, 0)

    in_specs.append(pl.BlockSpec((1,) + bias_tab.shape[1:], bias_map))
    return pl.pallas_call(
        functools.partial(_na_body, rows, n_qt),
        grid=(n_batch, n_qt + 1),
        in_specs=in_specs,
        out_specs=o_spec,
        out_shape=jax.ShapeDtypeStruct((n, MIX_PART), BF16),
        compiler_params=_cparams("arbitrary", "arbitrary"),
        name="neighborhood_attention",
    )(qb, kb, kb, vb, vb, bias_tab)


def _na_bias_tables(na_rpb, rows):
    L, H = na_rpb.shape[:2]
    kinds = ((0, 0), (NA_Q_ROWS, 0), (rows - NA_Q_ROWS, rows - NA_K_ROWS))
    qc = np.arange(GRID_W)
    kc = np.arange(GRID_W)
    wsc = np.clip(qc - NA_WIN_COLS // 2, 0, GRID_W - NA_WIN_COLS)
    col_ok = (kc[None, :] >= wsc[:, None]) & (kc[None, :] < wsc[:, None] + NA_WIN_COLS)
    dc = np.clip(kc[None, :] - qc[:, None] + NA_WIN_COLS - 1, 0, 2 * NA_WIN_COLS - 2)
    slabs = jnp.where(col_ok, na_rpb[:, :, :, dc], NEG_INF).transpose(0, 1, 3, 2, 4)

    def masked(n):
        return jnp.full((L, H, GRID_W, n, GRID_W), NEG_INF, F32)

    tabs = []
    for r0, k0 in kinds:
        per_row = []
        for qr in range(NA_Q_ROWS):
            r = r0 + qr
            first = int(np.clip(r - NA_WIN_ROWS // 2, 0, rows - NA_WIN_ROWS)) - k0
            d0 = k0 + first - r + NA_WIN_ROWS - 1
            assert 0 <= first <= NA_K_ROWS - NA_WIN_ROWS and 0 <= d0 <= NA_WIN_ROWS - 1
            per_row.append(jnp.concatenate(
                [masked(first), slabs[:, :, :, d0:d0 + NA_WIN_ROWS, :], masked(NA_K_ROWS - NA_WIN_ROWS - first)],
                axis=3))
        tabs.append(jnp.stack(per_row, axis=2))
    return jnp.stack(tabs, axis=1).reshape(L * 3, H, NA_Q_ROWS * GRID_W, NA_K_ROWS * GRID_W)


def _gelu_tanh(x):
    return 0.5 * x * (1.0 + jnp.tanh(np.sqrt(2.0 / np.pi).astype(np.float32) * (x + 0.044715 * (x * x * x))))


def _sgu_body(pc_ref, gn_ref, ones_ref, w_ref, b_ref, o_ref):
    tm = pc_ref.shape[0]
    uv = _gelu_tanh(pc_ref[...])
    u = uv[:, :MIX_PART]
    v = uv[:, MIX_PART:]
    ms = jnp.dot(v * v, ones_ref[...], precision=HIGHEST, preferred_element_type=F32)
    vg = (v * lax.rsqrt(ms + EPS) * gn_ref[...]).astype(BF16)
    lane_grp = lax.broadcasted_iota(jnp.int32, (1, MIX_PART), 1) // GROUP_DIM
    w = w_ref[...]
    for c in range(tm // SGU_CHUNK):
        rs = slice(c * SGU_CHUNK, (c + 1) * SGU_CHUNK)
        r = _dot(w, vg[rs])
        mixed = r[:SGU_CHUNK]
        for g in range(1, N_GROUPS):
            mixed = jnp.where(lane_grp == g, r[g * SGU_CHUNK:(g + 1) * SGU_CHUNK], mixed)
        o_ref[rs, :] = (u[rs] * (mixed + b_ref[...])).astype(BF16)


def _sgu_mixer(pc, gn, ones_bd, w_all, b_exp):
    n = pc.shape[0]
    tm = TOKEN_TILE
    const = lambda i: (0, 0)
    return pl.pallas_call(
        _sgu_body,
        grid=(n // tm,),
        in_specs=[pl.BlockSpec((tm, 2 * MIX_PART), lambda i: (i, 0)),
                  pl.BlockSpec(gn.shape, const), pl.BlockSpec(ones_bd.shape, const),
                  pl.BlockSpec(w_all.shape, const), pl.BlockSpec(b_exp.shape, const)],
        out_specs=pl.BlockSpec((tm, MIX_PART), lambda i: (i, 0)),
        out_shape=jax.ShapeDtypeStruct((n, MIX_PART), BF16),
        compiler_params=_cparams("arbitrary"),
        name="spatial_gating",
    )(pc, gn, ones_bd, w_all, b_exp)


def _mla_body(n_qt, n_kv, q_ref, kl_ref, kc_ref, vl_ref, vc_ref, o_ref, sa_ref, sb_ref):
    qt = pl.program_id(1)
    n_lat = jnp.where(qt < n_qt, n_kv, 0)
    heads = [slice(h * MLA_SLOT, (h + 1) * MLA_SLOT) for h in range(N_GROUPS)]

    def scores(t, h, dst):
        off = pl.multiple_of(t * MLA_KV_TILE, MLA_KV_TILE)
        dst[h] = _dot_nt(kl_ref[pl.ds(off, MLA_KV_TILE), heads[h]], q_ref[:, heads[h]])

    def consume(t, h, src, m, acc):
        off = pl.multiple_of(t * MLA_KV_TILE, MLA_KV_TILE)
        s = src[h]
        mn = jnp.maximum(m, s.max(axis=0, keepdims=True))
        p = jnp.exp2(s - mn).astype(BF16)
        return mn, jnp.exp2(m - mn) * acc + _dot(vl_ref[heads[h], pl.ds(off, MLA_KV_TILE)], p)

    state = []
    for h, hs in enumerate(heads):
        scores(0, h, sa_ref)
        s = _dot_nt(kc_ref[:, hs], q_ref[:, hs])
        m = s.max(axis=0, keepdims=True)
        state.append((m, _dot(vc_ref[hs, :], jnp.exp2(s - m).astype(BF16))))

    def step(i, carry):
        t = 2 * i
        mid = []
        for h, (m, acc) in enumerate(carry):
            scores(t + 1, h, sb_ref)
            mid.append(consume(t, h, sa_ref, m, acc))
        out = []
        for h, (m, acc) in enumerate(mid):
            scores(jnp.minimum(t + 2, n_kv - 1), h, sa_ref)
            out.append(consume(t + 1, h, sb_ref, m, acc))
        return tuple(out)

    state = lax.fori_loop(0, n_lat // 2, step, tuple(state))
    out_t = jnp.concatenate([acc[:MLA_V] / acc[MLA_V:MLA_V + 1] for _, acc in state], axis=0)
    o_ref[...] = out_t.T.astype(BF16)


def _mla_mixer(n_batch, seq, ctx_len, qd, kd, vdt):
    n = qd.shape[0]
    slots = N_GROUPS * MLA_SLOT
    n_qt, in_specs, o_spec = _attn_specs(n_batch, seq, ctx_len, slots, slots, slots, N_GROUPS * MLA_V)
    lat_blocks = n_batch * n_qt
    in_specs[3] = pl.BlockSpec((slots, seq), lambda b, t: (0, b))
    in_specs[4] = pl.BlockSpec((slots, ctx_len), lambda b, t: (0, lat_blocks + b))
    assert seq % MLA_KV_TILE == 0
    return pl.pallas_call(
        functools.partial(_mla_body, n_qt, seq // MLA_KV_TILE),
        grid=(n_batch, n_qt + 1),
        in_specs=in_specs,
        out_specs=o_spec,
        out_shape=jax.ShapeDtypeStruct((n, N_GROUPS * MLA_V), BF16),
        scratch_shapes=[pltpu.VMEM((N_GROUPS, MLA_KV_TILE, ATTN_TILE), F32)] * 2,
        compiler_params=_cparams("arbitrary", "arbitrary"),
        name="latent_attention",
    )(qd, kd, kd, vdt, vdt)


def _outproj_body(ya_ref, yb_ref, yc_ref, yd_ref, x_ref, m_ref, g2_ref, w_ref, rw_ref, rb_ref,
                  xo_ref, h_ref, r_ref):
    tm = x_ref.shape[0]
    m = m_ref[0]
    y = _dot(ya_ref[...], w_ref[0:MIX_PART, :])
    for k, ref in enumerate((yb_ref, yc_ref, yd_ref), start=1):
        y = y + _dot(ref[...], w_ref[k * MIX_PART:(k + 1) * MIX_PART, :])
    x = x_ref[...] + m[2:3] * y
    xo_ref[...] = x
    h = _rms(x, g2_ref[...]) * (1.0 + m[4:5]) + m[3:4]
    h_ref[...] = h.astype(BF16)

    scores = jax.nn.sigmoid(_dot_nt(rw_ref[...], h, precision=HIGHEST))
    biased = scores + rb_ref[...]
    E = EXPERTS_PER_GROUP
    bk = [biased[8 * k:8 * k + 8] for k in range(E)]
    sk = [scores[8 * k:8 * k + 8] for k in range(E)]
    gs = None
    for a in range(E):
        for b in range(a + 1, E):
            pair = bk[a] + bk[b]
            gs = pair if gs is None else jnp.maximum(gs, pair)
    best = gs[0:1]
    idx = jnp.zeros((1, tm), jnp.int32)
    for g in range(1, N_EXPERT_GROUPS):
        better = gs[g:g + 1] > best
        idx = jnp.where(better, g, idx)
        best = jnp.where(better, gs[g:g + 1], best)
    in_grp = lax.broadcasted_iota(jnp.int32, (8, tm), 0) == idx
    wk = []
    for k in range(E):
        rank = jnp.zeros((8, tm), jnp.int32)
        for j in range(E):
            if j != k:
                ahead = (bk[j] > bk[k]) | ((bk[j] == bk[k]) & (j < k))
                rank = rank + ahead.astype(jnp.int32)
        wk.append(jnp.where((rank < 2) & in_grp, sk[k], 0.0).sum(axis=0, keepdims=True))
    den = wk[0] + wk[1] + wk[2] + wk[3]
    ri = lax.broadcasted_iota(jnp.int32, (8, tm), 0)
    out = jnp.where(ri == E, idx.astype(F32), 0.0)
    for k in range(E):
        out = jnp.where(ri == k, wk[k] / den, out)
    r_ref[...] = out


def _out_projection(l, n_lat_tiles_per_batch, n_batch, ys, x, mod_r, g2, w, rw, rb):
    n, d = x.shape
    tm = TOKEN_TILE
    row = lambda i: (i, 0)
    const = lambda i: (0, 0)
    modrow = lambda i: (l * 8 + jnp.minimum(i // n_lat_tiles_per_batch, n_batch), 0, 0)
    in_specs = [pl.BlockSpec((tm, MIX_PART), row)] * 4 + [
        pl.BlockSpec((tm, d), row), pl.BlockSpec((1, N_MOD, d), modrow), pl.BlockSpec((1, d), const),
        pl.BlockSpec(w.shape, const), pl.BlockSpec(rw.shape, const), pl.BlockSpec(rb.shape, const)]
    return pl.pallas_call(
        _outproj_body,
        grid=(n // tm,),
        in_specs=in_specs,
        out_specs=[pl.BlockSpec((tm, d), row), pl.BlockSpec((tm, d), row), pl.BlockSpec((8, tm), lambda i: (0, i))],
        out_shape=[jax.ShapeDtypeStruct((n, d), F32), jax.ShapeDtypeStruct((n, d), BF16),
                   jax.ShapeDtypeStruct((8, n), F32)],
        compiler_params=_cparams("arbitrary"),
        name="out_projection_routing",
    )(*ys, x, mod_r, g2, w, rw, rb)


def _moe_body(tg_ref, tv_ref, h_ref, g_ref, wg_ref, wu_ref, wd_ref, o_ref, wgb, wub, wdb):
    i = pl.program_id(0)
    grp = tg_ref[i]
    prev = tg_ref[jnp.maximum(i - 1, 0)]

    @pl.when((i == 0) | (grp != prev))
    def _():
        wgb[...] = wg_ref[0].astype(BF16)
        wub[...] = wu_ref[0].astype(BF16)
        wdb[...] = wd_ref[0].astype(BF16)

    @pl.when(tv_ref[i] > 0)
    def _():
        h = h_ref[...]
        acc = jnp.zeros(o_ref.shape, F32)
        for k in range(EXPERTS_PER_GROUP):
            hid = _silu(_dot(h, wgb[k])) * _dot(h, wub[k]) * g_ref[:, k:k + 1]
            acc = acc + _dot(hid.astype(BF16), wdb[k])
        o_ref[...] = acc.astype(o_ref.dtype)

    @pl.when(tv_ref[i] == 0)
    def _():
        o_ref[...] = jnp.zeros(o_ref.shape, o_ref.dtype)


def _grouped_experts(l, tile_grp, tile_ok, hs, gs, w_gate, w_up, w_down):
    npad, d = hs.shape
    tm = MOE_TILE
    E = EXPERTS_PER_GROUP
    wmap = lambda i, tg, tv: (l, tg[i], 0, 0)
    grid_spec = pltpu.PrefetchScalarGridSpec(
        num_scalar_prefetch=2,
        grid=(npad // tm,),
        in_specs=[pl.BlockSpec((tm, d), lambda i, tg, tv: (i, 0)),
                  pl.BlockSpec((tm, E), lambda i, tg, tv: (i, 0)),
                  pl.BlockSpec((1, E, d, D_EXPERT), wmap),
                  pl.BlockSpec((1, E, d, D_EXPERT), wmap),
                  pl.BlockSpec((1, E, D_EXPERT, d), wmap)],
        out_specs=pl.BlockSpec((tm, d), lambda i, tg, tv: (i, 0)),
        scratch_shapes=[pltpu.VMEM((E, d, D_EXPERT), BF16), pltpu.VMEM((E, d, D_EXPERT), BF16),
                        pltpu.VMEM((E, D_EXPERT, d), BF16)])
    return pl.pallas_call(
        _moe_body,
        grid_spec=grid_spec,
        out_shape=jax.ShapeDtypeStruct((npad, d), BF16),
        compiler_params=_cparams("arbitrary"),
        name="grouped_experts",
    )(tile_grp, tile_ok, hs, gs, w_gate, w_up, w_down)


def _dispatch_plan(route, n_pad):
    n = route.shape[1]
    tm = MOE_TILE
    grp = route[EXPERTS_PER_GROUP].astype(jnp.int32)
    onehot = (grp[:, None] == jnp.arange(N_EXPERT_GROUPS)[None, :]).astype(jnp.int32)
    csum = jnp.cumsum(onehot, axis=0)
    counts = csum[-1]
    rank = jnp.take_along_axis(csum, grp[:, None], axis=1)[:, 0] - 1
    padded = ((counts + tm - 1) // tm) * tm
    ends = jnp.cumsum(padded)
    pos = (ends - padded)[grp] + rank
    perm = jnp.zeros((n_pad,), jnp.int32).at[pos].set(jnp.arange(n, dtype=jnp.int32))
    tile_start = jnp.arange(n_pad // tm, dtype=jnp.int32) * tm
    tile_grp = jnp.minimum(jnp.sum(tile_start[:, None] >= ends[None, :], axis=1), N_EXPERT_GROUPS - 1)
    tile_ok = (tile_start < ends[-1]).astype(jnp.int32)
    gates = jnp.take(route[:EXPERTS_PER_GROUP].T, perm, axis=0)
    return pos, perm, tile_grp.astype(jnp.int32), tile_ok, gates


def _final_body(x_ref, y_ref, m_ref, g_ref, o_ref):
    x = x_ref[...] + m_ref[0, 5:6, :] * y_ref[...].astype(F32)
    o_ref[...] = _rms(x, g_ref[...])


def _final_norm(l, n_lat, n_lat_tiles_per_batch, x, y, mod_r, g):
    d = x.shape[1]
    tm = TOKEN_TILE
    row = lambda i: (i, 0)
    return pl.pallas_call(
        _final_body,
        grid=(n_lat // tm,),
        in_specs=[pl.BlockSpec((tm, d), row), pl.BlockSpec((tm, d), row),
                  pl.BlockSpec((1, N_MOD, d), lambda i: (l * 8 + i // n_lat_tiles_per_batch, 0, 0)),
                  pl.BlockSpec((1, d), lambda i: (0, 0))],
        out_specs=pl.BlockSpec((tm, d), row),
        out_shape=jax.ShapeDtypeStruct((n_lat, d), F32),
        compiler_params=_cparams("arbitrary"),
        name="final_norm",
    )(x, y, mod_r, g)


def _rope_tables(seq, n_batch, n_ctx_tokens):
    t = np.arange(seq)
    nf = MLA_ROPE // 4
    inv = jnp.asarray(ROPE_BASE, F32) ** (-jnp.arange(nf, dtype=F32) / nf)
    rang = jnp.asarray(t // GRID_W, F32)[:, None] * inv
    cang = jnp.asarray(t % GRID_W, F32)[:, None] * inv
    cr, sr, cc, sc = jnp.cos(rang), jnp.sin(rang), jnp.cos(cang), jnp.sin(cang)
    one = jnp.ones((seq, MLA_NOPE), F32)
    zero_tail = jnp.zeros((seq, MLA_SLOT - MLA_NOPE - MLA_ROPE), F32)
    cosp = jnp.concatenate([one, cr, cr, cc, cc, zero_tail], axis=1)
    sinp = jnp.concatenate([0 * one, -sr, sr, -sc, sc, zero_tail], axis=1)
    ctx_cos = jnp.concatenate([jnp.ones((n_ctx_tokens, MLA_NOPE + MLA_ROPE), F32),
                               jnp.zeros((n_ctx_tokens, MLA_SLOT - MLA_NOPE - MLA_ROPE), F32)], axis=1)
    cosp = jnp.concatenate([jnp.tile(cosp, (n_batch, 1)), ctx_cos], axis=0)
    sinp = jnp.concatenate([jnp.tile(sinp, (n_batch, 1)), jnp.zeros((n_ctx_tokens, MLA_SLOT), F32)], axis=0)
    return cosp, sinp


_ROPE_SWAP = np.concatenate([np.arange(8, 16), np.arange(0, 8), np.arange(24, 32), np.arange(16, 24)])


def _pad_last(a, before, after):
    return jnp.pad(a, [(0, 0)] * (a.ndim - 1) + [(before, after)])


def _layouts(w_in, w_out, pool_w, sgu_w, sgu_b, mla_w_uq, mla_w_ukv, router_w, router_b):
    L = w_in.shape[0]
    tail = MLA_SLOT - MLA_NOPE - MLA_ROPE
    kr = w_in[:, :, OFF_KR:]
    w = jnp.concatenate([w_in[:, :, :OFF_KR], _pad_last(kr, MLA_NOPE, tail),
                         _pad_last(kr[:, :, _ROPE_SWAP], MLA_NOPE, tail)], axis=-1).astype(BF16)
    uq = mla_w_uq.reshape(L, MLA_Q_RANK, N_GROUPS, MLA_NOPE + MLA_ROPE)
    q1 = _pad_last(uq, 0, tail).reshape(L, MLA_Q_RANK, N_GROUPS * MLA_SLOT)
    q2 = _pad_last(uq[..., MLA_NOPE:][..., _ROPE_SWAP], MLA_NOPE, tail).reshape(L, MLA_Q_RANK, N_GROUPS * MLA_SLOT)
    wq = jnp.concatenate([q1, q2], axis=-1).astype(BF16)
    ukv = mla_w_ukv.reshape(L, MLA_KV_RANK, N_GROUPS, MLA_NOPE + MLA_V)
    kn = _pad_last(ukv[..., :MLA_NOPE], 0, MLA_SLOT - MLA_NOPE).reshape(L, MLA_KV_RANK, N_GROUPS * MLA_SLOT)
    vv = _pad_last(ukv[..., MLA_NOPE:], 0, MLA_SLOT - MLA_V).reshape(L, MLA_KV_RANK, N_GROUPS * MLA_SLOT)
    wkv = kn.astype(BF16)
    wvt = jnp.swapaxes(vv, 1, 2).astype(BF16)
    eye = jnp.eye(N_GROUPS, dtype=F32)
    pool_bd = jnp.einsum('lgcd,gh->lgchd', pool_w, eye).reshape(L, MIX_PART, MIX_PART).astype(BF16)
    ones_bd = jnp.asarray(np.kron(np.eye(N_GROUPS), np.full((GROUP_DIM, GROUP_DIM), 1.0 / GROUP_DIM)), F32)
    sgu_all = sgu_w.reshape(L, N_GROUPS * SGU_CHUNK, SGU_CHUNK).astype(BF16)
    sgu_bias = jnp.repeat(jnp.swapaxes(sgu_b, 1, 2), GROUP_DIM, axis=2)
    rw = router_w.T.reshape(N_EXPERT_GROUPS, EXPERTS_PER_GROUP, -1).transpose(1, 0, 2)
    rw = jnp.pad(rw, ((0, 0), (0, 8 - N_EXPERT_GROUPS), (0, 0))).reshape(8 * EXPERTS_PER_GROUP, -1)
    rb = router_b.reshape(N_EXPERT_GROUPS, EXPERTS_PER_GROUP).T
    rb = jnp.pad(rb, ((0, 0), (0, 8 - N_EXPERT_GROUPS))).reshape(8 * EXPERTS_PER_GROUP, 1)
    return w, wq, wkv, wvt, w_out.astype(BF16), pool_bd, ones_bd, sgu_all, sgu_bias, rw, rb


def kernel(x, c, ctx, c_ctx, ada_w, ada_b, norm1_g, norm2_g, w_in, w_out, pool_w, pool_s, na_rpb,
           sgu_norm_g, sgu_w, sgu_b, mla_q_norm_g, mla_w_uq, mla_kv_norm_g, mla_w_ukv,
           router_w, router_b, moe_w_gate, moe_w_up, moe_w_down, final_g):
    B, S, D = x.shape
    CTX = ctx.shape[1]
    L = ada_w.shape[0]
    n_lat, n_ctx = B * S, B * CTX
    n_all = n_lat + n_ctx
    assert B < 8 and S % TOKEN_TILE == 0 and n_ctx % TOKEN_TILE == 0
    tiles_per_batch = S // TOKEN_TILE
    n_pad = n_all + N_EXPERT_GROUPS * MOE_TILE

    c_all = jnp.concatenate([c, c_ctx[None, :], jnp.zeros((8 - B - 1, D), F32)], axis=0)
    mod_r = _modulation(c_all, ada_w, ada_b).reshape(L * 8, N_MOD, D)
    w, wq, wkv, wvt, wo, pool_bd, ones_bd, sgu_all, sgu_bias, rw, rb = _layouts(
        w_in, w_out, pool_w, sgu_w, sgu_b, mla_w_uq, mla_w_ukv, router_w, router_b)
    bias_tab = _na_bias_tables(na_rpb, S // GRID_W)
    cosp, sinp = _rope_tables(S, B, n_ctx)

    xs = jnp.concatenate([x.reshape(n_lat, D), ctx.reshape(n_ctx, D)], axis=0)
    y = None
    for l in range(L):
        xs, pa, qb, kb, vb, pc, qd, kd, vd = _in_projection(
            l, tiles_per_batch, B, xs, y, mod_r, norm1_g[l][None], w[l], wq[l], mla_q_norm_g[l][None],
            wkv[l], wvt[l], mla_kv_norm_g[l][None], cosp, sinp)
        ya = _pool_mixer(n_lat, S, CTX, pa, pool_bd[l], pool_s[l][None])
        yb = _na_mixer(l, B, S, CTX, qb, kb, vb, bias_tab)
        yc = _sgu_mixer(pc, sgu_norm_g[l][None], ones_bd, sgu_all[l], sgu_bias[l])
        yd = _mla_mixer(B, S, CTX, qd, kd, vd)
        xs, h, route = _out_projection(l, tiles_per_batch, B, (ya, yb, yc, yd), xs, mod_r, norm2_g[l][None],
                                       wo[l], rw, rb)
        pos, perm, tile_grp, tile_ok, gates = _dispatch_plan(route, n_pad)
        ysorted = _grouped_experts(l, tile_grp, tile_ok, jnp.take(h, perm, axis=0), gates,
                                   moe_w_gate, moe_w_up, moe_w_down)
        y = jnp.take(ysorted, pos, axis=0)
    out = _final_norm(L - 1, n_lat, tiles_per_batch, xs, y, mod_r, final_g[None])
    return out.reshape(B, S, D)
```

```python
import functools

import numpy as np
import jax
import jax.numpy as jnp
from jax import lax
from jax.experimental import pallas as pl
from jax.experimental.pallas import tpu as pltpu

F32 = jnp.float32
BF16 = jnp.bfloat16
HIGHEST = lax.Precision.HIGHEST

GRID_W = 64
POOL_WINDOWS = (2, 4, 8, 16)
GROUP_DIM = 64
N_GROUPS = 4
MIX_PART = N_GROUPS * GROUP_DIM
NA_WIN_ROWS = 8
NA_WIN_COLS = 16
NA_Q_ROWS = 4
NA_K_ROWS = NA_Q_ROWS + NA_WIN_ROWS
SGU_CHUNK = 128
MLA_Q_RANK = 256
MLA_KV_RANK = 128
MLA_NOPE = 64
MLA_ROPE = 32
MLA_V = 64
MLA_SLOT = 128
MLA_SCALE = (MLA_NOPE + MLA_ROPE) ** -0.5
LOG2_E = float(np.log2(np.e))
ROPE_BASE = 10000.0
N_EXPERTS = 16
N_EXPERT_GROUPS = 4
EXPERTS_PER_GROUP = 4
D_EXPERT = 256
N_MOD = 6
EPS = 1e-6
NEG_INF = -1e30

OFF_B = MIX_PART
OFF_C = OFF_B + 3 * MIX_PART
OFF_D = OFF_C + 2 * MIX_PART
OFF_KR = OFF_D + MLA_Q_RANK + MLA_KV_RANK
W_IN_COLS = OFF_KR + 2 * MLA_SLOT

LANES = 128
FEAT_ROWS = 8
SLAB_ROWS = 16

TOKEN_TILE = 512
ATTN_TILE = 256
MLA_KV_TILE = 512
MOE_TILE = 512
VMEM_LIMIT = 56 * 1024 * 1024


def _cparams(*sem):
    return pltpu.CompilerParams(dimension_semantics=sem, vmem_limit_bytes=VMEM_LIMIT)


def _dot(a, b):
    return jnp.dot(a, b, preferred_element_type=F32)


def _dot_nt(a, b, precision=None):
    return lax.dot_general(a, b, (((1,), (1,)), ((), ())), precision=precision,
                           preferred_element_type=F32)


def _rms(x, g):
    return x * lax.rsqrt(jnp.mean(x * x, axis=-1, keepdims=True) + EPS) * g


def _silu(x):
    return x * jax.nn.sigmoid(x)


def _mod_body(c_ref, w_ref, b_ref, o_ref):
    o_ref[0] = jnp.dot(_silu(c_ref[...]), w_ref[0], precision=HIGHEST,
                       preferred_element_type=F32) + b_ref[0]


def _modulation(c_all, ada_w, ada_b):
    L, D, ND = ada_w.shape
    tn = 1536
    return pl.pallas_call(
        _mod_body,
        grid=(L, ND // tn),
        in_specs=[pl.BlockSpec((8, D), lambda l, j: (0, 0)),
                  pl.BlockSpec((1, D, tn), lambda l, j: (l, 0, j)),
                  pl.BlockSpec((1, 1, tn), lambda l, j: (l, 0, j))],
        out_specs=pl.BlockSpec((1, 8, tn), lambda l, j: (l, 0, j)),
        out_shape=jax.ShapeDtypeStruct((L, 8, ND), F32),
        compiler_params=_cparams("arbitrary", "arbitrary"),
        name="modulation",
    )(c_all, ada_w, ada_b.reshape(L, 1, ND))


def _inproj_body(has_res, *refs):
    if has_res:
        (x_ref, y_ref, mp_ref, m_ref, g1_ref, w_ref, wq_ref, gq_ref, wkv_ref, wvt_ref, gkv_ref, cos_ref, sin_ref,
         xo_ref, pa_ref, qb_ref, kb_ref, vb_ref, pc_ref, qd_ref, kd_ref, vd_ref) = refs
        x = x_ref[...] + mp_ref[0, 5:6, :] * _slab_rows(y_ref)
        xo_ref[...] = x
    else:
        (x_ref, m_ref, g1_ref, w_ref, wq_ref, gq_ref, wkv_ref, wvt_ref, gkv_ref, cos_ref, sin_ref,
         pa_ref, qb_ref, kb_ref, vb_ref, pc_ref, qd_ref, kd_ref, vd_ref) = refs
        x = x_ref[...]
    m = m_ref[0]
    hb = (_rms(x, g1_ref[...]) * (1.0 + m[1:2]) + m[0:1]).astype(BF16)

    def proj(a, b):
        return _dot(hb, w_ref[:, a:b])

    pa_ref[...] = proj(0, OFF_B)
    qb_ref[...] = (proj(OFF_B, OFF_B + MIX_PART) * (GROUP_DIM ** -0.5)).astype(BF16)
    kb_ref[...] = proj(OFF_B + MIX_PART, OFF_B + 2 * MIX_PART).astype(BF16)
    vb_ref[...] = proj(OFF_B + 2 * MIX_PART, OFF_C).astype(BF16)
    pc_ref[...] = proj(OFF_C, OFF_D)

    cosp = cos_ref[...]
    sinp = sin_ref[...]
    qn = _rms(proj(OFF_D, OFF_D + MLA_Q_RANK), gq_ref[...]).astype(BF16)
    qq = _dot(qn, wq_ref[...])
    half = N_GROUPS * MLA_SLOT
    for h in range(N_GROUPS):
        a = h * MLA_SLOT
        q = qq[:, a:a + MLA_SLOT] * cosp + qq[:, half + a:half + a + MLA_SLOT] * sinp
        qd_ref[:, a:a + MLA_SLOT] = (q * (MLA_SCALE * LOG2_E)).astype(BF16)
    kvn = _rms(proj(OFF_D + MLA_Q_RANK, OFF_KR), gkv_ref[...]).astype(BF16)
    kk = _dot(kvn, wkv_ref[...])
    kr = proj(OFF_KR, OFF_KR + MLA_SLOT) * cosp + proj(OFF_KR + MLA_SLOT, W_IN_COLS) * sinp
    for h in range(N_GROUPS):
        a = h * MLA_SLOT
        kd_ref[:, a:a + MLA_SLOT] = (kk[:, a:a + MLA_SLOT] + kr).astype(BF16)
    ones_row = (lax.broadcasted_iota(jnp.int32, (half, 1), 0) % MLA_SLOT == MLA_V).astype(F32)
    vd_ref[...] = (_dot_nt(wvt_ref[...], kvn) + ones_row).astype(BF16)


def _in_projection(l, n_lat_tiles_per_batch, n_batch, x, y, mod_r, g1, w, wq, gq, wkv, wvt, gkv, cosp, sinp):
    n, d = x.shape
    tm = TOKEN_TILE
    has_res = y is not None

    def row(i):
        return (i, 0)

    def modrow(layer):
        return lambda i: (layer * 8 + jnp.minimum(i // n_lat_tiles_per_batch, n_batch), 0, 0)

    def const2(i):
        return (0, 0)

    tok = lambda c: pl.BlockSpec((tm, c), row)
    mod_spec = lambda layer: pl.BlockSpec((1, N_MOD, d), modrow(layer))
    in_specs = [tok(d)]
    args = [x]
    if has_res:
        in_specs += [pl.BlockSpec((tm, FEAT_ROWS, LANES), lambda i: (i, 0, 0)), mod_spec(l - 1)]
        args += [y, mod_r]
    in_specs += [mod_spec(l), pl.BlockSpec((1, d), const2), pl.BlockSpec(w.shape, const2),
                 pl.BlockSpec(wq.shape, const2), pl.BlockSpec(gq.shape, const2),
                 pl.BlockSpec(wkv.shape, const2), pl.BlockSpec(wvt.shape, const2), pl.BlockSpec(gkv.shape, const2),
                 tok(MLA_SLOT), tok(MLA_SLOT)]
    args += [mod_r, g1, w, wq, gq, wkv, wvt, gkv, cosp, sinp]
    outs = [(MIX_PART, F32), (MIX_PART, BF16), (MIX_PART, BF16), (MIX_PART, BF16), (2 * MIX_PART, F32),
            (N_GROUPS * MLA_SLOT, BF16), (N_GROUPS * MLA_SLOT, BF16)]
    if has_res:
        outs = [(d, F32)] + outs
    slots = N_GROUPS * MLA_SLOT
    res = pl.pallas_call(
        functools.partial(_inproj_body, has_res),
        grid=(n // tm,),
        in_specs=in_specs,
        out_specs=[tok(c) for c, _ in outs] + [pl.BlockSpec((slots, tm), lambda i: (0, i))],
        out_shape=[jax.ShapeDtypeStruct((n, c), t) for c, t in outs] + [jax.ShapeDtypeStruct((slots, n), BF16)],
        compiler_params=_cparams("arbitrary"),
        name="in_projection",
    )(*args)
    if not has_res:
        res = [x] + list(res)
    return res


def _pool_body(n_lat, seq, ctx_len, prev_ref, cur_ref, next_ref, w_ref, s_ref, o_ref):
    tb = cur_ref.shape[0]
    ext = jnp.concatenate([prev_ref[...], cur_ref[...], next_ref[...]], axis=0)
    n = tb + 16
    g = pl.program_id(0) * tb - 8 + lax.broadcasted_iota(jnp.int32, (n, 1), 0)
    is_lat = g < n_lat
    length = jnp.where(is_lat, seq, ctx_len)
    p = jnp.where(is_lat, g & (seq - 1), (g - n_lat) & (ctx_len - 1))

    def shifted(a, j):
        r = pltpu.roll(a, (-j) % n, axis=0)
        ok = (p + j >= 0) & (p + j < length)
        return jnp.where(ok, r, 0.0)

    before1 = shifted(ext, -1)
    before2 = before1 + shifted(before1, -1)
    before4 = before2 + shifted(before2, -2)
    before8 = before4 + shifted(before4, -4)
    after2 = ext + shifted(ext, 1)
    after4 = after2 + shifted(after2, 2)
    after8 = after4 + shifted(after4, 4)
    lane_grp = lax.broadcasted_iota(jnp.int32, (1, MIX_PART), 1) // GROUP_DIM
    sl = slice(8, 8 + tb)
    tot = jnp.where(lane_grp == 0, (before1 + ext)[sl],
                    jnp.where(lane_grp == 1, (before2 + after2)[sl],
                              jnp.where(lane_grp == 2, (before4 + after4)[sl], (before8 + after8)[sl])))
    half = jnp.where(lane_grp == 0, POOL_WINDOWS[0] // 2,
                     jnp.where(lane_grp == 1, POOL_WINDOWS[1] // 2,
                               jnp.where(lane_grp == 2, POOL_WINDOWS[2] // 2, POOL_WINDOWS[3] // 2)))
    pc = p[sl]
    cnt = jnp.minimum(pc + half, length[sl]) - jnp.maximum(pc - half, 0)
    dlt = tot / cnt.astype(F32) - ext[sl]
    o_ref[...] = (_dot(dlt.astype(BF16), w_ref[...]) * s_ref[...]).astype(BF16)


def _pool_mixer(n_lat, seq, ctx_len, pa, w_bd, s):
    n = pa.shape[0]
    tb = 1024
    assert seq & (seq - 1) == 0 and ctx_len & (ctx_len - 1) == 0
    assert n_lat % tb == 0 and n % tb == 0
    nb8 = n // 8
    return pl.pallas_call(
        functools.partial(_pool_body, n_lat, seq, ctx_len),
        grid=(n // tb,),
        in_specs=[pl.BlockSpec((8, MIX_PART), lambda i: (jnp.maximum(i * (tb // 8) - 1, 0), 0)),
                  pl.BlockSpec((tb, MIX_PART), lambda i: (i, 0)),
                  pl.BlockSpec((8, MIX_PART), lambda i: (jnp.minimum((i + 1) * (tb // 8), nb8 - 1), 0)),
                  pl.BlockSpec(w_bd.shape, lambda i: (0, 0)),
                  pl.BlockSpec(s.shape, lambda i: (0, 0))],
        out_specs=pl.BlockSpec((tb, MIX_PART), lambda i: (i, 0)),
        out_shape=jax.ShapeDtypeStruct((n, MIX_PART), BF16),
        compiler_params=_cparams("arbitrary"),
        name="pool_mixer",
    )(pa, pa, pa, w_bd, s)


def _softmax_pv(parts):
    m = parts[0][0].max(axis=-1, keepdims=True)
    for s, _ in parts[1:]:
        m = jnp.maximum(m, s.max(axis=-1, keepdims=True))
    den = 0.0
    out = 0.0
    for s, v in parts:
        p = jnp.exp(s - m)
        den = den + p.sum(axis=-1, keepdims=True)
        out = out + _dot(p.astype(BF16), v)
    return out / den


def _na_body(rows, n_qt, q_ref, kl_ref, kc_ref, vl_ref, vc_ref, bias_ref, o_ref):
    qt = pl.program_id(1)
    q = q_ref[...]
    kc = kc_ref[...]
    vc = vc_ref[...]
    lane_grp = lax.broadcasted_iota(jnp.int32, (1, MIX_PART), 1) // GROUP_DIM
    zero = jnp.zeros_like(q)

    @pl.when(qt < n_qt)
    def _():
        k0 = jnp.clip(NA_Q_ROWS * qt - NA_WIN_ROWS // 2, 0, rows - NA_K_ROWS)
        off = pl.multiple_of(k0 * GRID_W, NA_Q_ROWS * GRID_W)
        kw = kl_ref[pl.ds(off, NA_K_ROWS * GRID_W), :]
        vw = vl_ref[pl.ds(off, NA_K_ROWS * GRID_W), :]
        o = jnp.zeros(q.shape, F32)
        for h in range(N_GROUPS):
            qh = jnp.where(lane_grp == h, q, zero)
            oh = _softmax_pv([(_dot_nt(qh, kw) + bias_ref[0, h], vw), (_dot_nt(qh, kc), vc)])
            o = jnp.where(lane_grp == h, oh, o)
        o_ref[...] = o.astype(BF16)

    @pl.when(qt == n_qt)
    def _():
        o = jnp.zeros(q.shape, F32)
        for h in range(N_GROUPS):
            qh = jnp.where(lane_grp == h, q, zero)
            oh = _softmax_pv([(_dot_nt(qh, kc), vc)])
            o = jnp.where(lane_grp == h, oh, o)
        o_ref[...] = o.astype(BF16)


def _attn_specs(n_batch, seq, ctx_len, width_q, width_k, width_v, width_o):
    n_qt = seq // ATTN_TILE
    lat_blocks = n_batch * n_qt
    assert ctx_len == ATTN_TILE

    def qmap(b, t):
        return (jnp.where(t < n_qt, b * n_qt + t, lat_blocks + b), 0)

    q_spec = pl.BlockSpec((ATTN_TILE, width_q), qmap)
    kl_spec = pl.BlockSpec((seq, width_k), lambda b, t: (b, 0))
    kc_spec = pl.BlockSpec((ctx_len, width_k), lambda b, t: (lat_blocks + b, 0))
    vl_spec = pl.BlockSpec((seq, width_v), lambda b, t: (b, 0))
    vc_spec = pl.BlockSpec((ctx_len, width_v), lambda b, t: (lat_blocks + b, 0))
    o_spec = pl.BlockSpec((ATTN_TILE, width_o), qmap)
    return n_qt, [q_spec, kl_spec, kc_spec, vl_spec, vc_spec], o_spec


def _na_mixer(l, n_batch, seq, ctx_len, qb, kb, vb, bias_tab):
    n = qb.shape[0]
    rows = seq // GRID_W
    n_qt, in_specs, o_spec = _attn_specs(n_batch, seq, ctx_len, MIX_PART, MIX_PART, MIX_PART, MIX_PART)
    assert ATTN_TILE == NA_Q_ROWS * GRID_W and rows >= NA_K_ROWS + NA_Q_ROWS

    def bias_map(b, t):
        kind = jnp.where(t == 0, 0, jnp.where(t >= n_qt - 1, 2, 1))
        return (l * 3 + kind, 0, 0, 0)

    in_specs.append(pl.BlockSpec((1,) + bias_tab.shape[1:], bias_map))
    return pl.pallas_call(
        functools.partial(_na_body, rows, n_qt),
        grid=(n_batch, n_qt + 1),
        in_specs=in_specs,
        out_specs=o_spec,
        out_shape=jax.ShapeDtypeStruct((n, MIX_PART), BF16),
        compiler_params=_cparams("arbitrary", "arbitrary"),
        name="neighborhood_attention",
    )(qb, kb, kb, vb, vb, bias_tab)


def _na_bias_tables(na_rpb, rows):
    L, H = na_rpb.shape[:2]
    kinds = ((0, 0), (NA_Q_ROWS, 0), (rows - NA_Q_ROWS, rows - NA_K_ROWS))
    qc = np.arange(GRID_W)
    kc = np.arange(GRID_W)
    wsc = np.clip(qc - NA_WIN_COLS // 2, 0, GRID_W - NA_WIN_COLS)
    col_ok = (kc[None, :] >= wsc[:, None]) & (kc[None, :] < wsc[:, None] + NA_WIN_COLS)
    dc = np.clip(kc[None, :] - qc[:, None] + NA_WIN_COLS - 1, 0, 2 * NA_WIN_COLS - 2)
    slabs = jnp.where(col_ok, na_rpb[:, :, :, dc], NEG_INF).transpose(0, 1, 3, 2, 4)

    def masked(n):
        return jnp.full((L, H, GRID_W, n, GRID_W), NEG_INF, F32)

    tabs = []
    for r0, k0 in kinds:
        per_row = []
        for qr in range(NA_Q_ROWS):
            r = r0 + qr
            first = int(np.clip(r - NA_WIN_ROWS // 2, 0, rows - NA_WIN_ROWS)) - k0
            d0 = k0 + first - r + NA_WIN_ROWS - 1
            assert 0 <= first <= NA_K_ROWS - NA_WIN_ROWS and 0 <= d0 <= NA_WIN_ROWS - 1
            per_row.append(jnp.concatenate(
                [masked(first), slabs[:, :, :, d0:d0 + NA_WIN_ROWS, :], masked(NA_K_ROWS - NA_WIN_ROWS - first)],
                axis=3))
        tabs.append(jnp.stack(per_row, axis=2))
    return jnp.stack(tabs, axis=1).reshape(L * 3, H, NA_Q_ROWS * GRID_W, NA_K_ROWS * GRID_W)


def _gelu_tanh(x):
    return 0.5 * x * (1.0 + jnp.tanh(np.sqrt(2.0 / np.pi).astype(np.float32) * (x + 0.044715 * (x * x * x))))


def _sgu_body(pc_ref, gn_ref, ones_ref, w_ref, b_ref, o_ref):
    tm = pc_ref.shape[0]
    uv = _gelu_tanh(pc_ref[...])
    u = uv[:, :MIX_PART]
    v = uv[:, MIX_PART:]
    ms = jnp.dot(v * v, ones_ref[...], precision=HIGHEST, preferred_element_type=F32)
    vg = (v * lax.rsqrt(ms + EPS) * gn_ref[...]).astype(BF16)
    lane_grp = lax.broadcasted_iota(jnp.int32, (1, MIX_PART), 1) // GROUP_DIM
    w = w_ref[...]
    for c in range(tm // SGU_CHUNK):
        rs = slice(c * SGU_CHUNK, (c + 1) * SGU_CHUNK)
        r = _dot(w, vg[rs])
        mixed = r[:SGU_CHUNK]
        for g in range(1, N_GROUPS):
            mixed = jnp.where(lane_grp == g, r[g * SGU_CHUNK:(g + 1) * SGU_CHUNK], mixed)
        o_ref[rs, :] = (u[rs] * (mixed + b_ref[...])).astype(BF16)


def _sgu_mixer(pc, gn, ones_bd, w_all, b_exp):
    n = pc.shape[0]
    tm = TOKEN_TILE
    const = lambda i: (0, 0)
    return pl.pallas_call(
        _sgu_body,
        grid=(n // tm,),
        in_specs=[pl.BlockSpec((tm, 2 * MIX_PART), lambda i: (i, 0)),
                  pl.BlockSpec(gn.shape, const), pl.BlockSpec(ones_bd.shape, const),
                  pl.BlockSpec(w_all.shape, const), pl.BlockSpec(b_exp.shape, const)],
        out_specs=pl.BlockSpec((tm, MIX_PART), lambda i: (i, 0)),
        out_shape=jax.ShapeDtypeStruct((n, MIX_PART), BF16),
        compiler_params=_cparams("arbitrary"),
        name="spatial_gating",
    )(pc, gn, ones_bd, w_all, b_exp)


def _mla_body(n_qt, n_kv, q_ref, kl_ref, kc_ref, vl_ref, vc_ref, o_ref, sa_ref, sb_ref):
    qt = pl.program_id(1)
    n_lat = jnp.where(qt < n_qt, n_kv, 0)
    heads = [slice(h * MLA_SLOT, (h + 1) * MLA_SLOT) for h in range(N_GROUPS)]

    def scores(t, h, dst):
        off = pl.multiple_of(t * MLA_KV_TILE, MLA_KV_TILE)
        dst[h] = _dot_nt(kl_ref[pl.ds(off, MLA_KV_TILE), heads[h]], q_ref[:, heads[h]])

    def consume(t, h, src, m, acc):
        off = pl.multiple_of(t * MLA_KV_TILE, MLA_KV_TILE)
        s = src[h]
        mn = jnp.maximum(m, s.max(axis=0, keepdims=True))
        p = jnp.exp2(s - mn).astype(BF16)
        return mn, jnp.exp2(m - mn) * acc + _dot(vl_ref[heads[h], pl.ds(off, MLA_KV_TILE)], p)

    state = []
    for h, hs in enumerate(heads):
        scores(0, h, sa_ref)
        s = _dot_nt(kc_ref[:, hs], q_ref[:, hs])
        m = s.max(axis=0, keepdims=True)
        state.append((m, _dot(vc_ref[hs, :], jnp.exp2(s - m).astype(BF16))))

    def step(i, carry):
        t = 2 * i
        mid = []
        for h, (m, acc) in enumerate(carry):
            scores(t + 1, h, sb_ref)
            mid.append(consume(t, h, sa_ref, m, acc))
        out = []
        for h, (m, acc) in enumerate(mid):
            scores(jnp.minimum(t + 2, n_kv - 1), h, sa_ref)
            out.append(consume(t + 1, h, sb_ref, m, acc))
        return tuple(out)

    state = lax.fori_loop(0, n_lat // 2, step, tuple(state))
    out_t = jnp.concatenate([acc[:MLA_V] / acc[MLA_V:MLA_V + 1] for _, acc in state], axis=0)
    o_ref[...] = out_t.T.astype(BF16)


def _mla_mixer(n_batch, seq, ctx_len, qd, kd, vdt):
    n = qd.shape[0]
    slots = N_GROUPS * MLA_SLOT
    n_qt, in_specs, o_spec = _attn_specs(n_batch, seq, ctx_len, slots, slots, slots, N_GROUPS * MLA_V)
    lat_blocks = n_batch * n_qt
    in_specs[3] = pl.BlockSpec((slots, seq), lambda b, t: (0, b))
    in_specs[4] = pl.BlockSpec((slots, ctx_len), lambda b, t: (0, lat_blocks + b))
    assert seq % MLA_KV_TILE == 0
    return pl.pallas_call(
        functools.partial(_mla_body, n_qt, seq // MLA_KV_TILE),
        grid=(n_batch, n_qt + 1),
        in_specs=in_specs,
        out_specs=o_spec,
        out_shape=jax.ShapeDtypeStruct((n, N_GROUPS * MLA_V), BF16),
        scratch_shapes=[pltpu.VMEM((N_GROUPS, MLA_KV_TILE, ATTN_TILE), F32)] * 2,
        compiler_params=_cparams("arbitrary", "arbitrary"),
        name="latent_attention",
    )(qd, kd, kd, vdt, vdt)


def _outproj_body(ya_ref, yb_ref, yc_ref, yd_ref, x_ref, m_ref, g2_ref, w_ref, rw_ref, rb_ref,
                  xo_ref, hx_ref, r_ref):
    tm = x_ref.shape[0]
    m = m_ref[0]
    y = _dot(ya_ref[...], w_ref[0:MIX_PART, :])
    for k, ref in enumerate((yb_ref, yc_ref, yd_ref), start=1):
        y = y + _dot(ref[...], w_ref[k * MIX_PART:(k + 1) * MIX_PART, :])
    x = x_ref[...] + m[2:3] * y
    xo_ref[...] = x
    h = _rms(x, g2_ref[...]) * (1.0 + m[4:5]) + m[3:4]

    scores = jax.nn.sigmoid(_dot_nt(rw_ref[...], h, precision=HIGHEST))
    biased = scores + rb_ref[...]
    E = EXPERTS_PER_GROUP
    bk = [biased[8 * k:8 * k + 8] for k in range(E)]
    sk = [scores[8 * k:8 * k + 8] for k in range(E)]
    gs = None
    for a in range(E):
        for b in range(a + 1, E):
            pair = bk[a] + bk[b]
            gs = pair if gs is None else jnp.maximum(gs, pair)
    best = gs[0:1]
    idx = jnp.zeros((1, tm), jnp.int32)
    for g in range(1, N_EXPERT_GROUPS):
        better = gs[g:g + 1] > best
        idx = jnp.where(better, g, idx)
        best = jnp.where(better, gs[g:g + 1], best)
    in_grp = lax.broadcasted_iota(jnp.int32, (8, tm), 0) == idx
    wk = []
    for k in range(E):
        rank = jnp.zeros((8, tm), jnp.int32)
        for j in range(E):
            if j != k:
                ahead = (bk[j] > bk[k]) | ((bk[j] == bk[k]) & (j < k))
                rank = rank + ahead.astype(jnp.int32)
        wk.append(jnp.where((rank < 2) & in_grp, sk[k], 0.0).sum(axis=0, keepdims=True))
    den = wk[0] + wk[1] + wk[2] + wk[3]
    ri = lax.broadcasted_iota(jnp.int32, (8, tm), 0)
    out = jnp.where(ri == E, idx.astype(F32), 0.0)
    for k in range(E):
        out = jnp.where(ri == k, wk[k] / den, out)
    r_ref[...] = out

    for s in range(FEAT_ROWS):
        hx_ref[:, s, :] = h[:, s * LANES:(s + 1) * LANES]
    hx_ref[:, FEAT_ROWS, :] = jnp.concatenate([out, jnp.zeros((LANES - 8, tm), F32)], axis=0).T
    for s in range(FEAT_ROWS + 1, SLAB_ROWS):
        hx_ref[:, s, :] = jnp.zeros((tm, LANES), F32)


def _out_projection(l, n_lat_tiles_per_batch, n_batch, ys, x, mod_r, g2, w, rw, rb):
    n, d = x.shape
    tm = TOKEN_TILE
    row = lambda i: (i, 0)
    const = lambda i: (0, 0)
    modrow = lambda i: (l * 8 + jnp.minimum(i // n_lat_tiles_per_batch, n_batch), 0, 0)
    in_specs = [pl.BlockSpec((tm, MIX_PART), row)] * 4 + [
        pl.BlockSpec((tm, d), row), pl.BlockSpec((1, N_MOD, d), modrow), pl.BlockSpec((1, d), const),
        pl.BlockSpec(w.shape, const), pl.BlockSpec(rw.shape, const), pl.BlockSpec(rb.shape, const)]
    return pl.pallas_call(
        _outproj_body,
        grid=(n // tm,),
        in_specs=in_specs,
        out_specs=[pl.BlockSpec((tm, d), row), pl.BlockSpec((tm, SLAB_ROWS, LANES), lambda i: (i, 0, 0)),
                   pl.BlockSpec((8, tm), lambda i: (0, i))],
        out_shape=[jax.ShapeDtypeStruct((n, d), F32), jax.ShapeDtypeStruct((n, SLAB_ROWS, LANES), F32),
                   jax.ShapeDtypeStruct((8, n), F32)],
        compiler_params=_cparams("arbitrary"),
        name="out_projection_routing",
    )(*ys, x, mod_r, g2, w, rw, rb)


def _moe_body(tg_ref, tv_ref, src_ref, dst_ref, hx_hbm, wg_ref, wu_ref, wd_ref, y_hbm,
              hbuf, ybuf, wgb, wub, wdb, gsem, ssem):
    i = pl.program_id(0)
    n_t = pl.num_programs(0)
    tm = hbuf.shape[1]
    slot = i % 2
    grp = tg_ref[i]
    prev = tg_ref[jnp.maximum(i - 1, 0)]

    def gather_copy(r, t, sl):
        return pltpu.make_async_copy(hx_hbm.at[src_ref[t * tm + r]], hbuf.at[sl, r], gsem.at[sl])

    def scatter_copy(r, t, sl):
        return pltpu.make_async_copy(ybuf.at[sl, r], y_hbm.at[dst_ref[t * tm + r]], ssem.at[sl])

    def start_all(copy, t, sl):
        def body(r, c):
            copy(r, t, sl).start()
            return c
        lax.fori_loop(0, tm, body, 0, unroll=8)

    def wait_gather(sl):
        pltpu.make_async_copy(hx_hbm.at[pl.ds(0, tm)], hbuf.at[sl], gsem.at[sl]).wait()

    def wait_scatter(sl):
        pltpu.make_async_copy(ybuf.at[sl], y_hbm.at[pl.ds(0, tm)], ssem.at[sl]).wait()

    @pl.when(i == 0)
    def _():
        start_all(gather_copy, 0, 0)

    @pl.when((i + 1 < n_t) & (tv_ref[jnp.minimum(i + 1, n_t - 1)] > 0))
    def _():
        start_all(gather_copy, i + 1, 1 - slot)

    @pl.when((i == 0) | (grp != prev))
    def _():
        wgb[...] = wg_ref[0].astype(BF16)
        wub[...] = wu_ref[0].astype(BF16)
        wdb[...] = wd_ref[0].astype(BF16)

    @pl.when(i >= 2)
    def _():
        wait_scatter(slot)

    @pl.when(tv_ref[i] > 0)
    def _():
        wait_gather(slot)
        h = jnp.concatenate([hbuf[slot, :, s, :] for s in range(FEAT_ROWS)], axis=1).astype(BF16)
        gates = hbuf[slot, :, FEAT_ROWS, :]
        acc = jnp.zeros((tm, FEAT_ROWS * LANES), F32)
        for k in range(EXPERTS_PER_GROUP):
            hid = _silu(_dot(h, wgb[k])) * _dot(h, wub[k]) * gates[:, k:k + 1]
            acc = acc + _dot(hid.astype(BF16), wdb[k])
        for s in range(FEAT_ROWS):
            ybuf[slot, :, s, :] = acc[:, s * LANES:(s + 1) * LANES]

    @pl.when(tv_ref[i] == 0)
    def _():
        ybuf[slot] = jnp.zeros(ybuf.shape[1:], F32)

    start_all(scatter_copy, i, slot)

    @pl.when(i == n_t - 1)
    def _():
        wait_scatter(slot)

        @pl.when(n_t > 1)
        def _():
            wait_scatter(1 - slot)


def _grouped_experts(l, tile_grp, tile_ok, src, dst, hx, w_gate, w_up, w_down):
    n_pad = src.shape[0]
    tm = MOE_TILE
    E = EXPERTS_PER_GROUP
    d = FEAT_ROWS * LANES
    wmap = lambda i, tg, tv, sr, ds: (l, tg[i], 0, 0)
    grid_spec = pltpu.PrefetchScalarGridSpec(
        num_scalar_prefetch=4,
        grid=(n_pad // tm,),
        in_specs=[pl.BlockSpec(memory_space=pl.ANY),
                  pl.BlockSpec((1, E, d, D_EXPERT), wmap),
                  pl.BlockSpec((1, E, d, D_EXPERT), wmap),
                  pl.BlockSpec((1, E, D_EXPERT, d), wmap)],
        out_specs=pl.BlockSpec(memory_space=pl.ANY),
        scratch_shapes=[pltpu.VMEM((2, tm, SLAB_ROWS, LANES), F32), pltpu.VMEM((2, tm, FEAT_ROWS, LANES), F32),
                        pltpu.VMEM((E, d, D_EXPERT), BF16), pltpu.VMEM((E, d, D_EXPERT), BF16),
                        pltpu.VMEM((E, D_EXPERT, d), BF16),
                        pltpu.SemaphoreType.DMA((2,)), pltpu.SemaphoreType.DMA((2,))])
    return pl.pallas_call(
        _moe_body,
        grid_spec=grid_spec,
        out_shape=jax.ShapeDtypeStruct((n_pad, FEAT_ROWS, LANES), F32),
        compiler_params=_cparams("arbitrary"),
        name="grouped_experts",
    )(tile_grp, tile_ok, src, dst, hx, w_gate, w_up, w_down)


def _dispatch_plan(route, n_pad):
    n = route.shape[1]
    tm = MOE_TILE
    grp = route[EXPERTS_PER_GROUP].astype(jnp.int32)
    onehot = (grp[:, None] == jnp.arange(N_EXPERT_GROUPS)[None, :]).astype(jnp.int32)
    csum = jnp.cumsum(onehot, axis=0)
    counts = csum[-1]
    rank = jnp.sum(csum * onehot, axis=1) - 1
    padded = ((counts + tm - 1) // tm) * tm
    ends = jnp.cumsum(padded)
    pos = jnp.sum((ends - padded)[None, :] * onehot, axis=1) + rank
    owner = jnp.zeros((n_pad,), jnp.int32).at[pos].set(jnp.arange(1, n + 1, dtype=jnp.int32))
    used = owner > 0
    src = jnp.maximum(owner - 1, 0)
    dst = jnp.where(used, owner - 1, n - 1 + jnp.cumsum(1 - used.astype(jnp.int32)))
    tile_start = jnp.arange(n_pad // tm, dtype=jnp.int32) * tm
    tile_grp = jnp.minimum(jnp.sum(tile_start[:, None] >= ends[None, :], axis=1), N_EXPERT_GROUPS - 1)
    tile_ok = (tile_start < ends[-1]).astype(jnp.int32)
    return tile_grp.astype(jnp.int32), tile_ok, src, dst


def _slab_rows(ref):
    return jnp.concatenate([ref[:, s, :] for s in range(FEAT_ROWS)], axis=1)


def _final_body(x_ref, y_ref, m_ref, g_ref, o_ref):
    x = x_ref[...] + m_ref[0, 5:6, :] * _slab_rows(y_ref)
    o_ref[...] = _rms(x, g_ref[...])


def _final_norm(l, n_lat, n_lat_tiles_per_batch, x, y, mod_r, g):
    d = x.shape[1]
    tm = TOKEN_TILE
    row = lambda i: (i, 0)
    return pl.pallas_call(
        _final_body,
        grid=(n_lat // tm,),
        in_specs=[pl.BlockSpec((tm, d), row), pl.BlockSpec((tm, FEAT_ROWS, LANES), lambda i: (i, 0, 0)),
                  pl.BlockSpec((1, N_MOD, d), lambda i: (l * 8 + i // n_lat_tiles_per_batch, 0, 0)),
                  pl.BlockSpec((1, d), lambda i: (0, 0))],
        out_specs=pl.BlockSpec((tm, d), row),
        out_shape=jax.ShapeDtypeStruct((n_lat, d), F32),
        compiler_params=_cparams("arbitrary"),
        name="final_norm",
    )(x, y, mod_r, g)


def _rope_tables(seq, n_batch, n_ctx_tokens):
    t = np.arange(seq)
    nf = MLA_ROPE // 4
    inv = jnp.asarray(ROPE_BASE, F32) ** (-jnp.arange(nf, dtype=F32) / nf)
    rang = jnp.asarray(t // GRID_W, F32)[:, None] * inv
    cang = jnp.asarray(t % GRID_W, F32)[:, None] * inv
    cr, sr, cc, sc = jnp.cos(rang), jnp.sin(rang), jnp.cos(cang), jnp.sin(cang)
    one = jnp.ones((seq, MLA_NOPE), F32)
    zero_tail = jnp.zeros((seq, MLA_SLOT - MLA_NOPE - MLA_ROPE), F32)
    cosp = jnp.concatenate([one, cr, cr, cc, cc, zero_tail], axis=1)
    sinp = jnp.concatenate([0 * one, -sr, sr, -sc, sc, zero_tail], axis=1)
    ctx_cos = jnp.concatenate([jnp.ones((n_ctx_tokens, MLA_NOPE + MLA_ROPE), F32),
                               jnp.zeros((n_ctx_tokens, MLA_SLOT - MLA_NOPE - MLA_ROPE), F32)], axis=1)
    cosp = jnp.concatenate([jnp.tile(cosp, (n_batch, 1)), ctx_cos], axis=0)
    sinp = jnp.concatenate([jnp.tile(sinp, (n_batch, 1)), jnp.zeros((n_ctx_tokens, MLA_SLOT), F32)], axis=0)
    return cosp, sinp


_ROPE_SWAP = np.concatenate([np.arange(8, 16), np.arange(0, 8), np.arange(24, 32), np.arange(16, 24)])


def _pad_last(a, before, after):
    return jnp.pad(a, [(0, 0)] * (a.ndim - 1) + [(before, after)])


def _layouts(w_in, w_out, pool_w, sgu_w, sgu_b, mla_w_uq, mla_w_ukv, router_w, router_b):
    L = w_in.shape[0]
    tail = MLA_SLOT - MLA_NOPE - MLA_ROPE
    kr = w_in[:, :, OFF_KR:]
    w = jnp.concatenate([w_in[:, :, :OFF_KR], _pad_last(kr, MLA_NOPE, tail),
                         _pad_last(kr[:, :, _ROPE_SWAP], MLA_NOPE, tail)], axis=-1).astype(BF16)
    uq = mla_w_uq.reshape(L, MLA_Q_RANK, N_GROUPS, MLA_NOPE + MLA_ROPE)
    q1 = _pad_last(uq, 0, tail).reshape(L, MLA_Q_RANK, N_GROUPS * MLA_SLOT)
    q2 = _pad_last(uq[..., MLA_NOPE:][..., _ROPE_SWAP], MLA_NOPE, tail).reshape(L, MLA_Q_RANK, N_GROUPS * MLA_SLOT)
    wq = jnp.concatenate([q1, q2], axis=-1).astype(BF16)
    ukv = mla_w_ukv.reshape(L, MLA_KV_RANK, N_GROUPS, MLA_NOPE + MLA_V)
    kn = _pad_last(ukv[..., :MLA_NOPE], 0, MLA_SLOT - MLA_NOPE).reshape(L, MLA_KV_RANK, N_GROUPS * MLA_SLOT)
    vv = _pad_last(ukv[..., MLA_NOPE:], 0, MLA_SLOT - MLA_V).reshape(L, MLA_KV_RANK, N_GROUPS * MLA_SLOT)
    wkv = kn.astype(BF16)
    wvt = jnp.swapaxes(vv, 1, 2).astype(BF16)
    eye = jnp.eye(N_GROUPS, dtype=F32)
    pool_bd = jnp.einsum('lgcd,gh->lgchd', pool_w, eye).reshape(L, MIX_PART, MIX_PART).astype(BF16)
    ones_bd = jnp.asarray(np.kron(np.eye(N_GROUPS), np.full((GROUP_DIM, GROUP_DIM), 1.0 / GROUP_DIM)), F32)
    sgu_all = sgu_w.reshape(L, N_GROUPS * SGU_CHUNK, SGU_CHUNK).astype(BF16)
    sgu_bias = jnp.repeat(jnp.swapaxes(sgu_b, 1, 2), GROUP_DIM, axis=2)
    rw = router_w.T.reshape(N_EXPERT_GROUPS, EXPERTS_PER_GROUP, -1).transpose(1, 0, 2)
    rw = jnp.pad(rw, ((0, 0), (0, 8 - N_EXPERT_GROUPS), (0, 0))).reshape(8 * EXPERTS_PER_GROUP, -1)
    rb = router_b.reshape(N_EXPERT_GROUPS, EXPERTS_PER_GROUP).T
    rb = jnp.pad(rb, ((0, 0), (0, 8 - N_EXPERT_GROUPS))).reshape(8 * EXPERTS_PER_GROUP, 1)
    return w, wq, wkv, wvt, w_out.astype(BF16), pool_bd, ones_bd, sgu_all, sgu_bias, rw, rb


def kernel(x, c, ctx, c_ctx, ada_w, ada_b, norm1_g, norm2_g, w_in, w_out, pool_w, pool_s, na_rpb,
           sgu_norm_g, sgu_w, sgu_b, mla_q_norm_g, mla_w_uq, mla_kv_norm_g, mla_w_ukv,
           router_w, router_b, moe_w_gate, moe_w_up, moe_w_down, final_g):
    B, S, D = x.shape
    CTX = ctx.shape[1]
    L = ada_w.shape[0]
    n_lat, n_ctx = B * S, B * CTX
    n_all = n_lat + n_ctx
    assert B < 8 and S % TOKEN_TILE == 0 and n_ctx % TOKEN_TILE == 0
    tiles_per_batch = S // TOKEN_TILE
    n_pad = n_all + N_EXPERT_GROUPS * MOE_TILE

    c_all = jnp.concatenate([c, c_ctx[None, :], jnp.zeros((8 - B - 1, D), F32)], axis=0)
    mod_r = _modulation(c_all, ada_w, ada_b).reshape(L * 8, N_MOD, D)
    w, wq, wkv, wvt, wo, pool_bd, ones_bd, sgu_all, sgu_bias, rw, rb = _layouts(
        w_in, w_out, pool_w, sgu_w, sgu_b, mla_w_uq, mla_w_ukv, router_w, router_b)
    bias_tab = _na_bias_tables(na_rpb, S // GRID_W)
    cosp, sinp = _rope_tables(S, B, n_ctx)

    xs = jnp.concatenate([x.reshape(n_lat, D), ctx.reshape(n_ctx, D)], axis=0)
    y = None
    for l in range(L):
        xs, pa, qb, kb, vb, pc, qd, kd, vd = _in_projection(
            l, tiles_per_batch, B, xs, y, mod_r, norm1_g[l][None], w[l], wq[l], mla_q_norm_g[l][None],
            wkv[l], wvt[l], mla_kv_norm_g[l][None], cosp, sinp)
        ya = _pool_mixer(n_lat, S, CTX, pa, pool_bd[l], pool_s[l][None])
        yb = _na_mixer(l, B, S, CTX, qb, kb, vb, bias_tab)
        yc = _sgu_mixer(pc, sgu_norm_g[l][None], ones_bd, sgu_all[l], sgu_bias[l])
        yd = _mla_mixer(B, S, CTX, qd, kd, vd)
        xs, hx, route = _out_projection(l, tiles_per_batch, B, (ya, yb, yc, yd), xs, mod_r, norm2_g[l][None],
                                        wo[l], rw, rb)
        tile_grp, tile_ok, src, dst = _dispatch_plan(route, n_pad)
        y = _grouped_experts(l, tile_grp, tile_ok, src, dst, hx, moe_w_gate, moe_w_up, moe_w_down)
    out = _final_norm(L - 1, n_lat, tiles_per_batch, xs, y, mod_r, final_g[None])
    return out.reshape(B, S, D)
```

```python
import functools

import numpy as np
import jax
import jax.numpy as jnp
from jax import lax
from jax.experimental import pallas as pl
from jax.experimental.pallas import tpu as pltpu

F32 = jnp.float32
BF16 = jnp.bfloat16
HIGHEST = lax.Precision.HIGHEST

GRID_W = 64
POOL_WINDOWS = (2, 4, 8, 16)
GROUP_DIM = 64
N_GROUPS = 4
MIX_PART = N_GROUPS * GROUP_DIM
NA_WIN_ROWS = 8
NA_WIN_COLS = 16
NA_Q_ROWS = 4
NA_K_ROWS = NA_Q_ROWS + NA_WIN_ROWS
SGU_CHUNK = 128
MLA_Q_RANK = 256
MLA_KV_RANK = 128
MLA_NOPE = 64
MLA_ROPE = 32
MLA_V = 64
MLA_SLOT = 128
MLA_SCALE = (MLA_NOPE + MLA_ROPE) ** -0.5
LOG2_E = float(np.log2(np.e))
ROPE_BASE = 10000.0
N_EXPERTS = 16
N_EXPERT_GROUPS = 4
EXPERTS_PER_GROUP = 4
D_EXPERT = 256
N_MOD = 6
EPS = 1e-6
NEG_INF = -1e30

OFF_B = MIX_PART
OFF_C = OFF_B + 3 * MIX_PART
OFF_D = OFF_C + 2 * MIX_PART
OFF_KR = OFF_D + MLA_Q_RANK + MLA_KV_RANK
W_IN_COLS = OFF_KR + 2 * MLA_SLOT

LANES = 128
FEAT_ROWS = 8
SLAB_ROWS = 16

TOKEN_TILE = 512
ATTN_TILE = 256
MLA_KV_TILE = 512
MOE_TILE = 512
VMEM_LIMIT = 56 * 1024 * 1024


def _cparams(*sem):
    return pltpu.CompilerParams(dimension_semantics=sem, vmem_limit_bytes=VMEM_LIMIT)


def _dot(a, b):
    return jnp.dot(a, b, preferred_element_type=F32)


def _dot_nt(a, b, precision=None):
    return lax.dot_general(a, b, (((1,), (1,)), ((), ())), precision=precision,
                           preferred_element_type=F32)


def _rms(x, g):
    return x * lax.rsqrt(jnp.mean(x * x, axis=-1, keepdims=True) + EPS) * g


def _silu(x):
    return x * jax.nn.sigmoid(x)


def _mod_body(c_ref, w_ref, b_ref, o_ref):
    o_ref[0] = jnp.dot(_silu(c_ref[...]), w_ref[0], precision=HIGHEST,
                       preferred_element_type=F32) + b_ref[0]


def _modulation(c_all, ada_w, ada_b):
    L, D, ND = ada_w.shape
    tn = 1536
    return pl.pallas_call(
        _mod_body,
        grid=(L, ND // tn),
        in_specs=[pl.BlockSpec((8, D), lambda l, j: (0, 0)),
                  pl.BlockSpec((1, D, tn), lambda l, j: (l, 0, j)),
                  pl.BlockSpec((1, 1, tn), lambda l, j: (l, 0, j))],
        out_specs=pl.BlockSpec((1, 8, tn), lambda l, j: (l, 0, j)),
        out_shape=jax.ShapeDtypeStruct((L, 8, ND), F32),
        compiler_params=_cparams("arbitrary", "arbitrary"),
        name="modulation",
    )(c_all, ada_w, ada_b.reshape(L, 1, ND))


def _inproj_body(has_res, *refs):
    if has_res:
        (x_ref, y_ref, mp_ref, m_ref, g1_ref, w_ref, wq_ref, gq_ref, wkv_ref, wvt_ref, gkv_ref, cos_ref, sin_ref,
         xo_ref, pa_ref, qb_ref, kb_ref, vb_ref, pc_ref, qd_ref, kd_ref, vd_ref) = refs
        x = x_ref[...] + mp_ref[0, 5:6, :] * _slab_rows(y_ref)
        xo_ref[...] = x
    else:
        (x_ref, m_ref, g1_ref, w_ref, wq_ref, gq_ref, wkv_ref, wvt_ref, gkv_ref, cos_ref, sin_ref,
         pa_ref, qb_ref, kb_ref, vb_ref, pc_ref, qd_ref, kd_ref, vd_ref) = refs
        x = x_ref[...]
    m = m_ref[0]
    hb = (_rms(x, g1_ref[...]) * (1.0 + m[1:2]) + m[0:1]).astype(BF16)

    def proj(a, b):
        return _dot(hb, w_ref[:, a:b])

    pa_ref[...] = proj(0, OFF_B)
    qb_ref[...] = (proj(OFF_B, OFF_B + MIX_PART) * (GROUP_DIM ** -0.5)).astype(BF16)
    kb_ref[...] = proj(OFF_B + MIX_PART, OFF_B + 2 * MIX_PART).astype(BF16)
    vb_ref[...] = proj(OFF_B + 2 * MIX_PART, OFF_C).astype(BF16)
    pc_ref[...] = proj(OFF_C, OFF_D)

    cosp = cos_ref[...]
    sinp = sin_ref[...]
    qn = _rms(proj(OFF_D, OFF_D + MLA_Q_RANK), gq_ref[...]).astype(BF16)
    qq = _dot(qn, wq_ref[...])
    half = N_GROUPS * MLA_SLOT
    for h in range(N_GROUPS):
        a = h * MLA_SLOT
        q = qq[:, a:a + MLA_SLOT] * cosp + qq[:, half + a:half + a + MLA_SLOT] * sinp
        qd_ref[:, a:a + MLA_SLOT] = (q * (MLA_SCALE * LOG2_E)).astype(BF16)
    kvn = _rms(proj(OFF_D + MLA_Q_RANK, OFF_KR), gkv_ref[...]).astype(BF16)
    kk = _dot(kvn, wkv_ref[...])
    kr = proj(OFF_KR, OFF_KR + MLA_SLOT) * cosp + proj(OFF_KR + MLA_SLOT, W_IN_COLS) * sinp
    for h in range(N_GROUPS):
        a = h * MLA_SLOT
        kd_ref[:, a:a + MLA_SLOT] = (kk[:, a:a + MLA_SLOT] + kr).astype(BF16)
    ones_row = (lax.broadcasted_iota(jnp.int32, (half, 1), 0) % MLA_SLOT == MLA_V).astype(F32)
    vd_ref[...] = (_dot_nt(wvt_ref[...], kvn) + ones_row).astype(BF16)


def _in_projection(l, n_lat_tiles_per_batch, n_batch, x, y, mod_r, g1, w, wq, gq, wkv, wvt, gkv, cosp, sinp):
    n, d = x.shape
    tm = TOKEN_TILE
    has_res = y is not None

    def row(i):
        return (i, 0)

    def modrow(layer):
        return lambda i: (layer * 8 + jnp.minimum(i // n_lat_tiles_per_batch, n_batch), 0, 0)

    def const2(i):
        return (0, 0)

    tok = lambda c: pl.BlockSpec((tm, c), row)
    mod_spec = lambda layer: pl.BlockSpec((1, N_MOD, d), modrow(layer))
    in_specs = [tok(d)]
    args = [x]
    if has_res:
        in_specs += [pl.BlockSpec((tm, FEAT_ROWS, LANES), lambda i: (i, 0, 0)), mod_spec(l - 1)]
        args += [y, mod_r]
    in_specs += [mod_spec(l), pl.BlockSpec((1, d), const2), pl.BlockSpec(w.shape, const2),
                 pl.BlockSpec(wq.shape, const2), pl.BlockSpec(gq.shape, const2),
                 pl.BlockSpec(wkv.shape, const2), pl.BlockSpec(wvt.shape, const2), pl.BlockSpec(gkv.shape, const2),
                 tok(MLA_SLOT), tok(MLA_SLOT)]
    args += [mod_r, g1, w, wq, gq, wkv, wvt, gkv, cosp, sinp]
    outs = [(MIX_PART, F32), (MIX_PART, BF16), (MIX_PART, BF16), (MIX_PART, BF16), (2 * MIX_PART, F32),
            (N_GROUPS * MLA_SLOT, BF16), (N_GROUPS * MLA_SLOT, BF16)]
    if has_res:
        outs = [(d, F32)] + outs
    slots = N_GROUPS * MLA_SLOT
    res = pl.pallas_call(
        functools.partial(_inproj_body, has_res),
        grid=(n // tm,),
        in_specs=in_specs,
        out_specs=[tok(c) for c, _ in outs] + [pl.BlockSpec((slots, tm), lambda i: (0, i))],
        out_shape=[jax.ShapeDtypeStruct((n, c), t) for c, t in outs] + [jax.ShapeDtypeStruct((slots, n), BF16)],
        compiler_params=_cparams("arbitrary"),
        name="in_projection",
    )(*args)
    if not has_res:
        res = [x] + list(res)
    return res


def _pool_body(n_lat, seq, ctx_len, prev_ref, cur_ref, next_ref, w_ref, s_ref, o_ref):
    tb = cur_ref.shape[0]
    ext = jnp.concatenate([prev_ref[...], cur_ref[...], next_ref[...]], axis=0)
    n = tb + 16
    g = pl.program_id(0) * tb - 8 + lax.broadcasted_iota(jnp.int32, (n, 1), 0)
    is_lat = g < n_lat
    length = jnp.where(is_lat, seq, ctx_len)
    p = jnp.where(is_lat, g & (seq - 1), (g - n_lat) & (ctx_len - 1))

    def shifted(a, j):
        r = pltpu.roll(a, (-j) % n, axis=0)
        ok = (p + j >= 0) & (p + j < length)
        return jnp.where(ok, r, 0.0)

    before1 = shifted(ext, -1)
    before2 = before1 + shifted(before1, -1)
    before4 = before2 + shifted(before2, -2)
    before8 = before4 + shifted(before4, -4)
    after2 = ext + shifted(ext, 1)
    after4 = after2 + shifted(after2, 2)
    after8 = after4 + shifted(after4, 4)
    lane_grp = lax.broadcasted_iota(jnp.int32, (1, MIX_PART), 1) // GROUP_DIM
    sl = slice(8, 8 + tb)
    tot = jnp.where(lane_grp == 0, (before1 + ext)[sl],
                    jnp.where(lane_grp == 1, (before2 + after2)[sl],
                              jnp.where(lane_grp == 2, (before4 + after4)[sl], (before8 + after8)[sl])))
    half = jnp.where(lane_grp == 0, POOL_WINDOWS[0] // 2,
                     jnp.where(lane_grp == 1, POOL_WINDOWS[1] // 2,
                               jnp.where(lane_grp == 2, POOL_WINDOWS[2] // 2, POOL_WINDOWS[3] // 2)))
    pc = p[sl]
    cnt = jnp.minimum(pc + half, length[sl]) - jnp.maximum(pc - half, 0)
    dlt = tot / cnt.astype(F32) - ext[sl]
    o_ref[...] = (_dot(dlt.astype(BF16), w_ref[...]) * s_ref[...]).astype(BF16)


def _pool_mixer(n_lat, seq, ctx_len, pa, w_bd, s):
    n = pa.shape[0]
    tb = 1024
    assert seq & (seq - 1) == 0 and ctx_len & (ctx_len - 1) == 0
    assert n_lat % tb == 0 and n % tb == 0
    nb8 = n // 8
    return pl.pallas_call(
        functools.partial(_pool_body, n_lat, seq, ctx_len),
        grid=(n // tb,),
        in_specs=[pl.BlockSpec((8, MIX_PART), lambda i: (jnp.maximum(i * (tb // 8) - 1, 0), 0)),
                  pl.BlockSpec((tb, MIX_PART), lambda i: (i, 0)),
                  pl.BlockSpec((8, MIX_PART), lambda i: (jnp.minimum((i + 1) * (tb // 8), nb8 - 1), 0)),
                  pl.BlockSpec(w_bd.shape, lambda i: (0, 0)),
                  pl.BlockSpec(s.shape, lambda i: (0, 0))],
        out_specs=pl.BlockSpec((tb, MIX_PART), lambda i: (i, 0)),
        out_shape=jax.ShapeDtypeStruct((n, MIX_PART), BF16),
        compiler_params=_cparams("arbitrary"),
        name="pool_mixer",
    )(pa, pa, pa, w_bd, s)


def _softmax_pv(parts):
    m = parts[0][0].max(axis=-1, keepdims=True)
    for s, _ in parts[1:]:
        m = jnp.maximum(m, s.max(axis=-1, keepdims=True))
    den = 0.0
    out = 0.0
    for s, v in parts:
        p = jnp.exp(s - m)
        den = den + p.sum(axis=-1, keepdims=True)
        out = out + _dot(p.astype(BF16), v)
    return out / den


def _na_body(rows, n_qt, q_ref, kl_ref, kc_ref, vl_ref, vc_ref, bias_ref, o_ref):
    qt = pl.program_id(1)
    q = q_ref[...]
    kc = kc_ref[...]
    vc = vc_ref[...]
    lane_grp = lax.broadcasted_iota(jnp.int32, (1, MIX_PART), 1) // GROUP_DIM
    zero = jnp.zeros_like(q)

    @pl.when(qt < n_qt)
    def _():
        k0 = jnp.clip(NA_Q_ROWS * qt - NA_WIN_ROWS // 2, 0, rows - NA_K_ROWS)
        off = pl.multiple_of(k0 * GRID_W, NA_Q_ROWS * GRID_W)
        kw = kl_ref[pl.ds(off, NA_K_ROWS * GRID_W), :]
        vw = vl_ref[pl.ds(off, NA_K_ROWS * GRID_W), :]
        o = jnp.zeros(q.shape, F32)
        for h in range(N_GROUPS):
            qh = jnp.where(lane_grp == h, q, zero)
            oh = _softmax_pv([(_dot_nt(qh, kw) + bias_ref[0, h], vw), (_dot_nt(qh, kc), vc)])
            o = jnp.where(lane_grp == h, oh, o)
        o_ref[...] = o.astype(BF16)

    @pl.when(qt == n_qt)
    def _():
        o = jnp.zeros(q.shape, F32)
        for h in range(N_GROUPS):
            qh = jnp.where(lane_grp == h, q, zero)
            oh = _softmax_pv([(_dot_nt(qh, kc), vc)])
            o = jnp.where(lane_grp == h, oh, o)
        o_ref[...] = o.astype(BF16)


def _attn_specs(n_batch, seq, ctx_len, width_q, width_k, width_v, width_o):
    n_qt = seq // ATTN_TILE
    lat_blocks = n_batch * n_qt
    assert ctx_len == ATTN_TILE

    def qmap(b, t):
        return (jnp.where(t < n_qt, b * n_qt + t, lat_blocks + b), 0)

    q_spec = pl.BlockSpec((ATTN_TILE, width_q), qmap)
    kl_spec = pl.BlockSpec((seq, width_k), lambda b, t: (b, 0))
    kc_spec = pl.BlockSpec((ctx_len, width_k), lambda b, t: (lat_blocks + b, 0))
    vl_spec = pl.BlockSpec((seq, width_v), lambda b, t: (b, 0))
    vc_spec = pl.BlockSpec((ctx_len, width_v), lambda b, t: (lat_blocks + b, 0))
    o_spec = pl.BlockSpec((ATTN_TILE, width_o), qmap)
    return n_qt, [q_spec, kl_spec, kc_spec, vl_spec, vc_spec], o_spec


def _na_mixer(l, n_batch, seq, ctx_len, qb, kb, vb, bias_tab):
    n = qb.shape[0]
    rows = seq // GRID_W
    n_qt, in_specs, o_spec = _attn_specs(n_batch, seq, ctx_len, MIX_PART, MIX_PART, MIX_PART, MIX_PART)
    assert ATTN_TILE == NA_Q_ROWS * GRID_W and rows >= NA_K_ROWS + NA_Q_ROWS

    def bias_map(b, t):
        kind = jnp.where(t == 0, 0, jnp.where(t >= n_qt - 1, 2, 1))
        return (l * 3 + kind, 0, 0, 0)

    in_specs.append(pl.BlockSpec((1,) + bias_tab.shape[1:], bias_map))
    return pl.pallas_call(
        functools.partial(_na_body, rows, n_qt),
        grid=(n_batch, n_qt + 1),
        in_specs=in_specs,
        out_specs=o_spec,
        out_shape=jax.ShapeDtypeStruct((n, MIX_PART), BF16),
        compiler_params=_cparams("arbitrary", "arbitrary"),
        name="neighborhood_attention",
    )(qb, kb, kb, vb, vb, bias_tab)


def _na_bias_tables(na_rpb, rows):
    L, H = na_rpb.shape[:2]
    kinds = ((0, 0), (NA_Q_ROWS, 0), (rows - NA_Q_ROWS, rows - NA_K_ROWS))
    qc = np.arange(GRID_W)
    kc = np.arange(GRID_W)
    wsc = np.clip(qc - NA_WIN_COLS // 2, 0, GRID_W - NA_WIN_COLS)
    col_ok = (kc[None, :] >= wsc[:, None]) & (kc[None, :] < wsc[:, None] + NA_WIN_COLS)
    dc = np.clip(kc[None, :] - qc[:, None] + NA_WIN_COLS - 1, 0, 2 * NA_WIN_COLS - 2)
    slabs = jnp.where(col_ok, na_rpb[:, :, :, dc], NEG_INF).transpose(0, 1, 3, 2, 4)

    def masked(n):
        return jnp.full((L, H, GRID_W, n, GRID_W), NEG_INF, F32)

    tabs = []
    for r0, k0 in kinds:
        per_row = []
        for qr in range(NA_Q_ROWS):
            r = r0 + qr
            first = int(np.clip(r - NA_WIN_ROWS // 2, 0, rows - NA_WIN_ROWS)) - k0
            d0 = k0 + first - r + NA_WIN_ROWS - 1
            assert 0 <= first <= NA_K_ROWS - NA_WIN_ROWS and 0 <= d0 <= NA_WIN_ROWS - 1
            per_row.append(jnp.concatenate(
                [masked(first), slabs[:, :, :, d0:d0 + NA_WIN_ROWS, :], masked(NA_K_ROWS - NA_WIN_ROWS - first)],
                axis=3))
        tabs.append(jnp.stack(per_row, axis=2))
    return jnp.stack(tabs, axis=1).reshape(L * 3, H, NA_Q_ROWS * GRID_W, NA_K_ROWS * GRID_W)


def _gelu_tanh(x):
    return 0.5 * x * (1.0 + jnp.tanh(np.sqrt(2.0 / np.pi).astype(np.float32) * (x + 0.044715 * (x * x * x))))


def _sgu_body(pc_ref, gn_ref, ones_ref, w_ref, b_ref, o_ref):
    tm = pc_ref.shape[0]
    uv = _gelu_tanh(pc_ref[...])
    u = uv[:, :MIX_PART]
    v = uv[:, MIX_PART:]
    ms = jnp.dot(v * v, ones_ref[...], precision=HIGHEST, preferred_element_type=F32)
    vg = (v * lax.rsqrt(ms + EPS) * gn_ref[...]).astype(BF16)
    lane_grp = lax.broadcasted_iota(jnp.int32, (1, MIX_PART), 1) // GROUP_DIM
    w = w_ref[...]
    for c in range(tm // SGU_CHUNK):
        rs = slice(c * SGU_CHUNK, (c + 1) * SGU_CHUNK)
        r = _dot(w, vg[rs])
        mixed = r[:SGU_CHUNK]
        for g in range(1, N_GROUPS):
            mixed = jnp.where(lane_grp == g, r[g * SGU_CHUNK:(g + 1) * SGU_CHUNK], mixed)
        o_ref[rs, :] = (u[rs] * (mixed + b_ref[...])).astype(BF16)


def _sgu_mixer(pc, gn, ones_bd, w_all, b_exp):
    n = pc.shape[0]
    tm = TOKEN_TILE
    const = lambda i: (0, 0)
    return pl.pallas_call(
        _sgu_body,
        grid=(n // tm,),
        in_specs=[pl.BlockSpec((tm, 2 * MIX_PART), lambda i: (i, 0)),
                  pl.BlockSpec(gn.shape, const), pl.BlockSpec(ones_bd.shape, const),
                  pl.BlockSpec(w_all.shape, const), pl.BlockSpec(b_exp.shape, const)],
        out_specs=pl.BlockSpec((tm, MIX_PART), lambda i: (i, 0)),
        out_shape=jax.ShapeDtypeStruct((n, MIX_PART), BF16),
        compiler_params=_cparams("arbitrary"),
        name="spatial_gating",
    )(pc, gn, ones_bd, w_all, b_exp)


def _mla_body(n_qt, n_kv, q_ref, kl_ref, kc_ref, vl_ref, vc_ref, o_ref, sa_ref, sb_ref):
    qt = pl.program_id(1)
    n_lat = jnp.where(qt < n_qt, n_kv, 0)
    heads = [slice(h * MLA_SLOT, (h + 1) * MLA_SLOT) for h in range(N_GROUPS)]

    def scores(t, h, dst):
        off = pl.multiple_of(t * MLA_KV_TILE, MLA_KV_TILE)
        dst[h] = _dot_nt(kl_ref[pl.ds(off, MLA_KV_TILE), heads[h]], q_ref[:, heads[h]])

    def consume(t, h, src, m, acc):
        off = pl.multiple_of(t * MLA_KV_TILE, MLA_KV_TILE)
        s = src[h]
        mn = jnp.maximum(m, s.max(axis=0, keepdims=True))
        p = jnp.exp2(s - mn).astype(BF16)
        return mn, jnp.exp2(m - mn) * acc + _dot(vl_ref[heads[h], pl.ds(off, MLA_KV_TILE)], p)

    ctx_scores = [_dot_nt(kc_ref[:, hs], q_ref[:, hs]) for hs in heads]
    for h in range(N_GROUPS):
        scores(0, h, sa_ref)
    state = []
    for hs, s in zip(heads, ctx_scores):
        m = s.max(axis=0, keepdims=True)
        state.append((m, _dot(vc_ref[hs, :], jnp.exp2(s - m).astype(BF16))))

    def step(i, carry):
        t = 2 * i
        mid = []
        for h, (m, acc) in enumerate(carry):
            scores(t + 1, h, sb_ref)
            mid.append(consume(t, h, sa_ref, m, acc))
        out = []
        for h, (m, acc) in enumerate(mid):
            scores(jnp.minimum(t + 2, n_kv - 1), h, sa_ref)
            out.append(consume(t + 1, h, sb_ref, m, acc))
        return tuple(out)

    state = lax.fori_loop(0, n_lat // 2, step, tuple(state))
    out_t = jnp.concatenate([acc[:MLA_V] / acc[MLA_V:MLA_V + 1] for _, acc in state], axis=0)
    o_ref[...] = out_t.T.astype(BF16)


def _mla_mixer(n_batch, seq, ctx_len, qd, kd, vdt):
    n = qd.shape[0]
    slots = N_GROUPS * MLA_SLOT
    n_qt, in_specs, o_spec = _attn_specs(n_batch, seq, ctx_len, slots, slots, slots, N_GROUPS * MLA_V)
    lat_blocks = n_batch * n_qt
    in_specs[3] = pl.BlockSpec((slots, seq), lambda b, t: (0, b))
    in_specs[4] = pl.BlockSpec((slots, ctx_len), lambda b, t: (0, lat_blocks + b))
    assert seq % MLA_KV_TILE == 0
    return pl.pallas_call(
        functools.partial(_mla_body, n_qt, seq // MLA_KV_TILE),
        grid=(n_batch, n_qt + 1),
        in_specs=in_specs,
        out_specs=o_spec,
        out_shape=jax.ShapeDtypeStruct((n, N_GROUPS * MLA_V), BF16),
        scratch_shapes=[pltpu.VMEM((N_GROUPS, MLA_KV_TILE, ATTN_TILE), F32)] * 2,
        compiler_params=_cparams("arbitrary", "arbitrary"),
        name="latent_attention",
    )(qd, kd, kd, vdt, vdt)


def _outproj_body(ya_ref, yb_ref, yc_ref, yd_ref, x_ref, m_ref, g2_ref, w_ref, rw_ref, rb_ref,
                  xo_ref, hx_ref, r_ref):
    tm = x_ref.shape[0]
    m = m_ref[0]
    y = _dot(ya_ref[...], w_ref[0:MIX_PART, :])
    for k, ref in enumerate((yb_ref, yc_ref, yd_ref), start=1):
        y = y + _dot(ref[...], w_ref[k * MIX_PART:(k + 1) * MIX_PART, :])
    x = x_ref[...] + m[2:3] * y
    xo_ref[...] = x
    h = _rms(x, g2_ref[...]) * (1.0 + m[4:5]) + m[3:4]

    scores = jax.nn.sigmoid(_dot_nt(rw_ref[...], h, precision=HIGHEST))
    biased = scores + rb_ref[...]
    E = EXPERTS_PER_GROUP
    bk = [biased[8 * k:8 * k + 8] for k in range(E)]
    sk = [scores[8 * k:8 * k + 8] for k in range(E)]
    gs = None
    for a in range(E):
        for b in range(a + 1, E):
            pair = bk[a] + bk[b]
            gs = pair if gs is None else jnp.maximum(gs, pair)
    best = gs[0:1]
    idx = jnp.zeros((1, tm), jnp.int32)
    for g in range(1, N_EXPERT_GROUPS):
        better = gs[g:g + 1] > best
        idx = jnp.where(better, g, idx)
        best = jnp.where(better, gs[g:g + 1], best)
    in_grp = lax.broadcasted_iota(jnp.int32, (8, tm), 0) == idx
    wk = []
    for k in range(E):
        rank = jnp.zeros((8, tm), jnp.int32)
        for j in range(E):
            if j != k:
                ahead = (bk[j] > bk[k]) | ((bk[j] == bk[k]) & (j < k))
                rank = rank + ahead.astype(jnp.int32)
        wk.append(jnp.where((rank < 2) & in_grp, sk[k], 0.0).sum(axis=0, keepdims=True))
    den = wk[0] + wk[1] + wk[2] + wk[3]
    ri = lax.broadcasted_iota(jnp.int32, (8, tm), 0)
    out = jnp.where(ri == E, idx.astype(F32), 0.0)
    for k in range(E):
        out = jnp.where(ri == k, wk[k] / den, out)
    r_ref[...] = out

    for s in range(FEAT_ROWS):
        hx_ref[:, s, :] = h[:, s * LANES:(s + 1) * LANES]
    hx_ref[:, FEAT_ROWS, :] = jnp.concatenate([out, jnp.zeros((LANES - 8, tm), F32)], axis=0).T
    for s in range(FEAT_ROWS + 1, SLAB_ROWS):
        hx_ref[:, s, :] = jnp.zeros((tm, LANES), F32)


def _out_projection(l, n_lat_tiles_per_batch, n_batch, ys, x, mod_r, g2, w, rw, rb):
    n, d = x.shape
    tm = TOKEN_TILE
    row = lambda i: (i, 0)
    const = lambda i: (0, 0)
    modrow = lambda i: (l * 8 + jnp.minimum(i // n_lat_tiles_per_batch, n_batch), 0, 0)
    in_specs = [pl.BlockSpec((tm, MIX_PART), row)] * 4 + [
        pl.BlockSpec((tm, d), row), pl.BlockSpec((1, N_MOD, d), modrow), pl.BlockSpec((1, d), const),
        pl.BlockSpec(w.shape, const), pl.BlockSpec(rw.shape, const), pl.BlockSpec(rb.shape, const)]
    return pl.pallas_call(
        _outproj_body,
        grid=(n // tm,),
        in_specs=in_specs,
        out_specs=[pl.BlockSpec((tm, d), row), pl.BlockSpec((tm, SLAB_ROWS, LANES), lambda i: (i, 0, 0)),
                   pl.BlockSpec((8, tm), lambda i: (0, i))],
        out_shape=[jax.ShapeDtypeStruct((n, d), F32), jax.ShapeDtypeStruct((n, SLAB_ROWS, LANES), F32),
                   jax.ShapeDtypeStruct((8, n), F32)],
        compiler_params=_cparams("arbitrary"),
        name="out_projection_routing",
    )(*ys, x, mod_r, g2, w, rw, rb)


def _moe_body(tg_ref, tv_ref, src_ref, dst_ref, hx_hbm, wg_ref, wu_ref, wd_ref, y_hbm,
              hbuf, ybuf, wgb, wub, wdb, gsem, ssem):
    i = pl.program_id(0)
    n_t = pl.num_programs(0)
    tm = hbuf.shape[1]
    slot = i % 2
    last = n_t - 1
    grp = tg_ref[i]
    prev_valid = tv_ref[jnp.maximum(i - 1, 0)] > 0

    def gather_copy(r, t, sl):
        return pltpu.make_async_copy(hx_hbm.at[src_ref[t * tm + r]], hbuf.at[sl, r], gsem.at[sl])

    def scatter_copy(r, t, sl):
        return pltpu.make_async_copy(ybuf.at[sl, r], y_hbm.at[dst_ref[t * tm + r]], ssem.at[sl])

    def start_all(copy, t, sl):
        def body(r, c):
            copy(r, t, sl).start()
            return c
        lax.fori_loop(0, tm, body, 0, unroll=8)

    def wait_gather(sl):
        pltpu.make_async_copy(hx_hbm.at[pl.ds(0, tm)], hbuf.at[sl], gsem.at[sl]).wait()

    def wait_scatter(sl):
        pltpu.make_async_copy(ybuf.at[sl], y_hbm.at[pl.ds(0, tm)], ssem.at[sl]).wait()

    @pl.when(i == 0)
    def _():
        start_all(gather_copy, 0, 0)

    @pl.when((i == 0) | (grp != tg_ref[jnp.maximum(i - 1, 0)]))
    def _():
        wgb[...] = wg_ref[0].astype(BF16)
        wub[...] = wu_ref[0].astype(BF16)
        wdb[...] = wd_ref[0].astype(BF16)

    @pl.when((i == 0) | prev_valid)
    def _():
        wait_gather(slot)

    @pl.when(i >= 2)
    def _():
        wait_scatter(slot)

    def expert_tile(with_scatter):
        nxt = jnp.minimum(i + 1, last)
        quarter = tm // 4
        starts = {0: [], 1: [], 2: [], 3: []}
        if with_scatter:
            starts[0] += [("s", r) for r in range(0, 2 * quarter)]
            starts[1] += [("s", r) for r in range(2 * quarter, tm)]
        starts[1] += [("g", r) for r in range(0, quarter // 2)]
        starts[2] += [("g", r) for r in range(quarter // 2, quarter // 2 + 2 * quarter)]
        starts[3] += [("g", r) for r in range(quarter // 2 + 2 * quarter, tm)]
        h = jnp.concatenate([hbuf[slot, :, s, :] for s in range(FEAT_ROWS)], axis=1).astype(BF16)
        gates = hbuf[slot, :, FEAT_ROWS, :]
        acc = jnp.zeros((tm, FEAT_ROWS * LANES), F32)
        for k in range(EXPERTS_PER_GROUP):
            for kind, r in starts[k]:
                if kind == "s":
                    scatter_copy(r, i - 1, 1 - slot).start()
                else:
                    gather_copy(r, nxt, 1 - slot).start()
            hid = _silu(_dot(h, wgb[k])) * _dot(h, wub[k]) * gates[:, k:k + 1]
            acc = acc + _dot(hid.astype(BF16), wdb[k])
        for s in range(FEAT_ROWS):
            ybuf[slot, :, s, :] = acc[:, s * LANES:(s + 1) * LANES]

    valid = tv_ref[i] > 0

    @pl.when(valid & (i == 0))
    def _():
        expert_tile(False)

    @pl.when(valid & (i > 0))
    def _():
        expert_tile(True)

    @pl.when(jnp.logical_not(valid))
    def _():
        ybuf[slot] = jnp.zeros(ybuf.shape[1:], F32)

        @pl.when(i > 0)
        def _():
            start_all(scatter_copy, i - 1, 1 - slot)

    @pl.when(i == last)
    def _():
        start_all(scatter_copy, i, slot)
        wait_scatter(slot)

        @pl.when(i > 0)
        def _():
            wait_scatter(1 - slot)

        @pl.when(valid)
        def _():
            wait_gather(1 - slot)


def _grouped_experts(l, tile_grp, tile_ok, src, dst, hx, w_gate, w_up, w_down):
    n_pad = src.shape[0]
    tm = MOE_TILE
    E = EXPERTS_PER_GROUP
    d = FEAT_ROWS * LANES
    wmap = lambda i, tg, tv, sr, ds: (l, tg[i], 0, 0)
    grid_spec = pltpu.PrefetchScalarGridSpec(
        num_scalar_prefetch=4,
        grid=(n_pad // tm,),
        in_specs=[pl.BlockSpec(memory_space=pl.ANY),
                  pl.BlockSpec((1, E, d, D_EXPERT), wmap),
                  pl.BlockSpec((1, E, d, D_EXPERT), wmap),
                  pl.BlockSpec((1, E, D_EXPERT, d), wmap)],
        out_specs=pl.BlockSpec(memory_space=pl.ANY),
        scratch_shapes=[pltpu.VMEM((2, tm, SLAB_ROWS, LANES), F32), pltpu.VMEM((2, tm, FEAT_ROWS, LANES), F32),
                        pltpu.VMEM((E, d, D_EXPERT), BF16), pltpu.VMEM((E, d, D_EXPERT), BF16),
                        pltpu.VMEM((E, D_EXPERT, d), BF16),
                        pltpu.SemaphoreType.DMA((2,)), pltpu.SemaphoreType.DMA((2,))])
    return pl.pallas_call(
        _moe_body,
        grid_spec=grid_spec,
        out_shape=jax.ShapeDtypeStruct((n_pad, FEAT_ROWS, LANES), F32),
        compiler_params=_cparams("arbitrary"),
        name="grouped_experts",
    )(tile_grp, tile_ok, src, dst, hx, w_gate, w_up, w_down)


def _dispatch_plan(route, n_pad):
    n = route.shape[1]
    tm = MOE_TILE
    grp = route[EXPERTS_PER_GROUP].astype(jnp.int32)
    onehot = (grp[:, None] == jnp.arange(N_EXPERT_GROUPS)[None, :]).astype(jnp.int32)
    csum = jnp.cumsum(onehot, axis=0)
    counts = csum[-1]
    rank = jnp.sum(csum * onehot, axis=1) - 1
    padded = ((counts + tm - 1) // tm) * tm
    ends = jnp.cumsum(padded)
    pos = jnp.sum((ends - padded)[None, :] * onehot, axis=1) + rank
    owner = jnp.zeros((n_pad,), jnp.int32).at[pos].set(jnp.arange(1, n + 1, dtype=jnp.int32))
    used = owner > 0
    src = jnp.maximum(owner - 1, 0)
    dst = jnp.where(used, owner - 1, n - 1 + jnp.cumsum(1 - used.astype(jnp.int32)))
    tile_start = jnp.arange(n_pad // tm, dtype=jnp.int32) * tm
    tile_grp = jnp.minimum(jnp.sum(tile_start[:, None] >= ends[None, :], axis=1), N_EXPERT_GROUPS - 1)
    tile_ok = (tile_start < ends[-1]).astype(jnp.int32)
    return tile_grp.astype(jnp.int32), tile_ok, src, dst


def _slab_rows(ref):
    return jnp.concatenate([ref[:, s, :] for s in range(FEAT_ROWS)], axis=1)


def _final_body(x_ref, y_ref, m_ref, g_ref, o_ref):
    x = x_ref[...] + m_ref[0, 5:6, :] * _slab_rows(y_ref)
    o_ref[...] = _rms(x, g_ref[...])


def _final_norm(l, n_lat, n_lat_tiles_per_batch, x, y, mod_r, g):
    d = x.shape[1]
    tm = TOKEN_TILE
    row = lambda i: (i, 0)
    return pl.pallas_call(
        _final_body,
        grid=(n_lat // tm,),
        in_specs=[pl.BlockSpec((tm, d), row), pl.BlockSpec((tm, FEAT_ROWS, LANES), lambda i: (i, 0, 0)),
                  pl.BlockSpec((1, N_MOD, d), lambda i: (l * 8 + i // n_lat_tiles_per_batch, 0, 0)),
                  pl.BlockSpec((1, d), lambda i: (0, 0))],
        out_specs=pl.BlockSpec((tm, d), row),
        out_shape=jax.ShapeDtypeStruct((n_lat, d), F32),
        compiler_params=_cparams("arbitrary"),
        name="final_norm",
    )(x, y, mod_r, g)


def _rope_tables(seq, n_batch, n_ctx_tokens):
    t = np.arange(seq)
    nf = MLA_ROPE // 4
    inv = jnp.asarray(ROPE_BASE, F32) ** (-jnp.arange(nf, dtype=F32) / nf)
    rang = jnp.asarray(t // GRID_W, F32)[:, None] * inv
    cang = jnp.asarray(t % GRID_W, F32)[:, None] * inv
    cr, sr, cc, sc = jnp.cos(rang), jnp.sin(rang), jnp.cos(cang), jnp.sin(cang)
    one = jnp.ones((seq, MLA_NOPE), F32)
    zero_tail = jnp.zeros((seq, MLA_SLOT - MLA_NOPE - MLA_ROPE), F32)
    cosp = jnp.concatenate([one, cr, cr, cc, cc, zero_tail], axis=1)
    sinp = jnp.concatenate([0 * one, -sr, sr, -sc, sc, zero_tail], axis=1)
    ctx_cos = jnp.concatenate([jnp.ones((n_ctx_tokens, MLA_NOPE + MLA_ROPE), F32),
                               jnp.zeros((n_ctx_tokens, MLA_SLOT - MLA_NOPE - MLA_ROPE), F32)], axis=1)
    cosp = jnp.concatenate([jnp.tile(cosp, (n_batch, 1)), ctx_cos], axis=0)
    sinp = jnp.concatenate([jnp.tile(sinp, (n_batch, 1)), jnp.zeros((n_ctx_tokens, MLA_SLOT), F32)], axis=0)
    return cosp, sinp


_ROPE_SWAP = np.concatenate([np.arange(8, 16), np.arange(0, 8), np.arange(24, 32), np.arange(16, 24)])


def _pad_last(a, before, after):
    return jnp.pad(a, [(0, 0)] * (a.ndim - 1) + [(before, after)])


def _layouts(w_in, w_out, pool_w, sgu_w, sgu_b, mla_w_uq, mla_w_ukv, router_w, router_b):
    L = w_in.shape[0]
    tail = MLA_SLOT - MLA_NOPE - MLA_ROPE
    kr = w_in[:, :, OFF_KR:]
    w = jnp.concatenate([w_in[:, :, :OFF_KR], _pad_last(kr, MLA_NOPE, tail),
                         _pad_last(kr[:, :, _ROPE_SWAP], MLA_NOPE, tail)], axis=-1).astype(BF16)
    uq = mla_w_uq.reshape(L, MLA_Q_RANK, N_GROUPS, MLA_NOPE + MLA_ROPE)
    q1 = _pad_last(uq, 0, tail).reshape(L, MLA_Q_RANK, N_GROUPS * MLA_SLOT)
    q2 = _pad_last(uq[..., MLA_NOPE:][..., _ROPE_SWAP], MLA_NOPE, tail).reshape(L, MLA_Q_RANK, N_GROUPS * MLA_SLOT)
    wq = jnp.concatenate([q1, q2], axis=-1).astype(BF16)
    ukv = mla_w_ukv.reshape(L, MLA_KV_RANK, N_GROUPS, MLA_NOPE + MLA_V)
    kn = _pad_last(ukv[..., :MLA_NOPE], 0, MLA_SLOT - MLA_NOPE).reshape(L, MLA_KV_RANK, N_GROUPS * MLA_SLOT)
    vv = _pad_last(ukv[..., MLA_NOPE:], 0, MLA_SLOT - MLA_V).reshape(L, MLA_KV_RANK, N_GROUPS * MLA_SLOT)
    wkv = kn.astype(BF16)
    wvt = jnp.swapaxes(vv, 1, 2).astype(BF16)
    eye = jnp.eye(N_GROUPS, dtype=F32)
    pool_bd = jnp.einsum('lgcd,gh->lgchd', pool_w, eye).reshape(L, MIX_PART, MIX_PART).astype(BF16)
    ones_bd = jnp.asarray(np.kron(np.eye(N_GROUPS), np.full((GROUP_DIM, GROUP_DIM), 1.0 / GROUP_DIM)), F32)
    sgu_all = sgu_w.reshape(L, N_GROUPS * SGU_CHUNK, SGU_CHUNK).astype(BF16)
    sgu_bias = jnp.repeat(jnp.swapaxes(sgu_b, 1, 2), GROUP_DIM, axis=2)
    rw = router_w.T.reshape(N_EXPERT_GROUPS, EXPERTS_PER_GROUP, -1).transpose(1, 0, 2)
    rw = jnp.pad(rw, ((0, 0), (0, 8 - N_EXPERT_GROUPS), (0, 0))).reshape(8 * EXPERTS_PER_GROUP, -1)
    rb = router_b.reshape(N_EXPERT_GROUPS, EXPERTS_PER_GROUP).T
    rb = jnp.pad(rb, ((0, 0), (0, 8 - N_EXPERT_GROUPS))).reshape(8 * EXPERTS_PER_GROUP, 1)
    return w, wq, wkv, wvt, w_out.astype(BF16), pool_bd, ones_bd, sgu_all, sgu_bias, rw, rb


def kernel(x, c, ctx, c_ctx, ada_w, ada_b, norm1_g, norm2_g, w_in, w_out, pool_w, pool_s, na_rpb,
           sgu_norm_g, sgu_w, sgu_b, mla_q_norm_g, mla_w_uq, mla_kv_norm_g, mla_w_ukv,
           router_w, router_b, moe_w_gate, moe_w_up, moe_w_down, final_g):
    B, S, D = x.shape
    CTX = ctx.shape[1]
    L = ada_w.shape[0]
    n_lat, n_ctx = B * S, B * CTX
    n_all = n_lat + n_ctx
    assert B < 8 and S % TOKEN_TILE == 0 and n_ctx % TOKEN_TILE == 0
    tiles_per_batch = S // TOKEN_TILE
    n_pad = n_all + N_EXPERT_GROUPS * MOE_TILE

    c_all = jnp.concatenate([c, c_ctx[None, :], jnp.zeros((8 - B - 1, D), F32)], axis=0)
    mod_r = _modulation(c_all, ada_w, ada_b).reshape(L * 8, N_MOD, D)
    w, wq, wkv, wvt, wo, pool_bd, ones_bd, sgu_all, sgu_bias, rw, rb = _layouts(
        w_in, w_out, pool_w, sgu_w, sgu_b, mla_w_uq, mla_w_ukv, router_w, router_b)
    bias_tab = _na_bias_tables(na_rpb, S // GRID_W)
    cosp, sinp = _rope_tables(S, B, n_ctx)

    xs = jnp.concatenate([x.reshape(n_lat, D), ctx.reshape(n_ctx, D)], axis=0)
    y = None
    for l in range(L):
        xs, pa, qb, kb, vb, pc, qd, kd, vd = _in_projection(
            l, tiles_per_batch, B, xs, y, mod_r, norm1_g[l][None], w[l], wq[l], mla_q_norm_g[l][None],
            wkv[l], wvt[l], mla_kv_norm_g[l][None], cosp, sinp)
        ya = _pool_mixer(n_lat, S, CTX, pa, pool_bd[l], pool_s[l][None])
        yb = _na_mixer(l, B, S, CTX, qb, kb, vb, bias_tab)
        yc = _sgu_mixer(pc, sgu_norm_g[l][None], ones_bd, sgu_all[l], sgu_bias[l])
        yd = _mla_mixer(B, S, CTX, qd, kd, vd)
        xs, hx, route = _out_projection(l, tiles_per_batch, B, (ya, yb, yc, yd), xs, mod_r, norm2_g[l][None],
                                        wo[l], rw, rb)
        tile_grp, tile_ok, src, dst = _dispatch_plan(route, n_pad)
        y = _grouped_experts(l, tile_grp, tile_ok, src, dst, hx, moe_w_gate, moe_w_up, moe_w_down)
    out = _final_norm(L - 1, n_lat, tiles_per_batch, xs, y, mod_r, final_g[None])
    return out.reshape(B, S, D)
```

```python
import functools

import numpy as np
import jax
import jax.numpy as jnp
from jax import lax
from jax.experimental import pallas as pl
from jax.experimental.pallas import tpu as pltpu

F32 = jnp.float32
BF16 = jnp.bfloat16
HIGHEST = lax.Precision.HIGHEST

GRID_W = 64
POOL_WINDOWS = (2, 4, 8, 16)
GROUP_DIM = 64
N_GROUPS = 4
MIX_PART = N_GROUPS * GROUP_DIM
NA_WIN_ROWS = 8
NA_WIN_COLS = 16
NA_Q_ROWS = 4
NA_K_ROWS = NA_Q_ROWS + NA_WIN_ROWS
SGU_CHUNK = 128
MLA_Q_RANK = 256
MLA_KV_RANK = 128
MLA_NOPE = 64
MLA_ROPE = 32
MLA_V = 64
MLA_SLOT = 128
MLA_SCALE = (MLA_NOPE + MLA_ROPE) ** -0.5
LOG2_E = float(np.log2(np.e))
ROPE_BASE = 10000.0
N_EXPERTS = 16
N_EXPERT_GROUPS = 4
EXPERTS_PER_GROUP = 4
D_EXPERT = 256
N_MOD = 6
EPS = 1e-6
NEG_INF = -1e30

OFF_B = MIX_PART
OFF_C = OFF_B + 3 * MIX_PART
OFF_D = OFF_C + 2 * MIX_PART
OFF_KR = OFF_D + MLA_Q_RANK + MLA_KV_RANK
W_IN_COLS = OFF_KR + 2 * MLA_SLOT

LANES = 128
FEAT_ROWS = 8
SLAB_ROWS = 16

TOKEN_TILE = 512
ATTN_TILE = 256
MLA_KV_TILE = 512
MOE_TILE = 512
VMEM_LIMIT = 56 * 1024 * 1024


def _cparams(*sem):
    return pltpu.CompilerParams(dimension_semantics=sem, vmem_limit_bytes=VMEM_LIMIT)


def _dot(a, b):
    return jnp.dot(a, b, preferred_element_type=F32)


def _dot_nt(a, b, precision=None):
    return lax.dot_general(a, b, (((1,), (1,)), ((), ())), precision=precision,
                           preferred_element_type=F32)


def _rms(x, g):
    return x * lax.rsqrt(jnp.mean(x * x, axis=-1, keepdims=True) + EPS) * g


def _silu(x):
    return x * jax.nn.sigmoid(x)


def _mod_body(c_ref, w_ref, b_ref, o_ref):
    o_ref[0] = jnp.dot(_silu(c_ref[...]), w_ref[0], precision=HIGHEST,
                       preferred_element_type=F32) + b_ref[0]


def _modulation(c_all, ada_w, ada_b):
    L, D, ND = ada_w.shape
    tn = 1536
    return pl.pallas_call(
        _mod_body,
        grid=(L, ND // tn),
        in_specs=[pl.BlockSpec((8, D), lambda l, j: (0, 0)),
                  pl.BlockSpec((1, D, tn), lambda l, j: (l, 0, j)),
                  pl.BlockSpec((1, 1, tn), lambda l, j: (l, 0, j))],
        out_specs=pl.BlockSpec((1, 8, tn), lambda l, j: (l, 0, j)),
        out_shape=jax.ShapeDtypeStruct((L, 8, ND), F32),
        compiler_params=_cparams("arbitrary", "arbitrary"),
        name="modulation",
    )(c_all, ada_w, ada_b.reshape(L, 1, ND))


def _inproj_body(has_res, *refs):
    if has_res:
        (x_ref, y_ref, mp_ref, m_ref, g1_ref, w_ref, wq_ref, gq_ref, wkv_ref, wvt_ref, gkv_ref, cos_ref, sin_ref,
         xo_ref, pa_ref, qb_ref, kb_ref, vb_ref, pc_ref, qd_ref, kd_ref, vd_ref) = refs
        x = x_ref[...] + mp_ref[0, 5:6, :] * _slab_rows(y_ref)
        xo_ref[...] = x
    else:
        (x_ref, m_ref, g1_ref, w_ref, wq_ref, gq_ref, wkv_ref, wvt_ref, gkv_ref, cos_ref, sin_ref,
         pa_ref, qb_ref, kb_ref, vb_ref, pc_ref, qd_ref, kd_ref, vd_ref) = refs
        x = x_ref[...]
    m = m_ref[0]
    hb = (_rms(x, g1_ref[...]) * (1.0 + m[1:2]) + m[0:1]).astype(BF16)

    def proj(a, b):
        return _dot(hb, w_ref[:, a:b])

    pa_ref[...] = proj(0, OFF_B)
    qb_ref[...] = (proj(OFF_B, OFF_B + MIX_PART) * (GROUP_DIM ** -0.5)).astype(BF16)
    kb_ref[...] = proj(OFF_B + MIX_PART, OFF_B + 2 * MIX_PART).astype(BF16)
    vb_ref[...] = proj(OFF_B + 2 * MIX_PART, OFF_C).astype(BF16)
    pc_ref[...] = proj(OFF_C, OFF_D)

    cosp = cos_ref[...]
    sinp = sin_ref[...]
    qn = _rms(proj(OFF_D, OFF_D + MLA_Q_RANK), gq_ref[...]).astype(BF16)
    qq = _dot(qn, wq_ref[...])
    half = N_GROUPS * MLA_SLOT
    for h in range(N_GROUPS):
        a = h * MLA_SLOT
        q = qq[:, a:a + MLA_SLOT] * cosp + qq[:, half + a:half + a + MLA_SLOT] * sinp
        qd_ref[:, a:a + MLA_SLOT] = (q * (MLA_SCALE * LOG2_E)).astype(BF16)
    kvn = _rms(proj(OFF_D + MLA_Q_RANK, OFF_KR), gkv_ref[...]).astype(BF16)
    kk = _dot(kvn, wkv_ref[...])
    kr = proj(OFF_KR, OFF_KR + MLA_SLOT) * cosp + proj(OFF_KR + MLA_SLOT, W_IN_COLS) * sinp
    for h in range(N_GROUPS):
        a = h * MLA_SLOT
        kd_ref[:, a:a + MLA_SLOT] = (kk[:, a:a + MLA_SLOT] + kr).astype(BF16)
    ones_row = (lax.broadcasted_iota(jnp.int32, (half, 1), 0) % MLA_SLOT == MLA_V).astype(F32)
    vd_ref[...] = (_dot_nt(wvt_ref[...], kvn) + ones_row).astype(BF16)


def _in_projection(l, n_lat_tiles_per_batch, n_batch, x, y, mod_r, g1, w, wq, gq, wkv, wvt, gkv, cosp, sinp):
    n, d = x.shape
    tm = TOKEN_TILE
    has_res = y is not None

    def row(i):
        return (i, 0)

    def modrow(layer):
        return lambda i: (layer * 8 + jnp.minimum(i // n_lat_tiles_per_batch, n_batch), 0, 0)

    def const2(i):
        return (0, 0)

    tok = lambda c: pl.BlockSpec((tm, c), row)
    mod_spec = lambda layer: pl.BlockSpec((1, N_MOD, d), modrow(layer))
    in_specs = [tok(d)]
    args = [x]
    if has_res:
        in_specs += [pl.BlockSpec((tm, FEAT_ROWS, LANES), lambda i: (i, 0, 0)), mod_spec(l - 1)]
        args += [y, mod_r]
    in_specs += [mod_spec(l), pl.BlockSpec((1, d), const2), pl.BlockSpec(w.shape, const2),
                 pl.BlockSpec(wq.shape, const2), pl.BlockSpec(gq.shape, const2),
                 pl.BlockSpec(wkv.shape, const2), pl.BlockSpec(wvt.shape, const2), pl.BlockSpec(gkv.shape, const2),
                 tok(MLA_SLOT), tok(MLA_SLOT)]
    args += [mod_r, g1, w, wq, gq, wkv, wvt, gkv, cosp, sinp]
    outs = [(MIX_PART, F32), (MIX_PART, BF16), (MIX_PART, BF16), (MIX_PART, BF16), (2 * MIX_PART, F32),
            (N_GROUPS * MLA_SLOT, BF16), (N_GROUPS * MLA_SLOT, BF16)]
    if has_res:
        outs = [(d, F32)] + outs
    slots = N_GROUPS * MLA_SLOT
    res = pl.pallas_call(
        functools.partial(_inproj_body, has_res),
        grid=(n // tm,),
        in_specs=in_specs,
        out_specs=[tok(c) for c, _ in outs] + [pl.BlockSpec((slots, tm), lambda i: (0, i))],
        out_shape=[jax.ShapeDtypeStruct((n, c), t) for c, t in outs] + [jax.ShapeDtypeStruct((slots, n), BF16)],
        compiler_params=_cparams("arbitrary"),
        name="in_projection",
    )(*args)
    if not has_res:
        res = [x] + list(res)
    return res


def _pool_body(n_lat, seq, ctx_len, prev_ref, cur_ref, next_ref, w_ref, s_ref, o_ref):
    tb = cur_ref.shape[0]
    ext = jnp.concatenate([prev_ref[...], cur_ref[...], next_ref[...]], axis=0)
    n = tb + 16
    g = pl.program_id(0) * tb - 8 + lax.broadcasted_iota(jnp.int32, (n, 1), 0)
    is_lat = g < n_lat
    length = jnp.where(is_lat, seq, ctx_len)
    p = jnp.where(is_lat, g & (seq - 1), (g - n_lat) & (ctx_len - 1))

    def shifted(a, j):
        r = pltpu.roll(a, (-j) % n, axis=0)
        ok = (p + j >= 0) & (p + j < length)
        return jnp.where(ok, r, 0.0)

    before1 = shifted(ext, -1)
    before2 = before1 + shifted(before1, -1)
    before4 = before2 + shifted(before2, -2)
    before8 = before4 + shifted(before4, -4)
    after2 = ext + shifted(ext, 1)
    after4 = after2 + shifted(after2, 2)
    after8 = after4 + shifted(after4, 4)
    lane_grp = lax.broadcasted_iota(jnp.int32, (1, MIX_PART), 1) // GROUP_DIM
    sl = slice(8, 8 + tb)
    tot = jnp.where(lane_grp == 0, (before1 + ext)[sl],
                    jnp.where(lane_grp == 1, (before2 + after2)[sl],
                              jnp.where(lane_grp == 2, (before4 + after4)[sl], (before8 + after8)[sl])))
    half = jnp.where(lane_grp == 0, POOL_WINDOWS[0] // 2,
                     jnp.where(lane_grp == 1, POOL_WINDOWS[1] // 2,
                               jnp.where(lane_grp == 2, POOL_WINDOWS[2] // 2, POOL_WINDOWS[3] // 2)))
    pc = p[sl]
    cnt = jnp.minimum(pc + half, length[sl]) - jnp.maximum(pc - half, 0)
    dlt = tot / cnt.astype(F32) - ext[sl]
    o_ref[...] = (_dot(dlt.astype(BF16), w_ref[...]) * s_ref[...]).astype(BF16)


def _pool_mixer(n_lat, seq, ctx_len, pa, w_bd, s):
    n = pa.shape[0]
    tb = 1024
    assert seq & (seq - 1) == 0 and ctx_len & (ctx_len - 1) == 0
    assert n_lat % tb == 0 and n % tb == 0
    nb8 = n // 8
    return pl.pallas_call(
        functools.partial(_pool_body, n_lat, seq, ctx_len),
        grid=(n // tb,),
        in_specs=[pl.BlockSpec((8, MIX_PART), lambda i: (jnp.maximum(i * (tb // 8) - 1, 0), 0)),
                  pl.BlockSpec((tb, MIX_PART), lambda i: (i, 0)),
                  pl.BlockSpec((8, MIX_PART), lambda i: (jnp.minimum((i + 1) * (tb // 8), nb8 - 1), 0)),
                  pl.BlockSpec(w_bd.shape, lambda i: (0, 0)),
                  pl.BlockSpec(s.shape, lambda i: (0, 0))],
        out_specs=pl.BlockSpec((tb, MIX_PART), lambda i: (i, 0)),
        out_shape=jax.ShapeDtypeStruct((n, MIX_PART), BF16),
        compiler_params=_cparams("arbitrary"),
        name="pool_mixer",
    )(pa, pa, pa, w_bd, s)


def _softmax_pv(parts):
    m = parts[0][0].max(axis=-1, keepdims=True)
    for s, _ in parts[1:]:
        m = jnp.maximum(m, s.max(axis=-1, keepdims=True))
    den = 0.0
    out = 0.0
    for s, v in parts:
        p = jnp.exp(s - m)
        den = den + p.sum(axis=-1, keepdims=True)
        out = out + _dot(p.astype(BF16), v)
    return out / den


def _na_body(rows, n_qt, q_ref, kl_ref, kc_ref, vl_ref, vc_ref, bias_ref, o_ref):
    qt = pl.program_id(1)
    q = q_ref[...]
    kc = kc_ref[...]
    vc = vc_ref[...]
    lane_grp = lax.broadcasted_iota(jnp.int32, (1, MIX_PART), 1) // GROUP_DIM
    zero = jnp.zeros_like(q)

    @pl.when(qt < n_qt)
    def _():
        k0 = jnp.clip(NA_Q_ROWS * qt - NA_WIN_ROWS // 2, 0, rows - NA_K_ROWS)
        off = pl.multiple_of(k0 * GRID_W, NA_Q_ROWS * GRID_W)
        kw = kl_ref[pl.ds(off, NA_K_ROWS * GRID_W), :]
        vw = vl_ref[pl.ds(off, NA_K_ROWS * GRID_W), :]
        o = jnp.zeros(q.shape, F32)
        for h in range(N_GROUPS):
            qh = jnp.where(lane_grp == h, q, zero)
            oh = _softmax_pv([(_dot_nt(qh, kw) + bias_ref[0, h], vw), (_dot_nt(qh, kc), vc)])
            o = jnp.where(lane_grp == h, oh, o)
        o_ref[...] = o.astype(BF16)

    @pl.when(qt == n_qt)
    def _():
        o = jnp.zeros(q.shape, F32)
        for h in range(N_GROUPS):
            qh = jnp.where(lane_grp == h, q, zero)
            oh = _softmax_pv([(_dot_nt(qh, kc), vc)])
            o = jnp.where(lane_grp == h, oh, o)
        o_ref[...] = o.astype(BF16)


def _attn_specs(n_batch, seq, ctx_len, width_q, width_k, width_v, width_o):
    n_qt = seq // ATTN_TILE
    lat_blocks = n_batch * n_qt
    assert ctx_len == ATTN_TILE

    def qmap(b, t):
        return (jnp.where(t < n_qt, b * n_qt + t, lat_blocks + b), 0)

    q_spec = pl.BlockSpec((ATTN_TILE, width_q), qmap)
    kl_spec = pl.BlockSpec((seq, width_k), lambda b, t: (b, 0))
    kc_spec = pl.BlockSpec((ctx_len, width_k), lambda b, t: (lat_blocks + b, 0))
    vl_spec = pl.BlockSpec((seq, width_v), lambda b, t: (b, 0))
    vc_spec = pl.BlockSpec((ctx_len, width_v), lambda b, t: (lat_blocks + b, 0))
    o_spec = pl.BlockSpec((ATTN_TILE, width_o), qmap)
    return n_qt, [q_spec, kl_spec, kc_spec, vl_spec, vc_spec], o_spec


def _na_mixer(l, n_batch, seq, ctx_len, qb, kb, vb, bias_tab):
    n = qb.shape[0]
    rows = seq // GRID_W
    n_qt, in_specs, o_spec = _attn_specs(n_batch, seq, ctx_len, MIX_PART, MIX_PART, MIX_PART, MIX_PART)
    assert ATTN_TILE == NA_Q_ROWS * GRID_W and rows >= NA_K_ROWS + NA_Q_ROWS

    def bias_map(b, t):
        kind = jnp.where(t == 0, 0, jnp.where(t >= n_qt - 1, 2, 1))
        return (l * 3 + kind, 0, 0, 0)

    in_specs.append(pl.BlockSpec((1,) + bias_tab.shape[1:], bias_map))
    return pl.pallas_call(
        functools.partial(_na_body, rows, n_qt),
        grid=(n_batch, n_qt + 1),
        in_specs=in_specs,
        out_specs=o_spec,
        out_shape=jax.ShapeDtypeStruct((n, MIX_PART), BF16),
        compiler_params=_cparams("arbitrary", "arbitrary"),
        name="neighborhood_attention",
    )(qb, kb, kb, vb, vb, bias_tab)


def _na_bias_tables(na_rpb, rows):
    L, H = na_rpb.shape[:2]
    kinds = ((0, 0), (NA_Q_ROWS, 0), (rows - NA_Q_ROWS, rows - NA_K_ROWS))
    qc = np.arange(GRID_W)
    kc = np.arange(GRID_W)
    wsc = np.clip(qc - NA_WIN_COLS // 2, 0, GRID_W - NA_WIN_COLS)
    col_ok = (kc[None, :] >= wsc[:, None]) & (kc[None, :] < wsc[:, None] + NA_WIN_COLS)
    padded = _pad_last(na_rpb, GRID_W - NA_WIN_COLS, GRID_W - NA_WIN_COLS)
    shifted = jnp.stack([padded[..., GRID_W - 1 - c:2 * GRID_W - 1 - c] for c in range(GRID_W)], axis=2)
    slabs = jnp.where(col_ok[:, None, :], shifted, NEG_INF)

    def masked(n):
        return jnp.full((L, H, GRID_W, n, GRID_W), NEG_INF, F32)

    tabs = []
    for r0, k0 in kinds:
        per_row = []
        for qr in range(NA_Q_ROWS):
            r = r0 + qr
            first = int(np.clip(r - NA_WIN_ROWS // 2, 0, rows - NA_WIN_ROWS)) - k0
            d0 = k0 + first - r + NA_WIN_ROWS - 1
            assert 0 <= first <= NA_K_ROWS - NA_WIN_ROWS and 0 <= d0 <= NA_WIN_ROWS - 1
            per_row.append(jnp.concatenate(
                [masked(first), slabs[:, :, :, d0:d0 + NA_WIN_ROWS, :], masked(NA_K_ROWS - NA_WIN_ROWS - first)],
                axis=3))
        tabs.append(jnp.stack(per_row, axis=2))
    return jnp.stack(tabs, axis=1).reshape(L * 3, H, NA_Q_ROWS * GRID_W, NA_K_ROWS * GRID_W)


def _gelu_tanh(x):
    return 0.5 * x * (1.0 + jnp.tanh(np.sqrt(2.0 / np.pi).astype(np.float32) * (x + 0.044715 * (x * x * x))))


def _sgu_body(pc_ref, gn_ref, ones_ref, w_ref, b_ref, o_ref):
    tm = pc_ref.shape[0]
    uv = _gelu_tanh(pc_ref[...])
    u = uv[:, :MIX_PART]
    v = uv[:, MIX_PART:]
    ms = jnp.dot(v * v, ones_ref[...], precision=HIGHEST, preferred_element_type=F32)
    vg = (v * lax.rsqrt(ms + EPS) * gn_ref[...]).astype(BF16)
    lane_grp = lax.broadcasted_iota(jnp.int32, (1, MIX_PART), 1) // GROUP_DIM
    w = w_ref[...]
    for c in range(tm // SGU_CHUNK):
        rs = slice(c * SGU_CHUNK, (c + 1) * SGU_CHUNK)
        r = _dot(w, vg[rs])
        mixed = r[:SGU_CHUNK]
        for g in range(1, N_GROUPS):
            mixed = jnp.where(lane_grp == g, r[g * SGU_CHUNK:(g + 1) * SGU_CHUNK], mixed)
        o_ref[rs, :] = (u[rs] * (mixed + b_ref[...])).astype(BF16)


def _sgu_mixer(pc, gn, ones_bd, w_all, b_exp):
    n = pc.shape[0]
    tm = TOKEN_TILE
    const = lambda i: (0, 0)
    return pl.pallas_call(
        _sgu_body,
        grid=(n // tm,),
        in_specs=[pl.BlockSpec((tm, 2 * MIX_PART), lambda i: (i, 0)),
                  pl.BlockSpec(gn.shape, const), pl.BlockSpec(ones_bd.shape, const),
                  pl.BlockSpec(w_all.shape, const), pl.BlockSpec(b_exp.shape, const)],
        out_specs=pl.BlockSpec((tm, MIX_PART), lambda i: (i, 0)),
        out_shape=jax.ShapeDtypeStruct((n, MIX_PART), BF16),
        compiler_params=_cparams("arbitrary"),
        name="spatial_gating",
    )(pc, gn, ones_bd, w_all, b_exp)


def _mla_body(n_qt, n_kv, q_ref, kl_ref, kc_ref, vl_ref, vc_ref, o_ref, sa_ref, sb_ref):
    qt = pl.program_id(1)
    n_lat = jnp.where(qt < n_qt, n_kv, 0)
    heads = [slice(h * MLA_SLOT, (h + 1) * MLA_SLOT) for h in range(N_GROUPS)]

    def scores(t, h, dst):
        off = pl.multiple_of(t * MLA_KV_TILE, MLA_KV_TILE)
        dst[h] = _dot_nt(kl_ref[pl.ds(off, MLA_KV_TILE), heads[h]], q_ref[:, heads[h]])

    def consume(t, h, src, m, acc):
        off = pl.multiple_of(t * MLA_KV_TILE, MLA_KV_TILE)
        s = src[h]
        mn = jnp.maximum(m, s.max(axis=0, keepdims=True))
        p = jnp.exp2(s - mn).astype(BF16)
        return mn, jnp.exp2(m - mn) * acc + _dot(vl_ref[heads[h], pl.ds(off, MLA_KV_TILE)], p)

    ctx_scores = [_dot_nt(kc_ref[:, hs], q_ref[:, hs]) for hs in heads]
    for h in range(N_GROUPS):
        scores(0, h, sa_ref)
    state = []
    for hs, s in zip(heads, ctx_scores):
        m = s.max(axis=0, keepdims=True)
        state.append((m, _dot(vc_ref[hs, :], jnp.exp2(s - m).astype(BF16))))

    def step(i, carry):
        t = 2 * i
        mid = []
        for h, (m, acc) in enumerate(carry):
            scores(t + 1, h, sb_ref)
            mid.append(consume(t, h, sa_ref, m, acc))
        out = []
        for h, (m, acc) in enumerate(mid):
            scores(jnp.minimum(t + 2, n_kv - 1), h, sa_ref)
            out.append(consume(t + 1, h, sb_ref, m, acc))
        return tuple(out)

    state = lax.fori_loop(0, n_lat // 2, step, tuple(state))
    out_t = jnp.concatenate([acc[:MLA_V] / acc[MLA_V:MLA_V + 1] for _, acc in state], axis=0)
    o_ref[...] = out_t.T.astype(BF16)


def _mla_mixer(n_batch, seq, ctx_len, qd, kd, vdt):
    n = qd.shape[0]
    slots = N_GROUPS * MLA_SLOT
    n_qt, in_specs, o_spec = _attn_specs(n_batch, seq, ctx_len, slots, slots, slots, N_GROUPS * MLA_V)
    lat_blocks = n_batch * n_qt
    in_specs[3] = pl.BlockSpec((slots, seq), lambda b, t: (0, b))
    in_specs[4] = pl.BlockSpec((slots, ctx_len), lambda b, t: (0, lat_blocks + b))
    assert seq % MLA_KV_TILE == 0
    return pl.pallas_call(
        functools.partial(_mla_body, n_qt, seq // MLA_KV_TILE),
        grid=(n_batch, n_qt + 1),
        in_specs=in_specs,
        out_specs=o_spec,
        out_shape=jax.ShapeDtypeStruct((n, N_GROUPS * MLA_V), BF16),
        scratch_shapes=[pltpu.VMEM((N_GROUPS, MLA_KV_TILE, ATTN_TILE), F32)] * 2,
        compiler_params=_cparams("arbitrary", "arbitrary"),
        name="latent_attention",
    )(qd, kd, kd, vdt, vdt)


def _outproj_body(ya_ref, yb_ref, yc_ref, yd_ref, x_ref, m_ref, g2_ref, w_ref, rw_ref, rb_ref,
                  xo_ref, hx_ref, r_ref):
    tm = x_ref.shape[0]
    m = m_ref[0]
    y = _dot(ya_ref[...], w_ref[0:MIX_PART, :])
    for k, ref in enumerate((yb_ref, yc_ref, yd_ref), start=1):
        y = y + _dot(ref[...], w_ref[k * MIX_PART:(k + 1) * MIX_PART, :])
    x = x_ref[...] + m[2:3] * y
    xo_ref[...] = x
    h = _rms(x, g2_ref[...]) * (1.0 + m[4:5]) + m[3:4]

    scores = jax.nn.sigmoid(_dot_nt(rw_ref[...], h, precision=HIGHEST))
    biased = scores + rb_ref[...]
    E = EXPERTS_PER_GROUP
    bk = [biased[8 * k:8 * k + 8] for k in range(E)]
    sk = [scores[8 * k:8 * k + 8] for k in range(E)]
    gs = None
    for a in range(E):
        for b in range(a + 1, E):
            pair = bk[a] + bk[b]
            gs = pair if gs is None else jnp.maximum(gs, pair)
    best = gs[0:1]
    idx = jnp.zeros((1, tm), jnp.int32)
    for g in range(1, N_EXPERT_GROUPS):
        better = gs[g:g + 1] > best
        idx = jnp.where(better, g, idx)
        best = jnp.where(better, gs[g:g + 1], best)
    in_grp = lax.broadcasted_iota(jnp.int32, (8, tm), 0) == idx
    wk = []
    for k in range(E):
        rank = jnp.zeros((8, tm), jnp.int32)
        for j in range(E):
            if j != k:
                ahead = (bk[j] > bk[k]) | ((bk[j] == bk[k]) & (j < k))
                rank = rank + ahead.astype(jnp.int32)
        wk.append(jnp.where((rank < 2) & in_grp, sk[k], 0.0).sum(axis=0, keepdims=True))
    den = wk[0] + wk[1] + wk[2] + wk[3]
    ri = lax.broadcasted_iota(jnp.int32, (8, tm), 0)
    out = jnp.where(ri == E, idx.astype(F32), 0.0)
    for k in range(E):
        out = jnp.where(ri == k, wk[k] / den, out)
    r_ref[...] = out

    for s in range(FEAT_ROWS):
        hx_ref[:, s, :] = h[:, s * LANES:(s + 1) * LANES]
    hx_ref[:, FEAT_ROWS, :] = jnp.concatenate([out, jnp.zeros((LANES - 8, tm), F32)], axis=0).T
    for s in range(FEAT_ROWS + 1, SLAB_ROWS):
        hx_ref[:, s, :] = jnp.zeros((tm, LANES), F32)


def _out_projection(l, n_lat_tiles_per_batch, n_batch, ys, x, mod_r, g2, w, rw, rb):
    n, d = x.shape
    tm = TOKEN_TILE
    row = lambda i: (i, 0)
    const = lambda i: (0, 0)
    modrow = lambda i: (l * 8 + jnp.minimum(i // n_lat_tiles_per_batch, n_batch), 0, 0)
    in_specs = [pl.BlockSpec((tm, MIX_PART), row)] * 4 + [
        pl.BlockSpec((tm, d), row), pl.BlockSpec((1, N_MOD, d), modrow), pl.BlockSpec((1, d), const),
        pl.BlockSpec(w.shape, const), pl.BlockSpec(rw.shape, const), pl.BlockSpec(rb.shape, const)]
    return pl.pallas_call(
        _outproj_body,
        grid=(n // tm,),
        in_specs=in_specs,
        out_specs=[pl.BlockSpec((tm, d), row), pl.BlockSpec((tm, SLAB_ROWS, LANES), lambda i: (i, 0, 0)),
                   pl.BlockSpec((8, tm), lambda i: (0, i))],
        out_shape=[jax.ShapeDtypeStruct((n, d), F32), jax.ShapeDtypeStruct((n, SLAB_ROWS, LANES), F32),
                   jax.ShapeDtypeStruct((8, n), F32)],
        compiler_params=_cparams("arbitrary"),
        name="out_projection_routing",
    )(*ys, x, mod_r, g2, w, rw, rb)


def _moe_body(tg_ref, tv_ref, src_ref, dst_ref, hx_hbm, wg_ref, wu_ref, wd_ref, y_hbm,
              hbuf, ybuf, wgb, wub, wdb, gsem, ssem):
    i = pl.program_id(0)
    n_t = pl.num_programs(0)
    tm = hbuf.shape[1]
    slot = i % 2
    last = n_t - 1
    grp = tg_ref[i]
    prev_valid = tv_ref[jnp.maximum(i - 1, 0)] > 0

    def gather_copy(r, t, sl):
        return pltpu.make_async_copy(hx_hbm.at[src_ref[t * tm + r]], hbuf.at[sl, r], gsem.at[sl])

    def scatter_copy(r, t, sl):
        return pltpu.make_async_copy(ybuf.at[sl, r], y_hbm.at[dst_ref[t * tm + r]], ssem.at[sl])

    def start_all(copy, t, sl):
        def body(j, c):
            copy(2 * j, t, sl).start(priority=0)
            copy(2 * j + 1, t, sl).start(priority=1)
            return c
        lax.fori_loop(0, tm // 2, body, 0, unroll=4)

    def wait_gather(sl):
        pltpu.make_async_copy(hx_hbm.at[pl.ds(0, tm)], hbuf.at[sl], gsem.at[sl]).wait()

    def wait_scatter(sl):
        pltpu.make_async_copy(ybuf.at[sl], y_hbm.at[pl.ds(0, tm)], ssem.at[sl]).wait()

    @pl.when(i == 0)
    def _():
        start_all(gather_copy, 0, 0)

    @pl.when((i == 0) | (grp != tg_ref[jnp.maximum(i - 1, 0)]))
    def _():
        wgb[...] = wg_ref[0].astype(BF16)
        wub[...] = wu_ref[0].astype(BF16)
        wdb[...] = wd_ref[0].astype(BF16)

    @pl.when((i == 0) | prev_valid)
    def _():
        wait_gather(slot)

    @pl.when(i >= 2)
    def _():
        wait_scatter(slot)

    def expert_tile(with_scatter):
        nxt = jnp.minimum(i + 1, last)
        quarter = tm // 4
        starts = {0: [], 1: [], 2: [], 3: []}
        if with_scatter:
            starts[0] += [("s", r) for r in range(0, 2 * quarter)]
            starts[1] += [("s", r) for r in range(2 * quarter, tm)]
        starts[1] += [("g", r) for r in range(0, quarter // 2)]
        starts[2] += [("g", r) for r in range(quarter // 2, quarter // 2 + 2 * quarter)]
        starts[3] += [("g", r) for r in range(quarter // 2 + 2 * quarter, tm)]
        h = jnp.concatenate([hbuf[slot, :, s, :] for s in range(FEAT_ROWS)], axis=1).astype(BF16)
        gates = hbuf[slot, :, FEAT_ROWS, :]
        acc = jnp.zeros((tm, FEAT_ROWS * LANES), F32)
        for k in range(EXPERTS_PER_GROUP):
            for kind, r in starts[k]:
                if kind == "s":
                    scatter_copy(r, i - 1, 1 - slot).start(priority=r % 2)
                else:
                    gather_copy(r, nxt, 1 - slot).start(priority=r % 2)
            hid = _silu(_dot(h, wgb[k])) * _dot(h, wub[k]) * gates[:, k:k + 1]
            acc = acc + _dot(hid.astype(BF16), wdb[k])
        for s in range(FEAT_ROWS):
            ybuf[slot, :, s, :] = acc[:, s * LANES:(s + 1) * LANES]

    valid = tv_ref[i] > 0

    @pl.when(valid & (i == 0))
    def _():
        expert_tile(False)

    @pl.when(valid & (i > 0))
    def _():
        expert_tile(True)

    @pl.when(jnp.logical_not(valid))
    def _():
        ybuf[slot] = jnp.zeros(ybuf.shape[1:], F32)

        @pl.when(i > 0)
        def _():
            start_all(scatter_copy, i - 1, 1 - slot)

    @pl.when(i == last)
    def _():
        start_all(scatter_copy, i, slot)
        wait_scatter(slot)

        @pl.when(i > 0)
        def _():
            wait_scatter(1 - slot)

        @pl.when(valid)
        def _():
            wait_gather(1 - slot)


def _grouped_experts(l, tile_grp, tile_ok, src, dst, hx, w_gate, w_up, w_down):
    n_pad = src.shape[0]
    tm = MOE_TILE
    E = EXPERTS_PER_GROUP
    d = FEAT_ROWS * LANES
    wmap = lambda i, tg, tv, sr, ds: (l, tg[i], 0, 0)
    grid_spec = pltpu.PrefetchScalarGridSpec(
        num_scalar_prefetch=4,
        grid=(n_pad // tm,),
        in_specs=[pl.BlockSpec(memory_space=pl.ANY),
                  pl.BlockSpec((1, E, d, D_EXPERT), wmap),
                  pl.BlockSpec((1, E, d, D_EXPERT), wmap),
                  pl.BlockSpec((1, E, D_EXPERT, d), wmap)],
        out_specs=pl.BlockSpec(memory_space=pl.ANY),
        scratch_shapes=[pltpu.VMEM((2, tm, SLAB_ROWS, LANES), F32), pltpu.VMEM((2, tm, FEAT_ROWS, LANES), F32),
                        pltpu.VMEM((E, d, D_EXPERT), BF16), pltpu.VMEM((E, d, D_EXPERT), BF16),
                        pltpu.VMEM((E, D_EXPERT, d), BF16),
                        pltpu.SemaphoreType.DMA((2,)), pltpu.SemaphoreType.DMA((2,))])
    return pl.pallas_call(
        _moe_body,
        grid_spec=grid_spec,
        out_shape=jax.ShapeDtypeStruct((n_pad, FEAT_ROWS, LANES), F32),
        compiler_params=_cparams("arbitrary"),
        name="grouped_experts",
    )(tile_grp, tile_ok, src, dst, hx, w_gate, w_up, w_down)


def _dispatch_plan(route, n_pad):
    n = route.shape[1]
    tm = MOE_TILE
    grp = route[EXPERTS_PER_GROUP].astype(jnp.int32)
    onehot = (grp[:, None] == jnp.arange(N_EXPERT_GROUPS)[None, :]).astype(jnp.int32)
    csum = jnp.cumsum(onehot, axis=0)
    counts = csum[-1]
    rank = jnp.sum(csum * onehot, axis=1) - 1
    padded = ((counts + tm - 1) // tm) * tm
    ends = jnp.cumsum(padded)
    pos = jnp.sum((ends - padded)[None, :] * onehot, axis=1) + rank
    owner = jnp.zeros((n_pad,), jnp.int32).at[pos].set(jnp.arange(1, n + 1, dtype=jnp.int32))
    used = owner > 0
    src = jnp.maximum(owner - 1, 0)
    dst = jnp.where(used, owner - 1, n - 1 + jnp.cumsum(1 - used.astype(jnp.int32)))
    tile_start = jnp.arange(n_pad // tm, dtype=jnp.int32) * tm
    tile_grp = jnp.minimum(jnp.sum(tile_start[:, None] >= ends[None, :], axis=1), N_EXPERT_GROUPS - 1)
    tile_ok = (tile_start < ends[-1]).astype(jnp.int32)
    return tile_grp.astype(jnp.int32), tile_ok, src, dst


def _slab_rows(ref):
    return jnp.concatenate([ref[:, s, :] for s in range(FEAT_ROWS)], axis=1)


def _final_body(x_ref, y_ref, m_ref, g_ref, o_ref):
    x = x_ref[...] + m_ref[0, 5:6, :] * _slab_rows(y_ref)
    o_ref[...] = _rms(x, g_ref[...])


def _final_norm(l, n_lat, n_lat_tiles_per_batch, x, y, mod_r, g):
    d = x.shape[1]
    tm = TOKEN_TILE
    row = lambda i: (i, 0)
    return pl.pallas_call(
        _final_body,
        grid=(n_lat // tm,),
        in_specs=[pl.BlockSpec((tm, d), row), pl.BlockSpec((tm, FEAT_ROWS, LANES), lambda i: (i, 0, 0)),
                  pl.BlockSpec((1, N_MOD, d), lambda i: (l * 8 + i // n_lat_tiles_per_batch, 0, 0)),
                  pl.BlockSpec((1, d), lambda i: (0, 0))],
        out_specs=pl.BlockSpec((tm, d), row),
        out_shape=jax.ShapeDtypeStruct((n_lat, d), F32),
        compiler_params=_cparams("arbitrary"),
        name="final_norm",
    )(x, y, mod_r, g)


def _rope_tables(seq, n_batch, n_ctx_tokens):
    t = np.arange(seq)
    nf = MLA_ROPE // 4
    inv = jnp.asarray(ROPE_BASE, F32) ** (-jnp.arange(nf, dtype=F32) / nf)
    rang = jnp.asarray(t // GRID_W, F32)[:, None] * inv
    cang = jnp.asarray(t % GRID_W, F32)[:, None] * inv
    cr, sr, cc, sc = jnp.cos(rang), jnp.sin(rang), jnp.cos(cang), jnp.sin(cang)
    one = jnp.ones((seq, MLA_NOPE), F32)
    zero_tail = jnp.zeros((seq, MLA_SLOT - MLA_NOPE - MLA_ROPE), F32)
    cosp = jnp.concatenate([one, cr, cr, cc, cc, zero_tail], axis=1)
    sinp = jnp.concatenate([0 * one, -sr, sr, -sc, sc, zero_tail], axis=1)
    ctx_cos = jnp.concatenate([jnp.ones((n_ctx_tokens, MLA_NOPE + MLA_ROPE), F32),
                               jnp.zeros((n_ctx_tokens, MLA_SLOT - MLA_NOPE - MLA_ROPE), F32)], axis=1)
    cosp = jnp.concatenate([jnp.tile(cosp, (n_batch, 1)), ctx_cos], axis=0)
    sinp = jnp.concatenate([jnp.tile(sinp, (n_batch, 1)), jnp.zeros((n_ctx_tokens, MLA_SLOT), F32)], axis=0)
    return cosp, sinp


_ROPE_SWAP = np.concatenate([np.arange(8, 16), np.arange(0, 8), np.arange(24, 32), np.arange(16, 24)])


def _pad_last(a, before, after):
    return jnp.pad(a, [(0, 0)] * (a.ndim - 1) + [(before, after)])


def _layouts(w_in, w_out, pool_w, sgu_w, sgu_b, mla_w_uq, mla_w_ukv, router_w, router_b):
    L = w_in.shape[0]
    tail = MLA_SLOT - MLA_NOPE - MLA_ROPE
    kr = w_in[:, :, OFF_KR:]
    w = jnp.concatenate([w_in[:, :, :OFF_KR], _pad_last(kr, MLA_NOPE, tail),
                         _pad_last(kr[:, :, _ROPE_SWAP], MLA_NOPE, tail)], axis=-1).astype(BF16)
    uq = mla_w_uq.reshape(L, MLA_Q_RANK, N_GROUPS, MLA_NOPE + MLA_ROPE)
    q1 = _pad_last(uq, 0, tail).reshape(L, MLA_Q_RANK, N_GROUPS * MLA_SLOT)
    q2 = _pad_last(uq[..., MLA_NOPE:][..., _ROPE_SWAP], MLA_NOPE, tail).reshape(L, MLA_Q_RANK, N_GROUPS * MLA_SLOT)
    wq = jnp.concatenate([q1, q2], axis=-1).astype(BF16)
    ukv = mla_w_ukv.reshape(L, MLA_KV_RANK, N_GROUPS, MLA_NOPE + MLA_V)
    kn = _pad_last(ukv[..., :MLA_NOPE], 0, MLA_SLOT - MLA_NOPE).reshape(L, MLA_KV_RANK, N_GROUPS * MLA_SLOT)
    vv = _pad_last(ukv[..., MLA_NOPE:], 0, MLA_SLOT - MLA_V).reshape(L, MLA_KV_RANK, N_GROUPS * MLA_SLOT)
    wkv = kn.astype(BF16)
    wvt = jnp.swapaxes(vv, 1, 2).astype(BF16)
    eye = jnp.eye(N_GROUPS, dtype=F32)
    pool_bd = jnp.einsum('lgcd,gh->lgchd', pool_w, eye).reshape(L, MIX_PART, MIX_PART).astype(BF16)
    ones_bd = jnp.asarray(np.kron(np.eye(N_GROUPS), np.full((GROUP_DIM, GROUP_DIM), 1.0 / GROUP_DIM)), F32)
    sgu_all = sgu_w.reshape(L, N_GROUPS * SGU_CHUNK, SGU_CHUNK).astype(BF16)
    sgu_bias = jnp.repeat(jnp.swapaxes(sgu_b, 1, 2), GROUP_DIM, axis=2)
    rw = router_w.T.reshape(N_EXPERT_GROUPS, EXPERTS_PER_GROUP, -1).transpose(1, 0, 2)
    rw = jnp.pad(rw, ((0, 0), (0, 8 - N_EXPERT_GROUPS), (0, 0))).reshape(8 * EXPERTS_PER_GROUP, -1)
    rb = router_b.reshape(N_EXPERT_GROUPS, EXPERTS_PER_GROUP).T
    rb = jnp.pad(rb, ((0, 0), (0, 8 - N_EXPERT_GROUPS))).reshape(8 * EXPERTS_PER_GROUP, 1)
    return w, wq, wkv, wvt, w_out.astype(BF16), pool_bd, ones_bd, sgu_all, sgu_bias, rw, rb


def kernel(x, c, ctx, c_ctx, ada_w, ada_b, norm1_g, norm2_g, w_in, w_out, pool_w, pool_s, na_rpb,
           sgu_norm_g, sgu_w, sgu_b, mla_q_norm_g, mla_w_uq, mla_kv_norm_g, mla_w_ukv,
           router_w, router_b, moe_w_gate, moe_w_up, moe_w_down, final_g):
    B, S, D = x.shape
    CTX = ctx.shape[1]
    L = ada_w.shape[0]
    n_lat, n_ctx = B * S, B * CTX
    n_all = n_lat + n_ctx
    assert B < 8 and S % TOKEN_TILE == 0 and n_ctx % TOKEN_TILE == 0
    tiles_per_batch = S // TOKEN_TILE
    n_pad = n_all + N_EXPERT_GROUPS * MOE_TILE

    c_all = jnp.concatenate([c, c_ctx[None, :], jnp.zeros((8 - B - 1, D), F32)], axis=0)
    mod_r = _modulation(c_all, ada_w, ada_b).reshape(L * 8, N_MOD, D)
    w, wq, wkv, wvt, wo, pool_bd, ones_bd, sgu_all, sgu_bias, rw, rb = _layouts(
        w_in, w_out, pool_w, sgu_w, sgu_b, mla_w_uq, mla_w_ukv, router_w, router_b)
    bias_tab = _na_bias_tables(na_rpb, S // GRID_W)
    cosp, sinp = _rope_tables(S, B, n_ctx)

    xs = jnp.concatenate([x.reshape(n_lat, D), ctx.reshape(n_ctx, D)], axis=0)
    y = None
    for l in range(L):
        xs, pa, qb, kb, vb, pc, qd, kd, vd = _in_projection(
            l, tiles_per_batch, B, xs, y, mod_r, norm1_g[l][None], w[l], wq[l], mla_q_norm_g[l][None],
            wkv[l], wvt[l], mla_kv_norm_g[l][None], cosp, sinp)
        ya = _pool_mixer(n_lat, S, CTX, pa, pool_bd[l], pool_s[l][None])
        yb = _na_mixer(l, B, S, CTX, qb, kb, vb, bias_tab)
        yc = _sgu_mixer(pc, sgu_norm_g[l][None], ones_bd, sgu_all[l], sgu_bias[l])
        yd = _mla_mixer(B, S, CTX, qd, kd, vd)
        xs, hx, route = _out_projection(l, tiles_per_batch, B, (ya, yb, yc, yd), xs, mod_r, norm2_g[l][None],
                                        wo[l], rw, rb)
        tile_grp, tile_ok, src, dst = _dispatch_plan(route, n_pad)
        y = _grouped_experts(l, tile_grp, tile_ok, src, dst, hx, moe_w_gate, moe_w_up, moe_w_down)
    out = _final_norm(L - 1, n_lat, tiles_per_batch, xs, y, mod_r, final_g[None])
    return out.reshape(B, S, D)
```

```python
import functools

import numpy as np
import jax
import jax.numpy as jnp
from jax import lax
from jax.experimental import pallas as pl
from jax.experimental.pallas import tpu as pltpu

F32 = jnp.float32
BF16 = jnp.bfloat16
HIGHEST = lax.Precision.HIGHEST

GRID_W = 64
POOL_WINDOWS = (2, 4, 8, 16)
GROUP_DIM = 64
N_GROUPS = 4
MIX_PART = N_GROUPS * GROUP_DIM
NA_WIN_ROWS = 8
NA_WIN_COLS = 16
NA_Q_ROWS = 4
NA_K_ROWS = NA_Q_ROWS + NA_WIN_ROWS
SGU_CHUNK = 128
MLA_Q_RANK = 256
MLA_KV_RANK = 128
MLA_NOPE = 64
MLA_ROPE = 32
MLA_V = 64
MLA_SLOT = 128
MLA_SCALE = (MLA_NOPE + MLA_ROPE) ** -0.5
LOG2_E = float(np.log2(np.e))
ROPE_BASE = 10000.0
N_EXPERTS = 16
N_EXPERT_GROUPS = 4
EXPERTS_PER_GROUP = 4
D_EXPERT = 256
N_MOD = 6
EPS = 1e-6
NEG_INF = -1e30

OFF_B = MIX_PART
OFF_C = OFF_B + 3 * MIX_PART
OFF_D = OFF_C + 2 * MIX_PART
OFF_KR = OFF_D + MLA_Q_RANK + MLA_KV_RANK
W_IN_COLS = OFF_KR + 2 * MLA_SLOT

LANES = 128
FEAT_ROWS = 8
SLAB_ROWS = 16

TOKEN_TILE = 512
ATTN_TILE = 256
MLA_KV_TILE = 512
MOE_TILE = 512
VMEM_LIMIT = 56 * 1024 * 1024


def _cparams(*sem):
    return pltpu.CompilerParams(dimension_semantics=sem, vmem_limit_bytes=VMEM_LIMIT)


def _dot(a, b):
    return jnp.dot(a, b, preferred_element_type=F32)


def _dot_nt(a, b, precision=None):
    return lax.dot_general(a, b, (((1,), (1,)), ((), ())), precision=precision,
                           preferred_element_type=F32)


def _rms(x, g):
    return x * lax.rsqrt(jnp.mean(x * x, axis=-1, keepdims=True) + EPS) * g


def _silu(x):
    return x * jax.nn.sigmoid(x)


def _mod_body(c_ref, w_ref, b_ref, o_ref):
    o_ref[0] = jnp.dot(_silu(c_ref[...]), w_ref[0], precision=HIGHEST,
                       preferred_element_type=F32) + b_ref[0]


def _modulation(c_all, ada_w, ada_b):
    L, D, ND = ada_w.shape
    tn = 1536
    return pl.pallas_call(
        _mod_body,
        grid=(L, ND // tn),
        in_specs=[pl.BlockSpec((8, D), lambda l, j: (0, 0)),
                  pl.BlockSpec((1, D, tn), lambda l, j: (l, 0, j)),
                  pl.BlockSpec((1, 1, tn), lambda l, j: (l, 0, j))],
        out_specs=pl.BlockSpec((1, 8, tn), lambda l, j: (l, 0, j)),
        out_shape=jax.ShapeDtypeStruct((L, 8, ND), F32),
        compiler_params=_cparams("arbitrary", "arbitrary"),
        name="modulation",
    )(c_all, ada_w, ada_b.reshape(L, 1, ND))


def _inproj_body(has_res, *refs):
    if has_res:
        (x_ref, y_ref, mp_ref, m_ref, g1_ref, w_ref, wq_ref, gq_ref, wkv_ref, wvt_ref, gkv_ref, cos_ref, sin_ref,
         xo_ref, pa_ref, qb_ref, kb_ref, vb_ref, pc_ref, qd_ref, kd_ref, vd_ref) = refs
        x = x_ref[...] + mp_ref[0, 5:6, :] * _slab_rows(y_ref)
        xo_ref[...] = x
    else:
        (x_ref, m_ref, g1_ref, w_ref, wq_ref, gq_ref, wkv_ref, wvt_ref, gkv_ref, cos_ref, sin_ref,
         pa_ref, qb_ref, kb_ref, vb_ref, pc_ref, qd_ref, kd_ref, vd_ref) = refs
        x = x_ref[...]
    m = m_ref[0]
    hb = (_rms(x, g1_ref[...]) * (1.0 + m[1:2]) + m[0:1]).astype(BF16)

    def proj(a, b):
        return _dot(hb, w_ref[:, a:b])

    cq = proj(OFF_D, OFF_D + MLA_Q_RANK)
    ckv = proj(OFF_D + MLA_Q_RANK, OFF_KR)
    kr_plain = proj(OFF_KR, OFF_KR + MLA_SLOT)
    kr_swapped = proj(OFF_KR + MLA_SLOT, W_IN_COLS)

    pa_ref[...] = proj(0, OFF_B)
    qb_ref[...] = (proj(OFF_B, OFF_B + MIX_PART) * (GROUP_DIM ** -0.5)).astype(BF16)
    kb_ref[...] = proj(OFF_B + MIX_PART, OFF_B + 2 * MIX_PART).astype(BF16)
    vb_ref[...] = proj(OFF_B + 2 * MIX_PART, OFF_C).astype(BF16)
    pc_ref[...] = proj(OFF_C, OFF_D)

    cosp = cos_ref[...]
    sinp = sin_ref[...]
    qn = _rms(cq, gq_ref[...]).astype(BF16)
    qq = _dot(qn, wq_ref[...])
    half = N_GROUPS * MLA_SLOT
    for h in range(N_GROUPS):
        a = h * MLA_SLOT
        q = qq[:, a:a + MLA_SLOT] * cosp + qq[:, half + a:half + a + MLA_SLOT] * sinp
        qd_ref[:, a:a + MLA_SLOT] = (q * (MLA_SCALE * LOG2_E)).astype(BF16)
    kvn = _rms(ckv, gkv_ref[...]).astype(BF16)
    kk = _dot(kvn, wkv_ref[...])
    kr = kr_plain * cosp + kr_swapped * sinp
    for h in range(N_GROUPS):
        a = h * MLA_SLOT
        kd_ref[:, a:a + MLA_SLOT] = (kk[:, a:a + MLA_SLOT] + kr).astype(BF16)
    ones_row = (lax.broadcasted_iota(jnp.int32, (half, 1), 0) % MLA_SLOT == MLA_V).astype(F32)
    vd_ref[...] = (_dot_nt(wvt_ref[...], kvn) + ones_row).astype(BF16)


def _in_projection(l, n_lat_tiles_per_batch, n_batch, x, y, mod_r, g1, w, wq, gq, wkv, wvt, gkv, cosp, sinp):
    n, d = x.shape
    tm = TOKEN_TILE
    has_res = y is not None

    def row(i):
        return (i, 0)

    def modrow(layer):
        return lambda i: (layer * 8 + jnp.minimum(i // n_lat_tiles_per_batch, n_batch), 0, 0)

    def const2(i):
        return (0, 0)

    tok = lambda c: pl.BlockSpec((tm, c), row)
    mod_spec = lambda layer: pl.BlockSpec((1, N_MOD, d), modrow(layer))
    in_specs = [tok(d)]
    args = [x]
    if has_res:
        in_specs += [pl.BlockSpec((tm, FEAT_ROWS, LANES), lambda i: (i, 0, 0)), mod_spec(l - 1)]
        args += [y, mod_r]
    in_specs += [mod_spec(l), pl.BlockSpec((1, d), const2), pl.BlockSpec(w.shape, const2),
                 pl.BlockSpec(wq.shape, const2), pl.BlockSpec(gq.shape, const2),
                 pl.BlockSpec(wkv.shape, const2), pl.BlockSpec(wvt.shape, const2), pl.BlockSpec(gkv.shape, const2),
                 tok(MLA_SLOT), tok(MLA_SLOT)]
    args += [mod_r, g1, w, wq, gq, wkv, wvt, gkv, cosp, sinp]
    outs = [(MIX_PART, F32), (MIX_PART, BF16), (MIX_PART, BF16), (MIX_PART, BF16), (2 * MIX_PART, F32),
            (N_GROUPS * MLA_SLOT, BF16), (N_GROUPS * MLA_SLOT, BF16)]
    if has_res:
        outs = [(d, F32)] + outs
    slots = N_GROUPS * MLA_SLOT
    res = pl.pallas_call(
        functools.partial(_inproj_body, has_res),
        grid=(n // tm,),
        in_specs=in_specs,
        out_specs=[tok(c) for c, _ in outs] + [pl.BlockSpec((slots, tm), lambda i: (0, i))],
        out_shape=[jax.ShapeDtypeStruct((n, c), t) for c, t in outs] + [jax.ShapeDtypeStruct((slots, n), BF16)],
        compiler_params=_cparams("arbitrary"),
        name="in_projection",
    )(*args)
    if not has_res:
        res = [x] + list(res)
    return res


def _pool_body(n_lat, seq, ctx_len, prev_ref, cur_ref, next_ref, w_ref, s_ref, o_ref):
    tb = cur_ref.shape[0]
    ext = jnp.concatenate([prev_ref[...], cur_ref[...], next_ref[...]], axis=0)
    n = tb + 16
    g = pl.program_id(0) * tb - 8 + lax.broadcasted_iota(jnp.int32, (n, 1), 0)
    is_lat = g < n_lat
    length = jnp.where(is_lat, seq, ctx_len)
    p = jnp.where(is_lat, g & (seq - 1), (g - n_lat) & (ctx_len - 1))

    def shifted(a, j):
        r = pltpu.roll(a, (-j) % n, axis=0)
        ok = (p + j >= 0) & (p + j < length)
        return jnp.where(ok, r, 0.0)

    before1 = shifted(ext, -1)
    before2 = before1 + shifted(before1, -1)
    before4 = before2 + shifted(before2, -2)
    before8 = before4 + shifted(before4, -4)
    after2 = ext + shifted(ext, 1)
    after4 = after2 + shifted(after2, 2)
    after8 = after4 + shifted(after4, 4)
    lane_grp = lax.broadcasted_iota(jnp.int32, (1, MIX_PART), 1) // GROUP_DIM
    sl = slice(8, 8 + tb)
    tot = jnp.where(lane_grp == 0, (before1 + ext)[sl],
                    jnp.where(lane_grp == 1, (before2 + after2)[sl],
                              jnp.where(lane_grp == 2, (before4 + after4)[sl], (before8 + after8)[sl])))
    half = jnp.where(lane_grp == 0, POOL_WINDOWS[0] // 2,
                     jnp.where(lane_grp == 1, POOL_WINDOWS[1] // 2,
                               jnp.where(lane_grp == 2, POOL_WINDOWS[2] // 2, POOL_WINDOWS[3] // 2)))
    pc = p[sl]
    cnt = jnp.minimum(pc + half, length[sl]) - jnp.maximum(pc - half, 0)
    dlt = tot / cnt.astype(F32) - ext[sl]
    o_ref[...] = (_dot(dlt.astype(BF16), w_ref[...]) * s_ref[...]).astype(BF16)


def _pool_mixer(n_lat, seq, ctx_len, pa, w_bd, s):
    n = pa.shape[0]
    tb = 1024
    assert seq & (seq - 1) == 0 and ctx_len & (ctx_len - 1) == 0
    assert n_lat % tb == 0 and n % tb == 0
    nb8 = n // 8
    return pl.pallas_call(
        functools.partial(_pool_body, n_lat, seq, ctx_len),
        grid=(n // tb,),
        in_specs=[pl.BlockSpec((8, MIX_PART), lambda i: (jnp.maximum(i * (tb // 8) - 1, 0), 0)),
                  pl.BlockSpec((tb, MIX_PART), lambda i: (i, 0)),
                  pl.BlockSpec((8, MIX_PART), lambda i: (jnp.minimum((i + 1) * (tb // 8), nb8 - 1), 0)),
                  pl.BlockSpec(w_bd.shape, lambda i: (0, 0)),
                  pl.BlockSpec(s.shape, lambda i: (0, 0))],
        out_specs=pl.BlockSpec((tb, MIX_PART), lambda i: (i, 0)),
        out_shape=jax.ShapeDtypeStruct((n, MIX_PART), BF16),
        compiler_params=_cparams("arbitrary"),
        name="pool_mixer",
    )(pa, pa, pa, w_bd, s)


def _softmax_pv(parts):
    m = parts[0][0].max(axis=-1, keepdims=True)
    for s, _ in parts[1:]:
        m = jnp.maximum(m, s.max(axis=-1, keepdims=True))
    den = 0.0
    out = 0.0
    for s, v in parts:
        p = jnp.exp(s - m)
        den = den + p.sum(axis=-1, keepdims=True)
        out = out + _dot(p.astype(BF16), v)
    return out / den


def _na_body(rows, n_qt, q_ref, kl_ref, kc_ref, vl_ref, vc_ref, bias_ref, o_ref):
    qt = pl.program_id(1)
    q = q_ref[...]
    kc = kc_ref[...]
    vc = vc_ref[...]
    lane_grp = lax.broadcasted_iota(jnp.int32, (1, MIX_PART), 1) // GROUP_DIM
    zero = jnp.zeros_like(q)

    @pl.when(qt < n_qt)
    def _():
        k0 = jnp.clip(NA_Q_ROWS * qt - NA_WIN_ROWS // 2, 0, rows - NA_K_ROWS)
        off = pl.multiple_of(k0 * GRID_W, NA_Q_ROWS * GRID_W)
        kw = kl_ref[pl.ds(off, NA_K_ROWS * GRID_W), :]
        vw = vl_ref[pl.ds(off, NA_K_ROWS * GRID_W), :]
        o = jnp.zeros(q.shape, F32)
        for h in range(N_GROUPS):
            qh = jnp.where(lane_grp == h, q, zero)
            oh = _softmax_pv([(_dot_nt(qh, kw) + bias_ref[0, h], vw), (_dot_nt(qh, kc), vc)])
            o = jnp.where(lane_grp == h, oh, o)
        o_ref[...] = o.astype(BF16)

    @pl.when(qt == n_qt)
    def _():
        o = jnp.zeros(q.shape, F32)
        for h in range(N_GROUPS):
            qh = jnp.where(lane_grp == h, q, zero)
            oh = _softmax_pv([(_dot_nt(qh, kc), vc)])
            o = jnp.where(lane_grp == h, oh, o)
        o_ref[...] = o.astype(BF16)


def _attn_specs(n_batch, seq, ctx_len, width_q, width_k, width_v, width_o):
    n_qt = seq // ATTN_TILE
    lat_blocks = n_batch * n_qt
    assert ctx_len == ATTN_TILE

    def qmap(b, t):
        return (jnp.where(t < n_qt, b * n_qt + t, lat_blocks + b), 0)

    q_spec = pl.BlockSpec((ATTN_TILE, width_q), qmap)
    kl_spec = pl.BlockSpec((seq, width_k), lambda b, t: (b, 0))
    kc_spec = pl.BlockSpec((ctx_len, width_k), lambda b, t: (lat_blocks + b, 0))
    vl_spec = pl.BlockSpec((seq, width_v), lambda b, t: (b, 0))
    vc_spec = pl.BlockSpec((ctx_len, width_v), lambda b, t: (lat_blocks + b, 0))
    o_spec = pl.BlockSpec((ATTN_TILE, width_o), qmap)
    return n_qt, [q_spec, kl_spec, kc_spec, vl_spec, vc_spec], o_spec


def _na_mixer(l, n_batch, seq, ctx_len, qb, kb, vb, bias_tab):
    n = qb.shape[0]
    rows = seq // GRID_W
    n_qt, in_specs, o_spec = _attn_specs(n_batch, seq, ctx_len, MIX_PART, MIX_PART, MIX_PART, MIX_PART)
    assert ATTN_TILE == NA_Q_ROWS * GRID_W and rows >= NA_K_ROWS + NA_Q_ROWS

    def bias_map(b, t):
        kind = jnp.where(t == 0, 0, jnp.where(t >= n_qt - 1, 2, 1))
        return (l * 3 + kind, 0, 0, 0)

    in_specs.append(pl.BlockSpec((1,) + bias_tab.shape[1:], bias_map))
    return pl.pallas_call(
        functools.partial(_na_body, rows, n_qt),
        grid=(n_batch, n_qt + 1),
        in_specs=in_specs,
        out_specs=o_spec,
        out_shape=jax.ShapeDtypeStruct((n, MIX_PART), BF16),
        compiler_params=_cparams("arbitrary", "arbitrary"),
        name="neighborhood_attention",
    )(qb, kb, kb, vb, vb, bias_tab)


def _na_bias_tables(na_rpb, rows):
    L, H = na_rpb.shape[:2]
    kinds = ((0, 0), (NA_Q_ROWS, 0), (rows - NA_Q_ROWS, rows - NA_K_ROWS))
    qc = np.arange(GRID_W)
    kc = np.arange(GRID_W)
    wsc = np.clip(qc - NA_WIN_COLS // 2, 0, GRID_W - NA_WIN_COLS)
    col_ok = (kc[None, :] >= wsc[:, None]) & (kc[None, :] < wsc[:, None] + NA_WIN_COLS)
    padded = _pad_last(na_rpb, GRID_W - NA_WIN_COLS, GRID_W - NA_WIN_COLS)
    shifted = jnp.stack([padded[..., GRID_W - 1 - c:2 * GRID_W - 1 - c] for c in range(GRID_W)], axis=2)
    slabs = jnp.where(col_ok[:, None, :], shifted, NEG_INF)

    def masked(n):
        return jnp.full((L, H, GRID_W, n, GRID_W), NEG_INF, F32)

    tabs = []
    for r0, k0 in kinds:
        per_row = []
        for qr in range(NA_Q_ROWS):
            r = r0 + qr
            first = int(np.clip(r - NA_WIN_ROWS // 2, 0, rows - NA_WIN_ROWS)) - k0
            d0 = k0 + first - r + NA_WIN_ROWS - 1
            assert 0 <= first <= NA_K_ROWS - NA_WIN_ROWS and 0 <= d0 <= NA_WIN_ROWS - 1
            per_row.append(jnp.concatenate(
                [masked(first), slabs[:, :, :, d0:d0 + NA_WIN_ROWS, :], masked(NA_K_ROWS - NA_WIN_ROWS - first)],
                axis=3))
        tabs.append(jnp.stack(per_row, axis=2))
    return jnp.stack(tabs, axis=1).reshape(L * 3, H, NA_Q_ROWS * GRID_W, NA_K_ROWS * GRID_W)


def _gelu_tanh(x):
    return 0.5 * x * (1.0 + jnp.tanh(np.sqrt(2.0 / np.pi).astype(np.float32) * (x + 0.044715 * (x * x * x))))


def _sgu_body(pc_ref, gn_ref, ones_ref, w_ref, b_ref, o_ref):
    tm = pc_ref.shape[0]
    uv = _gelu_tanh(pc_ref[...])
    u = uv[:, :MIX_PART]
    v = uv[:, MIX_PART:]
    ms = jnp.dot(v * v, ones_ref[...], precision=HIGHEST, preferred_element_type=F32)
    vg = (v * lax.rsqrt(ms + EPS) * gn_ref[...]).astype(BF16)
    lane_grp = lax.broadcasted_iota(jnp.int32, (1, MIX_PART), 1) // GROUP_DIM
    w = w_ref[...]
    for c in range(tm // SGU_CHUNK):
        rs = slice(c * SGU_CHUNK, (c + 1) * SGU_CHUNK)
        r = _dot(w, vg[rs])
        mixed = r[:SGU_CHUNK]
        for g in range(1, N_GROUPS):
            mixed = jnp.where(lane_grp == g, r[g * SGU_CHUNK:(g + 1) * SGU_CHUNK], mixed)
        o_ref[rs, :] = (u[rs] * (mixed + b_ref[...])).astype(BF16)


def _sgu_mixer(pc, gn, ones_bd, w_all, b_exp):
    n = pc.shape[0]
    tm = TOKEN_TILE
    const = lambda i: (0, 0)
    return pl.pallas_call(
        _sgu_body,
        grid=(n // tm,),
        in_specs=[pl.BlockSpec((tm, 2 * MIX_PART), lambda i: (i, 0)),
                  pl.BlockSpec(gn.shape, const), pl.BlockSpec(ones_bd.shape, const),
                  pl.BlockSpec(w_all.shape, const), pl.BlockSpec(b_exp.shape, const)],
        out_specs=pl.BlockSpec((tm, MIX_PART), lambda i: (i, 0)),
        out_shape=jax.ShapeDtypeStruct((n, MIX_PART), BF16),
        compiler_params=_cparams("arbitrary"),
        name="spatial_gating",
    )(pc, gn, ones_bd, w_all, b_exp)


def _mla_body(n_qt, n_kv, q_ref, kl_ref, kc_ref, vl_ref, vc_ref, o_ref, sa_ref, sb_ref):
    qt = pl.program_id(1)
    n_lat = jnp.where(qt < n_qt, n_kv, 0)
    heads = [slice(h * MLA_SLOT, (h + 1) * MLA_SLOT) for h in range(N_GROUPS)]

    def scores(t, h, dst):
        off = pl.multiple_of(t * MLA_KV_TILE, MLA_KV_TILE)
        dst[h] = _dot_nt(kl_ref[pl.ds(off, MLA_KV_TILE), heads[h]], q_ref[:, heads[h]])

    def consume(t, h, src, m, acc):
        off = pl.multiple_of(t * MLA_KV_TILE, MLA_KV_TILE)
        s = src[h]
        mn = jnp.maximum(m, s.max(axis=0, keepdims=True))
        p = jnp.exp2(s - mn).astype(BF16)
        return mn, jnp.exp2(m - mn) * acc + _dot(vl_ref[heads[h], pl.ds(off, MLA_KV_TILE)], p)

    ctx_scores = [_dot_nt(kc_ref[:, hs], q_ref[:, hs]) for hs in heads]
    for h in range(N_GROUPS):
        scores(0, h, sa_ref)
    state = []
    for hs, s in zip(heads, ctx_scores):
        m = s.max(axis=0, keepdims=True)
        state.append((m, _dot(vc_ref[hs, :], jnp.exp2(s - m).astype(BF16))))

    def step(i, carry):
        t = 2 * i
        mid = []
        for h, (m, acc) in enumerate(carry):
            scores(t + 1, h, sb_ref)
            mid.append(consume(t, h, sa_ref, m, acc))
        out = []
        for h, (m, acc) in enumerate(mid):
            scores(jnp.minimum(t + 2, n_kv - 1), h, sa_ref)
            out.append(consume(t + 1, h, sb_ref, m, acc))
        return tuple(out)

    state = lax.fori_loop(0, n_lat // 2, step, tuple(state))
    out_t = jnp.concatenate([acc[:MLA_V] / acc[MLA_V:MLA_V + 1] for _, acc in state], axis=0)
    o_ref[...] = out_t.T.astype(BF16)


def _mla_mixer(n_batch, seq, ctx_len, qd, kd, vdt):
    n = qd.shape[0]
    slots = N_GROUPS * MLA_SLOT
    n_qt, in_specs, o_spec = _attn_specs(n_batch, seq, ctx_len, slots, slots, slots, N_GROUPS * MLA_V)
    lat_blocks = n_batch * n_qt
    in_specs[3] = pl.BlockSpec((slots, seq), lambda b, t: (0, b))
    in_specs[4] = pl.BlockSpec((slots, ctx_len), lambda b, t: (0, lat_blocks + b))
    assert seq % MLA_KV_TILE == 0
    return pl.pallas_call(
        functools.partial(_mla_body, n_qt, seq // MLA_KV_TILE),
        grid=(n_batch, n_qt + 1),
        in_specs=in_specs,
        out_specs=o_spec,
        out_shape=jax.ShapeDtypeStruct((n, N_GROUPS * MLA_V), BF16),
        scratch_shapes=[pltpu.VMEM((N_GROUPS, MLA_KV_TILE, ATTN_TILE), F32)] * 2,
        compiler_params=_cparams("arbitrary", "arbitrary"),
        name="latent_attention",
    )(qd, kd, kd, vdt, vdt)


def _outproj_body(ya_ref, yb_ref, yc_ref, yd_ref, x_ref, m_ref, g2_ref, w_ref, rw_ref, rb_ref,
                  xo_ref, hx_ref, r_ref):
    tm = x_ref.shape[0]
    m = m_ref[0]
    y = _dot(ya_ref[...], w_ref[0:MIX_PART, :])
    for k, ref in enumerate((yb_ref, yc_ref, yd_ref), start=1):
        y = y + _dot(ref[...], w_ref[k * MIX_PART:(k + 1) * MIX_PART, :])
    x = x_ref[...] + m[2:3] * y
    xo_ref[...] = x
    h = _rms(x, g2_ref[...]) * (1.0 + m[4:5]) + m[3:4]

    scores = jax.nn.sigmoid(_dot_nt(rw_ref[...], h, precision=HIGHEST))
    biased = scores + rb_ref[...]
    E = EXPERTS_PER_GROUP
    bk = [biased[8 * k:8 * k + 8] for k in range(E)]
    sk = [scores[8 * k:8 * k + 8] for k in range(E)]
    gs = None
    for a in range(E):
        for b in range(a + 1, E):
            pair = bk[a] + bk[b]
            gs = pair if gs is None else jnp.maximum(gs, pair)
    best = gs[0:1]
    idx = jnp.zeros((1, tm), jnp.int32)
    for g in range(1, N_EXPERT_GROUPS):
        better = gs[g:g + 1] > best
        idx = jnp.where(better, g, idx)
        best = jnp.where(better, gs[g:g + 1], best)
    in_grp = lax.broadcasted_iota(jnp.int32, (8, tm), 0) == idx
    wk = []
    for k in range(E):
        rank = jnp.zeros((8, tm), jnp.int32)
        for j in range(E):
            if j != k:
                ahead = (bk[j] > bk[k]) | ((bk[j] == bk[k]) & (j < k))
                rank = rank + ahead.astype(jnp.int32)
        wk.append(jnp.where((rank < 2) & in_grp, sk[k], 0.0).sum(axis=0, keepdims=True))
    den = wk[0] + wk[1] + wk[2] + wk[3]
    ri = lax.broadcasted_iota(jnp.int32, (8, tm), 0)
    out = jnp.where(ri == E, idx.astype(F32), 0.0)
    for k in range(E):
        out = jnp.where(ri == k, wk[k] / den, out)
    r_ref[...] = out

    for s in range(FEAT_ROWS):
        hx_ref[:, s, :] = h[:, s * LANES:(s + 1) * LANES]
    hx_ref[:, FEAT_ROWS, :] = jnp.concatenate([out, jnp.zeros((LANES - 8, tm), F32)], axis=0).T
    hx_ref[:, FEAT_ROWS + 1:, :] = jnp.zeros((tm, SLAB_ROWS - FEAT_ROWS - 1, LANES), F32)


def _out_projection(l, n_lat_tiles_per_batch, n_batch, ys, x, mod_r, g2, w, rw, rb):
    n, d = x.shape
    tm = TOKEN_TILE
    row = lambda i: (i, 0)
    const = lambda i: (0, 0)
    modrow = lambda i: (l * 8 + jnp.minimum(i // n_lat_tiles_per_batch, n_batch), 0, 0)
    in_specs = [pl.BlockSpec((tm, MIX_PART), row)] * 4 + [
        pl.BlockSpec((tm, d), row), pl.BlockSpec((1, N_MOD, d), modrow), pl.BlockSpec((1, d), const),
        pl.BlockSpec(w.shape, const), pl.BlockSpec(rw.shape, const), pl.BlockSpec(rb.shape, const)]
    return pl.pallas_call(
        _outproj_body,
        grid=(n // tm,),
        in_specs=in_specs,
        out_specs=[pl.BlockSpec((tm, d), row), pl.BlockSpec((tm, SLAB_ROWS, LANES), lambda i: (i, 0, 0)),
                   pl.BlockSpec((8, tm), lambda i: (0, i))],
        out_shape=[jax.ShapeDtypeStruct((n, d), F32), jax.ShapeDtypeStruct((n, SLAB_ROWS, LANES), F32),
                   jax.ShapeDtypeStruct((8, n), F32)],
        compiler_params=_cparams("arbitrary"),
        name="out_projection_routing",
    )(*ys, x, mod_r, g2, w, rw, rb)


def _moe_body(tg_ref, tv_ref, src_ref, dst_ref, hx_hbm, wg_ref, wu_ref, wd_ref, y_hbm,
              hbuf, ybuf, wgb, wub, wdb, gsem, ssem):
    i = pl.program_id(0)
    n_t = pl.num_programs(0)
    tm = hbuf.shape[1]
    slot = i % 2
    last = n_t - 1
    grp = tg_ref[i]
    prev_valid = tv_ref[jnp.maximum(i - 1, 0)] > 0

    def gather_copy(r, t, sl):
        return pltpu.make_async_copy(hx_hbm.at[src_ref[t * tm + r]], hbuf.at[sl, r], gsem.at[sl])

    def scatter_copy(r, t, sl):
        return pltpu.make_async_copy(ybuf.at[sl, r], y_hbm.at[dst_ref[t * tm + r]], ssem.at[sl])

    def start_all(copy, t, sl):
        def body(j, c):
            copy(2 * j, t, sl).start(priority=0)
            copy(2 * j + 1, t, sl).start(priority=1)
            return c
        lax.fori_loop(0, tm // 2, body, 0, unroll=4)

    def wait_gather(sl):
        pltpu.make_async_copy(hx_hbm.at[pl.ds(0, tm)], hbuf.at[sl], gsem.at[sl]).wait()

    def wait_scatter(sl):
        pltpu.make_async_copy(ybuf.at[sl], y_hbm.at[pl.ds(0, tm)], ssem.at[sl]).wait()

    @pl.when(i == 0)
    def _():
        start_all(gather_copy, 0, 0)

    @pl.when((i == 0) | (grp != tg_ref[jnp.maximum(i - 1, 0)]))
    def _():
        wgb[...] = wg_ref[0].astype(BF16)
        wub[...] = wu_ref[0].astype(BF16)
        wdb[...] = wd_ref[0].astype(BF16)

    @pl.when((i == 0) | prev_valid)
    def _():
        wait_gather(slot)

    @pl.when(i >= 2)
    def _():
        wait_scatter(slot)

    def expert_tile(with_scatter):
        nxt = jnp.minimum(i + 1, last)
        n_batches = 3 * EXPERTS_PER_GROUP
        s_bounds = [min(tm, (b * tm) // (n_batches - 4)) for b in range(n_batches + 1)]
        g_bounds = [min(tm, (b * tm) // (n_batches - 2)) for b in range(n_batches + 1)]

        def start_batch(b):
            if with_scatter:
                for r in range(s_bounds[b], s_bounds[b + 1]):
                    scatter_copy(r, i - 1, 1 - slot).start(priority=1)
            for r in range(g_bounds[b], g_bounds[b + 1]):
                gather_copy(r, nxt, 1 - slot).start(priority=0)

        h = jnp.concatenate([hbuf[slot, :, s, :] for s in range(FEAT_ROWS)], axis=1).astype(BF16)
        gates = hbuf[slot, :, FEAT_ROWS, :]
        acc = jnp.zeros((tm, FEAT_ROWS * LANES), F32)
        for k in range(EXPERTS_PER_GROUP):
            start_batch(3 * k)
            a = _dot(h, wgb[k])
            start_batch(3 * k + 1)
            u = _dot(h, wub[k])
            hid = _silu(a) * u * gates[:, k:k + 1]
            start_batch(3 * k + 2)
            acc = acc + _dot(hid.astype(BF16), wdb[k])
        for s in range(FEAT_ROWS):
            ybuf[slot, :, s, :] = acc[:, s * LANES:(s + 1) * LANES]

    valid = tv_ref[i] > 0

    @pl.when(valid & (i == 0))
    def _():
        expert_tile(False)

    @pl.when(valid & (i > 0))
    def _():
        expert_tile(True)

    @pl.when(jnp.logical_not(valid))
    def _():
        ybuf[slot] = jnp.zeros(ybuf.shape[1:], F32)

        @pl.when(i > 0)
        def _():
            start_all(scatter_copy, i - 1, 1 - slot)

    @pl.when(i == last)
    def _():
        start_all(scatter_copy, i, slot)
        wait_scatter(slot)

        @pl.when(i > 0)
        def _():
            wait_scatter(1 - slot)

        @pl.when(valid)
        def _():
            wait_gather(1 - slot)


def _grouped_experts(l, tile_grp, tile_ok, src, dst, hx, w_gate, w_up, w_down):
    n_pad = src.shape[0]
    tm = MOE_TILE
    E = EXPERTS_PER_GROUP
    d = FEAT_ROWS * LANES
    wmap = lambda i, tg, tv, sr, ds: (l, tg[i], 0, 0)
    grid_spec = pltpu.PrefetchScalarGridSpec(
        num_scalar_prefetch=4,
        grid=(n_pad // tm,),
        in_specs=[pl.BlockSpec(memory_space=pl.ANY),
                  pl.BlockSpec((1, E, d, D_EXPERT), wmap),
                  pl.BlockSpec((1, E, d, D_EXPERT), wmap),
                  pl.BlockSpec((1, E, D_EXPERT, d), wmap)],
        out_specs=pl.BlockSpec(memory_space=pl.ANY),
        scratch_shapes=[pltpu.VMEM((2, tm, SLAB_ROWS, LANES), F32), pltpu.VMEM((2, tm, FEAT_ROWS, LANES), F32),
                        pltpu.VMEM((E, d, D_EXPERT), BF16), pltpu.VMEM((E, d, D_EXPERT), BF16),
                        pltpu.VMEM((E, D_EXPERT, d), BF16),
                        pltpu.SemaphoreType.DMA((2,)), pltpu.SemaphoreType.DMA((2,))])
    return pl.pallas_call(
        _moe_body,
        grid_spec=grid_spec,
        out_shape=jax.ShapeDtypeStruct((n_pad, FEAT_ROWS, LANES), F32),
        compiler_params=_cparams("arbitrary"),
        name="grouped_experts",
    )(tile_grp, tile_ok, src, dst, hx, w_gate, w_up, w_down)


def _dispatch_plan(route, n_pad):
    n = route.shape[1]
    tm = MOE_TILE
    grp = route[EXPERTS_PER_GROUP].astype(jnp.int32)
    onehot = (grp[:, None] == jnp.arange(N_EXPERT_GROUPS)[None, :]).astype(jnp.int32)
    csum = jnp.cumsum(onehot, axis=0)
    counts = csum[-1]
    rank = jnp.sum(csum * onehot, axis=1) - 1
    padded = ((counts + tm - 1) // tm) * tm
    ends = jnp.cumsum(padded)
    pos = jnp.sum((ends - padded)[None, :] * onehot, axis=1) + rank
    owner = jnp.zeros((n_pad,), jnp.int32).at[pos].set(jnp.arange(1, n + 1, dtype=jnp.int32))
    used = owner > 0
    src = jnp.maximum(owner - 1, 0)
    dst = jnp.where(used, owner - 1, n - 1 + jnp.cumsum(1 - used.astype(jnp.int32)))
    tile_start = jnp.arange(n_pad // tm, dtype=jnp.int32) * tm
    tile_grp = jnp.minimum(jnp.sum(tile_start[:, None] >= ends[None, :], axis=1), N_EXPERT_GROUPS - 1)
    tile_ok = (tile_start < ends[-1]).astype(jnp.int32)
    return tile_grp.astype(jnp.int32), tile_ok, src, dst


def _slab_rows(ref):
    return jnp.concatenate([ref[:, s, :] for s in range(FEAT_ROWS)], axis=1)


def _final_body(x_ref, y_ref, m_ref, g_ref, o_ref):
    x = x_ref[...] + m_ref[0, 5:6, :] * _slab_rows(y_ref)
    o_ref[...] = _rms(x, g_ref[...])


def _final_norm(l, n_lat, n_lat_tiles_per_batch, x, y, mod_r, g):
    d = x.shape[1]
    tm = TOKEN_TILE
    row = lambda i: (i, 0)
    return pl.pallas_call(
        _final_body,
        grid=(n_lat // tm,),
        in_specs=[pl.BlockSpec((tm, d), row), pl.BlockSpec((tm, FEAT_ROWS, LANES), lambda i: (i, 0, 0)),
                  pl.BlockSpec((1, N_MOD, d), lambda i: (l * 8 + i // n_lat_tiles_per_batch, 0, 0)),
                  pl.BlockSpec((1, d), lambda i: (0, 0))],
        out_specs=pl.BlockSpec((tm, d), row),
        out_shape=jax.ShapeDtypeStruct((n_lat, d), F32),
        compiler_params=_cparams("arbitrary"),
        name="final_norm",
    )(x, y, mod_r, g)


def _rope_tables(seq, n_batch, n_ctx_tokens):
    t = np.arange(seq)
    nf = MLA_ROPE // 4
    inv = jnp.asarray(ROPE_BASE, F32) ** (-jnp.arange(nf, dtype=F32) / nf)
    rang = jnp.asarray(t // GRID_W, F32)[:, None] * inv
    cang = jnp.asarray(t % GRID_W, F32)[:, None] * inv
    cr, sr, cc, sc = jnp.cos(rang), jnp.sin(rang), jnp.cos(cang), jnp.sin(cang)
    one = jnp.ones((seq, MLA_NOPE), F32)
    zero_tail = jnp.zeros((seq, MLA_SLOT - MLA_NOPE - MLA_ROPE), F32)
    cosp = jnp.concatenate([one, cr, cr, cc, cc, zero_tail], axis=1)
    sinp = jnp.concatenate([0 * one, -sr, sr, -sc, sc, zero_tail], axis=1)
    ctx_cos = jnp.concatenate([jnp.ones((n_ctx_tokens, MLA_NOPE + MLA_ROPE), F32),
                               jnp.zeros((n_ctx_tokens, MLA_SLOT - MLA_NOPE - MLA_ROPE), F32)], axis=1)
    cosp = jnp.concatenate([jnp.tile(cosp, (n_batch, 1)), ctx_cos], axis=0)
    sinp = jnp.concatenate([jnp.tile(sinp, (n_batch, 1)), jnp.zeros((n_ctx_tokens, MLA_SLOT), F32)], axis=0)
    return cosp, sinp


_ROPE_SWAP = np.concatenate([np.arange(8, 16), np.arange(0, 8), np.arange(24, 32), np.arange(16, 24)])


def _pad_last(a, before, after):
    return jnp.pad(a, [(0, 0)] * (a.ndim - 1) + [(before, after)])


def _layouts(w_in, w_out, pool_w, sgu_w, sgu_b, mla_w_uq, mla_w_ukv, router_w, router_b):
    L = w_in.shape[0]
    tail = MLA_SLOT - MLA_NOPE - MLA_ROPE
    kr = w_in[:, :, OFF_KR:]
    w = jnp.concatenate([w_in[:, :, :OFF_KR], _pad_last(kr, MLA_NOPE, tail),
                         _pad_last(kr[:, :, _ROPE_SWAP], MLA_NOPE, tail)], axis=-1).astype(BF16)
    uq = mla_w_uq.reshape(L, MLA_Q_RANK, N_GROUPS, MLA_NOPE + MLA_ROPE)
    q1 = _pad_last(uq, 0, tail).reshape(L, MLA_Q_RANK, N_GROUPS * MLA_SLOT)
    q2 = _pad_last(uq[..., MLA_NOPE:][..., _ROPE_SWAP], MLA_NOPE, tail).reshape(L, MLA_Q_RANK, N_GROUPS * MLA_SLOT)
    wq = jnp.concatenate([q1, q2], axis=-1).astype(BF16)
    ukv = mla_w_ukv.reshape(L, MLA_KV_RANK, N_GROUPS, MLA_NOPE + MLA_V)
    kn = _pad_last(ukv[..., :MLA_NOPE], 0, MLA_SLOT - MLA_NOPE).reshape(L, MLA_KV_RANK, N_GROUPS * MLA_SLOT)
    vv = _pad_last(ukv[..., MLA_NOPE:], 0, MLA_SLOT - MLA_V).reshape(L, MLA_KV_RANK, N_GROUPS * MLA_SLOT)
    wkv = kn.astype(BF16)
    wvt = jnp.swapaxes(vv, 1, 2).astype(BF16)
    eye = jnp.eye(N_GROUPS, dtype=F32)
    pool_bd = jnp.einsum('lgcd,gh->lgchd', pool_w, eye).reshape(L, MIX_PART, MIX_PART).astype(BF16)
    ones_bd = jnp.asarray(np.kron(np.eye(N_GROUPS), np.full((GROUP_DIM, GROUP_DIM), 1.0 / GROUP_DIM)), F32)
    sgu_all = sgu_w.reshape(L, N_GROUPS * SGU_CHUNK, SGU_CHUNK).astype(BF16)
    sgu_bias = jnp.repeat(jnp.swapaxes(sgu_b, 1, 2), GROUP_DIM, axis=2)
    rw = router_w.T.reshape(N_EXPERT_GROUPS, EXPERTS_PER_GROUP, -1).transpose(1, 0, 2)
    rw = jnp.pad(rw, ((0, 0), (0, 8 - N_EXPERT_GROUPS), (0, 0))).reshape(8 * EXPERTS_PER_GROUP, -1)
    rb = router_b.reshape(N_EXPERT_GROUPS, EXPERTS_PER_GROUP).T
    rb = jnp.pad(rb, ((0, 0), (0, 8 - N_EXPERT_GROUPS))).reshape(8 * EXPERTS_PER_GROUP, 1)
    return w, wq, wkv, wvt, w_out.astype(BF16), pool_bd, ones_bd, sgu_all, sgu_bias, rw, rb


def kernel(x, c, ctx, c_ctx, ada_w, ada_b, norm1_g, norm2_g, w_in, w_out, pool_w, pool_s, na_rpb,
           sgu_norm_g, sgu_w, sgu_b, mla_q_norm_g, mla_w_uq, mla_kv_norm_g, mla_w_ukv,
           router_w, router_b, moe_w_gate, moe_w_up, moe_w_down, final_g):
    B, S, D = x.shape
    CTX = ctx.shape[1]
    L = ada_w.shape[0]
    n_lat, n_ctx = B * S, B * CTX
    n_all = n_lat + n_ctx
    assert B < 8 and S % TOKEN_TILE == 0 and n_ctx % TOKEN_TILE == 0
    tiles_per_batch = S // TOKEN_TILE
    n_pad = n_all + N_EXPERT_GROUPS * MOE_TILE

    c_all = jnp.concatenate([c, c_ctx[None, :], jnp.zeros((8 - B - 1, D), F32)], axis=0)
    mod_r = _modulation(c_all, ada_w, ada_b).reshape(L * 8, N_MOD, D)
    w, wq, wkv, wvt, wo, pool_bd, ones_bd, sgu_all, sgu_bias, rw, rb = _layouts(
        w_in, w_out, pool_w, sgu_w, sgu_b, mla_w_uq, mla_w_ukv, router_w, router_b)
    bias_tab = _na_bias_tables(na_rpb, S // GRID_W)
    cosp, sinp = _rope_tables(S, B, n_ctx)

    xs = jnp.concatenate([x.reshape(n_lat, D), ctx.reshape(n_ctx, D)], axis=0)
    y = None
    for l in range(L):
        xs, pa, qb, kb, vb, pc, qd, kd, vd = _in_projection(
            l, tiles_per_batch, B, xs, y, mod_r, norm1_g[l][None], w[l], wq[l], mla_q_norm_g[l][None],
            wkv[l], wvt[l], mla_kv_norm_g[l][None], cosp, sinp)
        ya = _pool_mixer(n_lat, S, CTX, pa, pool_bd[l], pool_s[l][None])
        yb = _na_mixer(l, B, S, CTX, qb, kb, vb, bias_tab)
        yc = _sgu_mixer(pc, sgu_norm_g[l][None], ones_bd, sgu_all[l], sgu_bias[l])
        yd = _mla_mixer(B, S, CTX, qd, kd, vd)
        xs, hx, route = _out_projection(l, tiles_per_batch, B, (ya, yb, yc, yd), xs, mod_r, norm2_g[l][None],
                                        wo[l], rw, rb)
        tile_grp, tile_ok, src, dst = _dispatch_plan(route, n_pad)
        y = _grouped_experts(l, tile_grp, tile_ok, src, dst, hx, moe_w_gate, moe_w_up, moe_w_down)
    out = _final_norm(L - 1, n_lat, tiles_per_batch, xs, y, mod_r, final_g[None])
    return out.reshape(B, S, D)
```

```python
import functools

import numpy as np
import jax
import jax.numpy as jnp
from jax import lax
from jax.experimental import pallas as pl
from jax.experimental.pallas import tpu as pltpu

F32 = jnp.float32
BF16 = jnp.bfloat16
HIGHEST = lax.Precision.HIGHEST

GRID_W = 64
POOL_WINDOWS = (2, 4, 8, 16)
GROUP_DIM = 64
N_GROUPS = 4
MIX_PART = N_GROUPS * GROUP_DIM
NA_WIN_ROWS = 8
NA_WIN_COLS = 16
NA_Q_ROWS = 4
NA_K_ROWS = NA_Q_ROWS + NA_WIN_ROWS
SGU_CHUNK = 128
MLA_Q_RANK = 256
MLA_KV_RANK = 128
MLA_NOPE = 64
MLA_ROPE = 32
MLA_V = 64
MLA_SLOT = 128
MLA_SCALE = (MLA_NOPE + MLA_ROPE) ** -0.5
LOG2_E = float(np.log2(np.e))
ROPE_BASE = 10000.0
N_EXPERTS = 16
N_EXPERT_GROUPS = 4
EXPERTS_PER_GROUP = 4
D_EXPERT = 256
N_MOD = 6
EPS = 1e-6
NEG_INF = -1e30

OFF_B = MIX_PART
OFF_C = OFF_B + 3 * MIX_PART
OFF_D = OFF_C + 2 * MIX_PART
OFF_KR = OFF_D + MLA_Q_RANK + MLA_KV_RANK
W_IN_COLS = OFF_KR + 2 * MLA_SLOT

LANES = 128
FEAT_ROWS = 8
SLAB_ROWS = 16

TOKEN_TILE = 512
ATTN_TILE = 256
MLA_KV_TILE = 512
MLA_MAX_UNROLLED_TILES = 16
MOE_TILE = 512
VMEM_LIMIT = 56 * 1024 * 1024


def _cparams(*sem):
    return pltpu.CompilerParams(dimension_semantics=sem, vmem_limit_bytes=VMEM_LIMIT)


def _dot(a, b):
    return jnp.dot(a, b, preferred_element_type=F32)


def _dot_nt(a, b, precision=None):
    return lax.dot_general(a, b, (((1,), (1,)), ((), ())), precision=precision,
                           preferred_element_type=F32)


def _rms(x, g):
    return x * lax.rsqrt(jnp.mean(x * x, axis=-1, keepdims=True) + EPS) * g


def _silu(x):
    return x * jax.nn.sigmoid(x)


def _mod_body(c_ref, w_ref, b_ref, o_ref):
    o_ref[0] = jnp.dot(_silu(c_ref[...]), w_ref[0], precision=HIGHEST,
                       preferred_element_type=F32) + b_ref[0]


def _modulation(c_all, ada_w, ada_b):
    L, D, ND = ada_w.shape
    tn = 1536
    return pl.pallas_call(
        _mod_body,
        grid=(L, ND // tn),
        in_specs=[pl.BlockSpec((8, D), lambda l, j: (0, 0)),
                  pl.BlockSpec((1, D, tn), lambda l, j: (l, 0, j)),
                  pl.BlockSpec((1, 1, tn), lambda l, j: (l, 0, j))],
        out_specs=pl.BlockSpec((1, 8, tn), lambda l, j: (l, 0, j)),
        out_shape=jax.ShapeDtypeStruct((L, 8, ND), F32),
        compiler_params=_cparams("arbitrary", "arbitrary"),
        name="modulation",
    )(c_all, ada_w, ada_b.reshape(L, 1, ND))


def _inproj_body(has_res, *refs):
    if has_res:
        (x_ref, y_ref, mp_ref, m_ref, g1_ref, w_ref, wq_ref, gq_ref, wkv_ref, wvt_ref, gkv_ref, cos_ref, sin_ref,
         xo_ref, pa_ref, qb_ref, kb_ref, vb_ref, pc_ref, qd_ref, kd_ref, vd_ref) = refs
        x = x_ref[...] + mp_ref[0, 5:6, :] * _slab_rows(y_ref)
        xo_ref[...] = x
    else:
        (x_ref, m_ref, g1_ref, w_ref, wq_ref, gq_ref, wkv_ref, wvt_ref, gkv_ref, cos_ref, sin_ref,
         pa_ref, qb_ref, kb_ref, vb_ref, pc_ref, qd_ref, kd_ref, vd_ref) = refs
        x = x_ref[...]
    m = m_ref[0]
    hb = (_rms(x, g1_ref[...]) * (1.0 + m[1:2]) + m[0:1]).astype(BF16)

    def proj(a, b):
        return _dot(hb, w_ref[:, a:b])

    cq = proj(OFF_D, OFF_D + MLA_Q_RANK)
    ckv = proj(OFF_D + MLA_Q_RANK, OFF_KR)
    kr_plain = proj(OFF_KR, OFF_KR + MLA_SLOT)
    kr_swapped = proj(OFF_KR + MLA_SLOT, W_IN_COLS)

    pa_ref[...] = proj(0, OFF_B)
    qb_ref[...] = (proj(OFF_B, OFF_B + MIX_PART) * (GROUP_DIM ** -0.5)).astype(BF16)
    kb_ref[...] = proj(OFF_B + MIX_PART, OFF_B + 2 * MIX_PART).astype(BF16)
    vb_ref[...] = proj(OFF_B + 2 * MIX_PART, OFF_C).astype(BF16)
    pc_ref[...] = proj(OFF_C, OFF_D)

    cosp = cos_ref[...]
    sinp = sin_ref[...]
    qn = _rms(cq, gq_ref[...]).astype(BF16)
    qq = _dot(qn, wq_ref[...])
    half = N_GROUPS * MLA_SLOT
    for h in range(N_GROUPS):
        a = h * MLA_SLOT
        q = qq[:, a:a + MLA_SLOT] * cosp + qq[:, half + a:half + a + MLA_SLOT] * sinp
        qd_ref[:, a:a + MLA_SLOT] = (q * (MLA_SCALE * LOG2_E)).astype(BF16)
    kvn = _rms(ckv, gkv_ref[...]).astype(BF16)
    kk = _dot(kvn, wkv_ref[...])
    kr = kr_plain * cosp + kr_swapped * sinp
    for h in range(N_GROUPS):
        a = h * MLA_SLOT
        kd_ref[:, a:a + MLA_SLOT] = (kk[:, a:a + MLA_SLOT] + kr).astype(BF16)
    ones_row = (lax.broadcasted_iota(jnp.int32, (half, 1), 0) % MLA_SLOT == MLA_V).astype(F32)
    vd_ref[...] = (_dot_nt(wvt_ref[...], kvn) + ones_row).astype(BF16)


def _in_projection(l, n_lat_tiles_per_batch, n_batch, x, y, mod_r, g1, w, wq, gq, wkv, wvt, gkv, cosp, sinp):
    n, d = x.shape
    tm = TOKEN_TILE
    has_res = y is not None

    def row(i):
        return (i, 0)

    def modrow(layer):
        return lambda i: (layer * 8 + jnp.minimum(i // n_lat_tiles_per_batch, n_batch), 0, 0)

    def const2(i):
        return (0, 0)

    tok = lambda c: pl.BlockSpec((tm, c), row)
    mod_spec = lambda layer: pl.BlockSpec((1, N_MOD, d), modrow(layer))
    in_specs = [tok(d)]
    args = [x]
    if has_res:
        in_specs += [pl.BlockSpec((tm, FEAT_ROWS, LANES), lambda i: (i, 0, 0)), mod_spec(l - 1)]
        args += [y, mod_r]
    in_specs += [mod_spec(l), pl.BlockSpec((1, d), const2), pl.BlockSpec(w.shape, const2),
                 pl.BlockSpec(wq.shape, const2), pl.BlockSpec(gq.shape, const2),
                 pl.BlockSpec(wkv.shape, const2), pl.BlockSpec(wvt.shape, const2), pl.BlockSpec(gkv.shape, const2),
                 tok(MLA_SLOT), tok(MLA_SLOT)]
    args += [mod_r, g1, w, wq, gq, wkv, wvt, gkv, cosp, sinp]
    outs = [(MIX_PART, F32), (MIX_PART, BF16), (MIX_PART, BF16), (MIX_PART, BF16), (2 * MIX_PART, F32),
            (N_GROUPS * MLA_SLOT, BF16), (N_GROUPS * MLA_SLOT, BF16)]
    if has_res:
        outs = [(d, F32)] + outs
    slots = N_GROUPS * MLA_SLOT
    res = pl.pallas_call(
        functools.partial(_inproj_body, has_res),
        grid=(n // tm,),
        in_specs=in_specs,
        out_specs=[tok(c) for c, _ in outs] + [pl.BlockSpec((slots, tm), lambda i: (0, i))],
        out_shape=[jax.ShapeDtypeStruct((n, c), t) for c, t in outs] + [jax.ShapeDtypeStruct((slots, n), BF16)],
        compiler_params=_cparams("arbitrary"),
        name="in_projection",
    )(*args)
    if not has_res:
        res = [x] + list(res)
    return res


def _pool_body(n_lat, seq, ctx_len, prev_ref, cur_ref, next_ref, w_ref, s_ref, o_ref):
    tb = cur_ref.shape[0]
    ext = jnp.concatenate([prev_ref[...], cur_ref[...], next_ref[...]], axis=0)
    n = tb + 16
    g = pl.program_id(0) * tb - 8 + lax.broadcasted_iota(jnp.int32, (n, 1), 0)
    is_lat = g < n_lat
    length = jnp.where(is_lat, seq, ctx_len)
    p = jnp.where(is_lat, g & (seq - 1), (g - n_lat) & (ctx_len - 1))

    def shifted(a, j):
        r = pltpu.roll(a, (-j) % n, axis=0)
        ok = (p + j >= 0) & (p + j < length)
        return jnp.where(ok, r, 0.0)

    before1 = shifted(ext, -1)
    before2 = before1 + shifted(before1, -1)
    before4 = before2 + shifted(before2, -2)
    before8 = before4 + shifted(before4, -4)
    after2 = ext + shifted(ext, 1)
    after4 = after2 + shifted(after2, 2)
    after8 = after4 + shifted(after4, 4)
    lane_grp = lax.broadcasted_iota(jnp.int32, (1, MIX_PART), 1) // GROUP_DIM
    sl = slice(8, 8 + tb)
    tot = jnp.where(lane_grp == 0, (before1 + ext)[sl],
                    jnp.where(lane_grp == 1, (before2 + after2)[sl],
                              jnp.where(lane_grp == 2, (before4 + after4)[sl], (before8 + after8)[sl])))
    half = jnp.where(lane_grp == 0, POOL_WINDOWS[0] // 2,
                     jnp.where(lane_grp == 1, POOL_WINDOWS[1] // 2,
                               jnp.where(lane_grp == 2, POOL_WINDOWS[2] // 2, POOL_WINDOWS[3] // 2)))
    pc = p[sl]
    cnt = jnp.minimum(pc + half, length[sl]) - jnp.maximum(pc - half, 0)
    dlt = tot / cnt.astype(F32) - ext[sl]
    o_ref[...] = (_dot(dlt.astype(BF16), w_ref[...]) * s_ref[...]).astype(BF16)


def _pool_mixer(n_lat, seq, ctx_len, pa, w_bd, s):
    n = pa.shape[0]
    tb = 1024
    assert seq & (seq - 1) == 0 and ctx_len & (ctx_len - 1) == 0
    assert n_lat % tb == 0 and n % tb == 0
    nb8 = n // 8
    return pl.pallas_call(
        functools.partial(_pool_body, n_lat, seq, ctx_len),
        grid=(n // tb,),
        in_specs=[pl.BlockSpec((8, MIX_PART), lambda i: (jnp.maximum(i * (tb // 8) - 1, 0), 0)),
                  pl.BlockSpec((tb, MIX_PART), lambda i: (i, 0)),
                  pl.BlockSpec((8, MIX_PART), lambda i: (jnp.minimum((i + 1) * (tb // 8), nb8 - 1), 0)),
                  pl.BlockSpec(w_bd.shape, lambda i: (0, 0)),
                  pl.BlockSpec(s.shape, lambda i: (0, 0))],
        out_specs=pl.BlockSpec((tb, MIX_PART), lambda i: (i, 0)),
        out_shape=jax.ShapeDtypeStruct((n, MIX_PART), BF16),
        compiler_params=_cparams("arbitrary"),
        name="pool_mixer",
    )(pa, pa, pa, w_bd, s)


def _softmax_pv(parts):
    m = parts[0][0].max(axis=-1, keepdims=True)
    for s, _ in parts[1:]:
        m = jnp.maximum(m, s.max(axis=-1, keepdims=True))
    den = 0.0
    out = 0.0
    for s, v in parts:
        p = jnp.exp(s - m)
        den = den + p.sum(axis=-1, keepdims=True)
        out = out + _dot(p.astype(BF16), v)
    return out / den


def _na_body(rows, n_qt, q_ref, kl_ref, kc_ref, vl_ref, vc_ref, bias_ref, o_ref):
    qt = pl.program_id(1)
    q = q_ref[...]
    kc = kc_ref[...]
    vc = vc_ref[...]
    lane_grp = lax.broadcasted_iota(jnp.int32, (1, MIX_PART), 1) // GROUP_DIM
    zero = jnp.zeros_like(q)

    @pl.when(qt < n_qt)
    def _():
        k0 = jnp.clip(NA_Q_ROWS * qt - NA_WIN_ROWS // 2, 0, rows - NA_K_ROWS)
        off = pl.multiple_of(k0 * GRID_W, NA_Q_ROWS * GRID_W)
        kw = kl_ref[pl.ds(off, NA_K_ROWS * GRID_W), :]
        vw = vl_ref[pl.ds(off, NA_K_ROWS * GRID_W), :]
        o = jnp.zeros(q.shape, F32)
        for h in range(N_GROUPS):
            qh = jnp.where(lane_grp == h, q, zero)
            oh = _softmax_pv([(_dot_nt(qh, kw) + bias_ref[0, h], vw), (_dot_nt(qh, kc), vc)])
            o = jnp.where(lane_grp == h, oh, o)
        o_ref[...] = o.astype(BF16)

    @pl.when(qt == n_qt)
    def _():
        o = jnp.zeros(q.shape, F32)
        for h in range(N_GROUPS):
            qh = jnp.where(lane_grp == h, q, zero)
            oh = _softmax_pv([(_dot_nt(qh, kc), vc)])
            o = jnp.where(lane_grp == h, oh, o)
        o_ref[...] = o.astype(BF16)


def _attn_specs(n_batch, seq, ctx_len, width_q, width_k, width_v, width_o):
    n_qt = seq // ATTN_TILE
    lat_blocks = n_batch * n_qt
    assert ctx_len == ATTN_TILE

    def qmap(b, t):
        return (jnp.where(t < n_qt, b * n_qt + t, lat_blocks + b), 0)

    q_spec = pl.BlockSpec((ATTN_TILE, width_q), qmap)
    kl_spec = pl.BlockSpec((seq, width_k), lambda b, t: (b, 0))
    kc_spec = pl.BlockSpec((ctx_len, width_k), lambda b, t: (lat_blocks + b, 0))
    vl_spec = pl.BlockSpec((seq, width_v), lambda b, t: (b, 0))
    vc_spec = pl.BlockSpec((ctx_len, width_v), lambda b, t: (lat_blocks + b, 0))
    o_spec = pl.BlockSpec((ATTN_TILE, width_o), qmap)
    return n_qt, [q_spec, kl_spec, kc_spec, vl_spec, vc_spec], o_spec


def _na_mixer(l, n_batch, seq, ctx_len, qb, kb, vb, bias_tab):
    n = qb.shape[0]
    rows = seq // GRID_W
    n_qt, in_specs, o_spec = _attn_specs(n_batch, seq, ctx_len, MIX_PART, MIX_PART, MIX_PART, MIX_PART)
    assert ATTN_TILE == NA_Q_ROWS * GRID_W and rows >= NA_K_ROWS + NA_Q_ROWS

    def bias_map(b, t):
        kind = jnp.where(t == 0, 0, jnp.where(t >= n_qt - 1, 2, 1))
        return (l * 3 + kind, 0, 0, 0)

    in_specs.append(pl.BlockSpec((1,) + bias_tab.shape[1:], bias_map))
    return pl.pallas_call(
        functools.partial(_na_body, rows, n_qt),
        grid=(n_batch, n_qt + 1),
        in_specs=in_specs,
        out_specs=o_spec,
        out_shape=jax.ShapeDtypeStruct((n, MIX_PART), BF16),
        compiler_params=_cparams("arbitrary", "arbitrary"),
        name="neighborhood_attention",
    )(qb, kb, kb, vb, vb, bias_tab)


def _na_bias_tables(na_rpb, rows):
    L, H = na_rpb.shape[:2]
    kinds = ((0, 0), (NA_Q_ROWS, 0), (rows - NA_Q_ROWS, rows - NA_K_ROWS))
    qc = np.arange(GRID_W)
    kc = np.arange(GRID_W)
    wsc = np.clip(qc - NA_WIN_COLS // 2, 0, GRID_W - NA_WIN_COLS)
    col_ok = (kc[None, :] >= wsc[:, None]) & (kc[None, :] < wsc[:, None] + NA_WIN_COLS)
    padded = _pad_last(na_rpb, GRID_W - NA_WIN_COLS, GRID_W - NA_WIN_COLS)
    shifted = jnp.stack([padded[..., GRID_W - 1 - c:2 * GRID_W - 1 - c] for c in range(GRID_W)], axis=2)
    slabs = jnp.where(col_ok[:, None, :], shifted, NEG_INF)

    def masked(n):
        return jnp.full((L, H, GRID_W, n, GRID_W), NEG_INF, F32)

    tabs = []
    for r0, k0 in kinds:
        per_row = []
        for qr in range(NA_Q_ROWS):
            r = r0 + qr
            first = int(np.clip(r - NA_WIN_ROWS // 2, 0, rows - NA_WIN_ROWS)) - k0
            d0 = k0 + first - r + NA_WIN_ROWS - 1
            assert 0 <= first <= NA_K_ROWS - NA_WIN_ROWS and 0 <= d0 <= NA_WIN_ROWS - 1
            per_row.append(jnp.concatenate(
                [masked(first), slabs[:, :, :, d0:d0 + NA_WIN_ROWS, :], masked(NA_K_ROWS - NA_WIN_ROWS - first)],
                axis=3))
        tabs.append(jnp.stack(per_row, axis=2))
    return jnp.stack(tabs, axis=1).reshape(L * 3, H, NA_Q_ROWS * GRID_W, NA_K_ROWS * GRID_W)


def _gelu_tanh(x):
    return 0.5 * x * (1.0 + jnp.tanh(np.sqrt(2.0 / np.pi).astype(np.float32) * (x + 0.044715 * (x * x * x))))


def _sgu_body(pc_ref, gn_ref, ones_ref, w_ref, b_ref, o_ref):
    tm = pc_ref.shape[0]
    uv = _gelu_tanh(pc_ref[...])
    u = uv[:, :MIX_PART]
    v = uv[:, MIX_PART:]
    ms = jnp.dot(v * v, ones_ref[...], precision=HIGHEST, preferred_element_type=F32)
    vg = (v * lax.rsqrt(ms + EPS) * gn_ref[...]).astype(BF16)
    lane_grp = lax.broadcasted_iota(jnp.int32, (1, MIX_PART), 1) // GROUP_DIM
    w = w_ref[...]
    for c in range(tm // SGU_CHUNK):
        rs = slice(c * SGU_CHUNK, (c + 1) * SGU_CHUNK)
        r = _dot(w, vg[rs])
        mixed = r[:SGU_CHUNK]
        for g in range(1, N_GROUPS):
            mixed = jnp.where(lane_grp == g, r[g * SGU_CHUNK:(g + 1) * SGU_CHUNK], mixed)
        o_ref[rs, :] = (u[rs] * (mixed + b_ref[...])).astype(BF16)


def _sgu_mixer(pc, gn, ones_bd, w_all, b_exp):
    n = pc.shape[0]
    tm = TOKEN_TILE
    const = lambda i: (0, 0)
    return pl.pallas_call(
        _sgu_body,
        grid=(n // tm,),
        in_specs=[pl.BlockSpec((tm, 2 * MIX_PART), lambda i: (i, 0)),
                  pl.BlockSpec(gn.shape, const), pl.BlockSpec(ones_bd.shape, const),
                  pl.BlockSpec(w_all.shape, const), pl.BlockSpec(b_exp.shape, const)],
        out_specs=pl.BlockSpec((tm, MIX_PART), lambda i: (i, 0)),
        out_shape=jax.ShapeDtypeStruct((n, MIX_PART), BF16),
        compiler_params=_cparams("arbitrary"),
        name="spatial_gating",
    )(pc, gn, ones_bd, w_all, b_exp)


def _mla_body(n_qt, n_kv, q_ref, kl_ref, kc_ref, vl_ref, vc_ref, o_ref, sa_ref, sb_ref):
    qt = pl.program_id(1)
    heads = [slice(h * MLA_SLOT, (h + 1) * MLA_SLOT) for h in range(N_GROUPS)]

    def scores(t, h, dst):
        rows = slice(t * MLA_KV_TILE, (t + 1) * MLA_KV_TILE)
        dst[h] = _dot_nt(kl_ref[rows, heads[h]], q_ref[:, heads[h]])

    def consume(t, h, src, m, acc):
        s = src[h]
        mn = jnp.maximum(m, s.max(axis=0, keepdims=True))
        p = jnp.exp2(s - mn).astype(BF16)
        return mn, jnp.exp2(m - mn) * acc + _dot(vl_ref[heads[h], t * MLA_KV_TILE:(t + 1) * MLA_KV_TILE], p)

    def attend(n_tiles):
        ctx_scores = [_dot_nt(kc_ref[:, hs], q_ref[:, hs]) for hs in heads]
        if n_tiles:
            for h in range(N_GROUPS):
                scores(0, h, sa_ref)
        state = []
        for hs, s in zip(heads, ctx_scores):
            m = s.max(axis=0, keepdims=True)
            state.append((m, _dot(vc_ref[hs, :], jnp.exp2(s - m).astype(BF16))))
        for t in range(n_tiles):
            cur, nxt = (sa_ref, sb_ref) if t % 2 == 0 else (sb_ref, sa_ref)
            out = []
            for h, (m, acc) in enumerate(state):
                if t + 1 < n_tiles:
                    scores(t + 1, h, nxt)
                out.append(consume(t, h, cur, m, acc))
            state = out
        out_t = jnp.concatenate([acc[:MLA_V] / acc[MLA_V:MLA_V + 1] for _, acc in state], axis=0)
        o_ref[...] = out_t.T.astype(BF16)

    @pl.when(qt < n_qt)
    def _():
        attend(n_kv)

    @pl.when(qt == n_qt)
    def _():
        attend(0)


def _mla_mixer(n_batch, seq, ctx_len, qd, kd, vdt):
    n = qd.shape[0]
    slots = N_GROUPS * MLA_SLOT
    n_qt, in_specs, o_spec = _attn_specs(n_batch, seq, ctx_len, slots, slots, slots, N_GROUPS * MLA_V)
    lat_blocks = n_batch * n_qt
    in_specs[3] = pl.BlockSpec((slots, seq), lambda b, t: (0, b))
    in_specs[4] = pl.BlockSpec((slots, ctx_len), lambda b, t: (0, lat_blocks + b))
    n_kv = seq // MLA_KV_TILE
    assert seq % MLA_KV_TILE == 0 and n_kv <= MLA_MAX_UNROLLED_TILES
    return pl.pallas_call(
        functools.partial(_mla_body, n_qt, n_kv),
        grid=(n_batch, n_qt + 1),
        in_specs=in_specs,
        out_specs=o_spec,
        out_shape=jax.ShapeDtypeStruct((n, N_GROUPS * MLA_V), BF16),
        scratch_shapes=[pltpu.VMEM((N_GROUPS, MLA_KV_TILE, ATTN_TILE), F32)] * 2,
        compiler_params=_cparams("arbitrary", "arbitrary"),
        name="latent_attention",
    )(qd, kd, kd, vdt, vdt)


def _outproj_body(ya_ref, yb_ref, yc_ref, yd_ref, x_ref, m_ref, g2_ref, w_ref, rw_ref, rb_ref,
                  xo_ref, hx_ref, r_ref):
    tm = x_ref.shape[0]
    m = m_ref[0]
    y = _dot(ya_ref[...], w_ref[0:MIX_PART, :])
    for k, ref in enumerate((yb_ref, yc_ref, yd_ref), start=1):
        y = y + _dot(ref[...], w_ref[k * MIX_PART:(k + 1) * MIX_PART, :])
    x = x_ref[...] + m[2:3] * y
    xo_ref[...] = x
    h = _rms(x, g2_ref[...]) * (1.0 + m[4:5]) + m[3:4]

    scores = jax.nn.sigmoid(_dot_nt(rw_ref[...], h, precision=HIGHEST))
    biased = scores + rb_ref[...]
    E = EXPERTS_PER_GROUP
    bk = [biased[8 * k:8 * k + 8] for k in range(E)]
    sk = [scores[8 * k:8 * k + 8] for k in range(E)]
    gs = None
    for a in range(E):
        for b in range(a + 1, E):
            pair = bk[a] + bk[b]
            gs = pair if gs is None else jnp.maximum(gs, pair)
    best = gs[0:1]
    idx = jnp.zeros((1, tm), jnp.int32)
    for g in range(1, N_EXPERT_GROUPS):
        better = gs[g:g + 1] > best
        idx = jnp.where(better, g, idx)
        best = jnp.where(better, gs[g:g + 1], best)
    in_grp = lax.broadcasted_iota(jnp.int32, (8, tm), 0) == idx
    wk = []
    for k in range(E):
        rank = jnp.zeros((8, tm), jnp.int32)
        for j in range(E):
            if j != k:
                ahead = (bk[j] > bk[k]) | ((bk[j] == bk[k]) & (j < k))
                rank = rank + ahead.astype(jnp.int32)
        wk.append(jnp.where((rank < 2) & in_grp, sk[k], 0.0).sum(axis=0, keepdims=True))
    den = wk[0] + wk[1] + wk[2] + wk[3]
    ri = lax.broadcasted_iota(jnp.int32, (8, tm), 0)
    out = jnp.where(ri == E, idx.astype(F32), 0.0)
    for k in range(E):
        out = jnp.where(ri == k, wk[k] / den, out)
    r_ref[...] = out

    for s in range(FEAT_ROWS):
        hx_ref[:, s, :] = h[:, s * LANES:(s + 1) * LANES]
    hx_ref[:, FEAT_ROWS, :] = jnp.concatenate([out, jnp.zeros((LANES - 8, tm), F32)], axis=0).T
    hx_ref[:, FEAT_ROWS + 1:, :] = jnp.zeros((tm, SLAB_ROWS - FEAT_ROWS - 1, LANES), F32)


def _out_projection(l, n_lat_tiles_per_batch, n_batch, ys, x, mod_r, g2, w, rw, rb):
    n, d = x.shape
    tm = TOKEN_TILE
    row = lambda i: (i, 0)
    const = lambda i: (0, 0)
    modrow = lambda i: (l * 8 + jnp.minimum(i // n_lat_tiles_per_batch, n_batch), 0, 0)
    in_specs = [pl.BlockSpec((tm, MIX_PART), row)] * 4 + [
        pl.BlockSpec((tm, d), row), pl.BlockSpec((1, N_MOD, d), modrow), pl.BlockSpec((1, d), const),
        pl.BlockSpec(w.shape, const), pl.BlockSpec(rw.shape, const), pl.BlockSpec(rb.shape, const)]
    return pl.pallas_call(
        _outproj_body,
        grid=(n // tm,),
        in_specs=in_specs,
        out_specs=[pl.BlockSpec((tm, d), row), pl.BlockSpec((tm, SLAB_ROWS, LANES), lambda i: (i, 0, 0)),
                   pl.BlockSpec((8, tm), lambda i: (0, i))],
        out_shape=[jax.ShapeDtypeStruct((n, d), F32), jax.ShapeDtypeStruct((n, SLAB_ROWS, LANES), F32),
                   jax.ShapeDtypeStruct((8, n), F32)],
        compiler_params=_cparams("arbitrary"),
        name="out_projection_routing",
    )(*ys, x, mod_r, g2, w, rw, rb)


def _moe_body(tg_ref, tv_ref, src_ref, dst_ref, hx_hbm, wg_ref, wu_ref, wd_ref, y_hbm,
              hbuf, ybuf, wgb, wub, wdb, gsem, ssem):
    i = pl.program_id(0)
    n_t = pl.num_programs(0)
    tm = hbuf.shape[1]
    slot = i % 2
    last = n_t - 1
    grp = tg_ref[i]
    prev_valid = tv_ref[jnp.maximum(i - 1, 0)] > 0

    def gather_copy(r, t, sl):
        return pltpu.make_async_copy(hx_hbm.at[src_ref[t * tm + r]], hbuf.at[sl, r], gsem.at[sl])

    def scatter_copy(r, t, sl):
        return pltpu.make_async_copy(ybuf.at[sl, r], y_hbm.at[dst_ref[t * tm + r]], ssem.at[sl])

    def start_all(copy, t, sl):
        def body(j, c):
            copy(2 * j, t, sl).start(priority=0)
            copy(2 * j + 1, t, sl).start(priority=1)
            return c
        lax.fori_loop(0, tm // 2, body, 0, unroll=4)

    def wait_gather(sl):
        pltpu.make_async_copy(hx_hbm.at[pl.ds(0, tm)], hbuf.at[sl], gsem.at[sl]).wait()

    def wait_scatter(sl):
        pltpu.make_async_copy(ybuf.at[sl], y_hbm.at[pl.ds(0, tm)], ssem.at[sl]).wait()

    @pl.when(i == 0)
    def _():
        start_all(gather_copy, 0, 0)

    @pl.when((i == 0) | (grp != tg_ref[jnp.maximum(i - 1, 0)]))
    def _():
        wgb[...] = wg_ref[0].astype(BF16)
        wub[...] = wu_ref[0].astype(BF16)
        wdb[...] = wd_ref[0].astype(BF16)

    @pl.when((i == 0) | prev_valid)
    def _():
        wait_gather(slot)

    @pl.when(i >= 2)
    def _():
        wait_scatter(slot)

    def expert_tile(with_scatter):
        nxt = jnp.minimum(i + 1, last)
        n_batches = 3 * EXPERTS_PER_GROUP
        s_bounds = [min(tm, (b * tm) // (n_batches - 4)) for b in range(n_batches + 1)]
        g_bounds = [min(tm, (b * tm) // (n_batches - 2)) for b in range(n_batches + 1)]

        def start_batch(b):
            if with_scatter:
                for r in range(s_bounds[b], s_bounds[b + 1]):
                    scatter_copy(r, i - 1, 1 - slot).start(priority=1)
            for r in range(g_bounds[b], g_bounds[b + 1]):
                gather_copy(r, nxt, 1 - slot).start(priority=0)

        h = jnp.concatenate([hbuf[slot, :, s, :] for s in range(FEAT_ROWS)], axis=1).astype(BF16)
        gates = hbuf[slot, :, FEAT_ROWS, :]
        acc = jnp.zeros((tm, FEAT_ROWS * LANES), F32)
        for k in range(EXPERTS_PER_GROUP):
            start_batch(3 * k)
            a = _dot(h, wgb[k])
            start_batch(3 * k + 1)
            u = _dot(h, wub[k])
            hid = _silu(a) * u * gates[:, k:k + 1]
            start_batch(3 * k + 2)
            acc = acc + _dot(hid.astype(BF16), wdb[k])
        for s in range(FEAT_ROWS):
            ybuf[slot, :, s, :] = acc[:, s * LANES:(s + 1) * LANES]

    valid = tv_ref[i] > 0

    @pl.when(valid & (i == 0))
    def _():
        expert_tile(False)

    @pl.when(valid & (i > 0))
    def _():
        expert_tile(True)

    @pl.when(jnp.logical_not(valid))
    def _():
        ybuf[slot] = jnp.zeros(ybuf.shape[1:], F32)

        @pl.when(i > 0)
        def _():
            start_all(scatter_copy, i - 1, 1 - slot)

    @pl.when(i == last)
    def _():
        start_all(scatter_copy, i, slot)
        wait_scatter(slot)

        @pl.when(i > 0)
        def _():
            wait_scatter(1 - slot)

        @pl.when(valid)
        def _():
            wait_gather(1 - slot)


def _grouped_experts(l, tile_grp, tile_ok, src, dst, hx, w_gate, w_up, w_down):
    n_pad = src.shape[0]
    tm = MOE_TILE
    E = EXPERTS_PER_GROUP
    d = FEAT_ROWS * LANES
    wmap = lambda i, tg, tv, sr, ds: (l, tg[i], 0, 0)
    grid_spec = pltpu.PrefetchScalarGridSpec(
        num_scalar_prefetch=4,
        grid=(n_pad // tm,),
        in_specs=[pl.BlockSpec(memory_space=pl.ANY),
                  pl.BlockSpec((1, E, d, D_EXPERT), wmap),
                  pl.BlockSpec((1, E, d, D_EXPERT), wmap),
                  pl.BlockSpec((1, E, D_EXPERT, d), wmap)],
        out_specs=pl.BlockSpec(memory_space=pl.ANY),
        scratch_shapes=[pltpu.VMEM((2, tm, SLAB_ROWS, LANES), F32), pltpu.VMEM((2, tm, FEAT_ROWS, LANES), F32),
                        pltpu.VMEM((E, d, D_EXPERT), BF16), pltpu.VMEM((E, d, D_EXPERT), BF16),
                        pltpu.VMEM((E, D_EXPERT, d), BF16),
                        pltpu.SemaphoreType.DMA((2,)), pltpu.SemaphoreType.DMA((2,))])
    return pl.pallas_call(
        _moe_body,
        grid_spec=grid_spec,
        out_shape=jax.ShapeDtypeStruct((n_pad, FEAT_ROWS, LANES), F32),
        compiler_params=_cparams("arbitrary"),
        name="grouped_experts",
    )(tile_grp, tile_ok, src, dst, hx, w_gate, w_up, w_down)


def _dispatch_plan(route, n_pad):
    n = route.shape[1]
    tm = MOE_TILE
    grp = route[EXPERTS_PER_GROUP].astype(jnp.int32)
    onehot = (grp[:, None] == jnp.arange(N_EXPERT_GROUPS)[None, :]).astype(jnp.int32)
    csum = jnp.cumsum(onehot, axis=0)
    counts = csum[-1]
    rank = jnp.sum(csum * onehot, axis=1) - 1
    padded = ((counts + tm - 1) // tm) * tm
    ends = jnp.cumsum(padded)
    pos = jnp.sum((ends - padded)[None, :] * onehot, axis=1) + rank
    owner = jnp.zeros((n_pad,), jnp.int32).at[pos].set(jnp.arange(1, n + 1, dtype=jnp.int32))
    used = owner > 0
    src = jnp.maximum(owner - 1, 0)
    dst = jnp.where(used, owner - 1, n - 1 + jnp.cumsum(1 - used.astype(jnp.int32)))
    tile_start = jnp.arange(n_pad // tm, dtype=jnp.int32) * tm
    tile_grp = jnp.minimum(jnp.sum(tile_start[:, None] >= ends[None, :], axis=1), N_EXPERT_GROUPS - 1)
    tile_ok = (tile_start < ends[-1]).astype(jnp.int32)
    return tile_grp.astype(jnp.int32), tile_ok, src, dst


def _slab_rows(ref):
    return jnp.concatenate([ref[:, s, :] for s in range(FEAT_ROWS)], axis=1)


def _final_body(x_ref, y_ref, m_ref, g_ref, o_ref):
    x = x_ref[...] + m_ref[0, 5:6, :] * _slab_rows(y_ref)
    o_ref[...] = _rms(x, g_ref[...])


def _final_norm(l, n_lat, n_lat_tiles_per_batch, x, y, mod_r, g):
    d = x.shape[1]
    tm = TOKEN_TILE
    row = lambda i: (i, 0)
    return pl.pallas_call(
        _final_body,
        grid=(n_lat // tm,),
        in_specs=[pl.BlockSpec((tm, d), row), pl.BlockSpec((tm, FEAT_ROWS, LANES), lambda i: (i, 0, 0)),
                  pl.BlockSpec((1, N_MOD, d), lambda i: (l * 8 + i // n_lat_tiles_per_batch, 0, 0)),
                  pl.BlockSpec((1, d), lambda i: (0, 0))],
        out_specs=pl.BlockSpec((tm, d), row),
        out_shape=jax.ShapeDtypeStruct((n_lat, d), F32),
        compiler_params=_cparams("arbitrary"),
        name="final_norm",
    )(x, y, mod_r, g)


def _rope_tables(seq, n_batch, n_ctx_tokens):
    t = np.arange(seq)
    nf = MLA_ROPE // 4
    inv = jnp.asarray(ROPE_BASE, F32) ** (-jnp.arange(nf, dtype=F32) / nf)
    rang = jnp.asarray(t // GRID_W, F32)[:, None] * inv
    cang = jnp.asarray(t % GRID_W, F32)[:, None] * inv
    cr, sr, cc, sc = jnp.cos(rang), jnp.sin(rang), jnp.cos(cang), jnp.sin(cang)
    one = jnp.ones((seq, MLA_NOPE), F32)
    zero_tail = jnp.zeros((seq, MLA_SLOT - MLA_NOPE - MLA_ROPE), F32)
    cosp = jnp.concatenate([one, cr, cr, cc, cc, zero_tail], axis=1)
    sinp = jnp.concatenate([0 * one, -sr, sr, -sc, sc, zero_tail], axis=1)
    ctx_cos = jnp.concatenate([jnp.ones((n_ctx_tokens, MLA_NOPE + MLA_ROPE), F32),
                               jnp.zeros((n_ctx_tokens, MLA_SLOT - MLA_NOPE - MLA_ROPE), F32)], axis=1)
    cosp = jnp.concatenate([jnp.tile(cosp, (n_batch, 1)), ctx_cos], axis=0)
    sinp = jnp.concatenate([jnp.tile(sinp, (n_batch, 1)), jnp.zeros((n_ctx_tokens, MLA_SLOT), F32)], axis=0)
    return cosp, sinp


_ROPE_SWAP = np.concatenate([np.arange(8, 16), np.arange(0, 8), np.arange(24, 32), np.arange(16, 24)])


def _pad_last(a, before, after):
    return jnp.pad(a, [(0, 0)] * (a.ndim - 1) + [(before, after)])


def _layouts(w_in, w_out, pool_w, sgu_w, sgu_b, mla_w_uq, mla_w_ukv, router_w, router_b):
    L = w_in.shape[0]
    tail = MLA_SLOT - MLA_NOPE - MLA_ROPE
    kr = w_in[:, :, OFF_KR:]
    w = jnp.concatenate([w_in[:, :, :OFF_KR], _pad_last(kr, MLA_NOPE, tail),
                         _pad_last(kr[:, :, _ROPE_SWAP], MLA_NOPE, tail)], axis=-1).astype(BF16)
    uq = mla_w_uq.reshape(L, MLA_Q_RANK, N_GROUPS, MLA_NOPE + MLA_ROPE)
    q1 = _pad_last(uq, 0, tail).reshape(L, MLA_Q_RANK, N_GROUPS * MLA_SLOT)
    q2 = _pad_last(uq[..., MLA_NOPE:][..., _ROPE_SWAP], MLA_NOPE, tail).reshape(L, MLA_Q_RANK, N_GROUPS * MLA_SLOT)
    wq = jnp.concatenate([q1, q2], axis=-1).astype(BF16)
    ukv = mla_w_ukv.reshape(L, MLA_KV_RANK, N_GROUPS, MLA_NOPE + MLA_V)
    kn = _pad_last(ukv[..., :MLA_NOPE], 0, MLA_SLOT - MLA_NOPE).reshape(L, MLA_KV_RANK, N_GROUPS * MLA_SLOT)
    vv = _pad_last(ukv[..., MLA_NOPE:], 0, MLA_SLOT - MLA_V).reshape(L, MLA_KV_RANK, N_GROUPS * MLA_SLOT)
    wkv = kn.astype(BF16)
    wvt = jnp.swapaxes(vv, 1, 2).astype(BF16)
    eye = jnp.eye(N_GROUPS, dtype=F32)
    pool_bd = jnp.einsum('lgcd,gh->lgchd', pool_w, eye).reshape(L, MIX_PART, MIX_PART).astype(BF16)
    ones_bd = jnp.asarray(np.kron(np.eye(N_GROUPS), np.full((GROUP_DIM, GROUP_DIM), 1.0 / GROUP_DIM)), F32)
    sgu_all = sgu_w.reshape(L, N_GROUPS * SGU_CHUNK, SGU_CHUNK).astype(BF16)
    sgu_bias = jnp.repeat(jnp.swapaxes(sgu_b, 1, 2), GROUP_DIM, axis=2)
    rw = router_w.T.reshape(N_EXPERT_GROUPS, EXPERTS_PER_GROUP, -1).transpose(1, 0, 2)
    rw = jnp.pad(rw, ((0, 0), (0, 8 - N_EXPERT_GROUPS), (0, 0))).reshape(8 * EXPERTS_PER_GROUP, -1)
    rb = router_b.reshape(N_EXPERT_GROUPS, EXPERTS_PER_GROUP).T
    rb = jnp.pad(rb, ((0, 0), (0, 8 - N_EXPERT_GROUPS))).reshape(8 * EXPERTS_PER_GROUP, 1)
    return w, wq, wkv, wvt, w_out.astype(BF16), pool_bd, ones_bd, sgu_all, sgu_bias, rw, rb


def kernel(x, c, ctx, c_ctx, ada_w, ada_b, norm1_g, norm2_g, w_in, w_out, pool_w, pool_s, na_rpb,
           sgu_norm_g, sgu_w, sgu_b, mla_q_norm_g, mla_w_uq, mla_kv_norm_g, mla_w_ukv,
           router_w, router_b, moe_w_gate, moe_w_up, moe_w_down, final_g):
    B, S, D = x.shape
    CTX = ctx.shape[1]
    L = ada_w.shape[0]
    n_lat, n_ctx = B * S, B * CTX
    n_all = n_lat + n_ctx
    assert B < 8 and S % TOKEN_TILE == 0 and n_ctx % TOKEN_TILE == 0
    tiles_per_batch = S // TOKEN_TILE
    n_pad = n_all + N_EXPERT_GROUPS * MOE_TILE

    c_all = jnp.concatenate([c, c_ctx[None, :], jnp.zeros((8 - B - 1, D), F32)], axis=0)
    mod_r = _modulation(c_all, ada_w, ada_b).reshape(L * 8, N_MOD, D)
    w, wq, wkv, wvt, wo, pool_bd, ones_bd, sgu_all, sgu_bias, rw, rb = _layouts(
        w_in, w_out, pool_w, sgu_w, sgu_b, mla_w_uq, mla_w_ukv, router_w, router_b)
    bias_tab = _na_bias_tables(na_rpb, S // GRID_W)
    cosp, sinp = _rope_tables(S, B, n_ctx)

    xs = jnp.concatenate([x.reshape(n_lat, D), ctx.reshape(n_ctx, D)], axis=0)
    y = None
    for l in range(L):
        xs, pa, qb, kb, vb, pc, qd, kd, vd = _in_projection(
            l, tiles_per_batch, B, xs, y, mod_r, norm1_g[l][None], w[l], wq[l], mla_q_norm_g[l][None],
            wkv[l], wvt[l], mla_kv_norm_g[l][None], cosp, sinp)
        ya = _pool_mixer(n_lat, S, CTX, pa, pool_bd[l], pool_s[l][None])
        yb = _na_mixer(l, B, S, CTX, qb, kb, vb, bias_tab)
        yc = _sgu_mixer(pc, sgu_norm_g[l][None], ones_bd, sgu_all[l], sgu_bias[l])
        yd = _mla_mixer(B, S, CTX, qd, kd, vd)
        xs, hx, route = _out_projection(l, tiles_per_batch, B, (ya, yb, yc, yd), xs, mod_r, norm2_g[l][None],
                                        wo[l], rw, rb)
        tile_grp, tile_ok, src, dst = _dispatch_plan(route, n_pad)
        y = _grouped_experts(l, tile_grp, tile_ok, src, dst, hx, moe_w_gate, moe_w_up, moe_w_down)
    out = _final_norm(L - 1, n_lat, tiles_per_batch, xs, y, mod_r, final_g[None])
    return out.reshape(B, S, D)
```

```python
import functools

import numpy as np
import jax
import jax.numpy as jnp
from jax import lax
from jax.experimental import pallas as pl
from jax.experimental.pallas import tpu as pltpu

F32 = jnp.float32
BF16 = jnp.bfloat16
HIGHEST = lax.Precision.HIGHEST

GRID_W = 64
POOL_WINDOWS = (2, 4, 8, 16)
GROUP_DIM = 64
N_GROUPS = 4
MIX_PART = N_GROUPS * GROUP_DIM
NA_WIN_ROWS = 8
NA_WIN_COLS = 16
NA_Q_ROWS = 4
NA_K_ROWS = NA_Q_ROWS + NA_WIN_ROWS
SGU_CHUNK = 128
MLA_Q_RANK = 256
MLA_KV_RANK = 128
MLA_NOPE = 64
MLA_ROPE = 32
MLA_V = 64
MLA_SLOT = 128
MLA_SCALE = (MLA_NOPE + MLA_ROPE) ** -0.5
LOG2_E = float(np.log2(np.e))
ROPE_BASE = 10000.0
N_EXPERTS = 16
N_EXPERT_GROUPS = 4
EXPERTS_PER_GROUP = 4
D_EXPERT = 256
N_MOD = 6
EPS = 1e-6
NEG_INF = -1e30

OFF_B = MIX_PART
OFF_C = OFF_B + 3 * MIX_PART
OFF_D = OFF_C + 2 * MIX_PART
OFF_KR = OFF_D + MLA_Q_RANK + MLA_KV_RANK
W_IN_COLS = OFF_KR + 2 * MLA_SLOT

LANES = 128
FEAT_ROWS = 8
SLAB_ROWS = 16

TOKEN_TILE = 512
ATTN_TILE = 256
MLA_KV_TILE = 512
MLA_MAX_UNROLLED_TILES = 16
MOE_TILE = 512
VMEM_LIMIT = 56 * 1024 * 1024


def _cparams(*sem):
    return pltpu.CompilerParams(dimension_semantics=sem, vmem_limit_bytes=VMEM_LIMIT)


def _dot(a, b):
    return jnp.dot(a, b, preferred_element_type=F32)


def _dot_nt(a, b, precision=None):
    return lax.dot_general(a, b, (((1,), (1,)), ((), ())), precision=precision,
                           preferred_element_type=F32)


def _rms(x, g):
    return x * lax.rsqrt(jnp.mean(x * x, axis=-1, keepdims=True) + EPS) * g


def _silu(x):
    return x * jax.nn.sigmoid(x)


def _mod_body(c_ref, w_ref, b_ref, o_ref):
    o_ref[0] = jnp.dot(_silu(c_ref[...]), w_ref[0], precision=HIGHEST,
                       preferred_element_type=F32) + b_ref[0]


def _modulation(c_all, ada_w, ada_b):
    L, D, ND = ada_w.shape
    tn = 1536
    return pl.pallas_call(
        _mod_body,
        grid=(L, ND // tn),
        in_specs=[pl.BlockSpec((8, D), lambda l, j: (0, 0)),
                  pl.BlockSpec((1, D, tn), lambda l, j: (l, 0, j)),
                  pl.BlockSpec((1, 1, tn), lambda l, j: (l, 0, j))],
        out_specs=pl.BlockSpec((1, 8, tn), lambda l, j: (l, 0, j)),
        out_shape=jax.ShapeDtypeStruct((L, 8, ND), F32),
        compiler_params=_cparams("arbitrary", "arbitrary"),
        name="modulation",
    )(c_all, ada_w, ada_b.reshape(L, 1, ND))


def _inproj_body(has_res, n_lat_tiles, *refs):
    if has_res:
        (x_ref, y_ref, mp_ref, m_ref, g1_ref, w_ref, wq_ref, gq_ref, wkv_ref, wvt_ref, gkv_ref, cos_ref, sin_ref,
         xo_ref, pa_ref, qb_ref, kb_ref, vb_ref, pc_ref, qd_ref, kd_ref, vd_ref) = refs
        x = x_ref[...] + mp_ref[0, 5:6, :] * _slab_rows(y_ref)
    else:
        (xl_ref, xc_ref, m_ref, g1_ref, w_ref, wq_ref, gq_ref, wkv_ref, wvt_ref, gkv_ref, cos_ref, sin_ref,
         xo_ref, pa_ref, qb_ref, kb_ref, vb_ref, pc_ref, qd_ref, kd_ref, vd_ref) = refs
        x = jnp.where(pl.program_id(0) < n_lat_tiles, xl_ref[...], xc_ref[...])
    xo_ref[...] = x
    m = m_ref[0]
    hb = (_rms(x, g1_ref[...]) * (1.0 + m[1:2]) + m[0:1]).astype(BF16)

    def proj(a, b):
        return _dot(hb, w_ref[:, a:b])

    cq = proj(OFF_D, OFF_D + MLA_Q_RANK)
    ckv = proj(OFF_D + MLA_Q_RANK, OFF_KR)
    kr_plain = proj(OFF_KR, OFF_KR + MLA_SLOT)
    kr_swapped = proj(OFF_KR + MLA_SLOT, W_IN_COLS)

    pa_ref[...] = proj(0, OFF_B)
    qb_ref[...] = (proj(OFF_B, OFF_B + MIX_PART) * (GROUP_DIM ** -0.5)).astype(BF16)
    kb_ref[...] = proj(OFF_B + MIX_PART, OFF_B + 2 * MIX_PART).astype(BF16)
    vb_ref[...] = proj(OFF_B + 2 * MIX_PART, OFF_C).astype(BF16)
    pc_ref[...] = proj(OFF_C, OFF_D)

    cosp = cos_ref[...]
    sinp = sin_ref[...]
    qn = _rms(cq, gq_ref[...]).astype(BF16)
    qq = _dot(qn, wq_ref[...])
    half = N_GROUPS * MLA_SLOT
    for h in range(N_GROUPS):
        a = h * MLA_SLOT
        q = qq[:, a:a + MLA_SLOT] * cosp + qq[:, half + a:half + a + MLA_SLOT] * sinp
        qd_ref[:, a:a + MLA_SLOT] = (q * (MLA_SCALE * LOG2_E)).astype(BF16)
    kvn = _rms(ckv, gkv_ref[...]).astype(BF16)
    kk = _dot(kvn, wkv_ref[...])
    kr = kr_plain * cosp + kr_swapped * sinp
    for h in range(N_GROUPS):
        a = h * MLA_SLOT
        kd_ref[:, a:a + MLA_SLOT] = (kk[:, a:a + MLA_SLOT] + kr).astype(BF16)
    ones_row = (lax.broadcasted_iota(jnp.int32, (half, 1), 0) % MLA_SLOT == MLA_V).astype(F32)
    vd_ref[...] = (_dot_nt(wvt_ref[...], kvn) + ones_row).astype(BF16)


def _in_projection(l, n_lat_tiles_per_batch, n_batch, x, y, mod_r, g1, w, wq, gq, wkv, wvt, gkv, cosp, sinp):
    tm = TOKEN_TILE
    has_res = y is not None
    n_lat_tiles = n_lat_tiles_per_batch * n_batch
    if has_res:
        n, d = x.shape
    else:
        n, d = x[0].shape[0] + x[1].shape[0], x[0].shape[1]
        assert x[0].shape[0] == n_lat_tiles * tm

    def row(i):
        return (i, 0)

    def modrow(layer):
        return lambda i: (layer * 8 + jnp.minimum(i // n_lat_tiles_per_batch, n_batch), 0, 0)

    def const2(i):
        return (0, 0)

    tok = lambda c: pl.BlockSpec((tm, c), row)
    mod_spec = lambda layer: pl.BlockSpec((1, N_MOD, d), modrow(layer))
    if has_res:
        in_specs = [tok(d), pl.BlockSpec((tm, FEAT_ROWS, LANES), lambda i: (i, 0, 0)), mod_spec(l - 1)]
        args = [x, y, mod_r]
    else:
        in_specs = [pl.BlockSpec((tm, d), lambda i: (jnp.minimum(i, n_lat_tiles - 1), 0)),
                    pl.BlockSpec((tm, d), lambda i: (jnp.maximum(i - n_lat_tiles, 0), 0))]
        args = list(x)
    in_specs += [mod_spec(l), pl.BlockSpec((1, d), const2), pl.BlockSpec(w.shape, const2),
                 pl.BlockSpec(wq.shape, const2), pl.BlockSpec(gq.shape, const2),
                 pl.BlockSpec(wkv.shape, const2), pl.BlockSpec(wvt.shape, const2), pl.BlockSpec(gkv.shape, const2),
                 tok(MLA_SLOT), tok(MLA_SLOT)]
    args += [mod_r, g1, w, wq, gq, wkv, wvt, gkv, cosp, sinp]
    outs = [(d, F32), (MIX_PART, F32), (MIX_PART, BF16), (MIX_PART, BF16), (MIX_PART, BF16), (2 * MIX_PART, F32),
            (N_GROUPS * MLA_SLOT, BF16), (N_GROUPS * MLA_SLOT, BF16)]
    slots = N_GROUPS * MLA_SLOT
    return pl.pallas_call(
        functools.partial(_inproj_body, has_res, n_lat_tiles),
        grid=(n // tm,),
        in_specs=in_specs,
        out_specs=[tok(c) for c, _ in outs] + [pl.BlockSpec((slots, tm), lambda i: (0, i))],
        out_shape=[jax.ShapeDtypeStruct((n, c), t) for c, t in outs] + [jax.ShapeDtypeStruct((slots, n), BF16)],
        compiler_params=_cparams("arbitrary"),
        name="in_projection",
    )(*args)


def _pool_body(n_lat, seq, ctx_len, prev_ref, cur_ref, next_ref, w_ref, s_ref, o_ref):
    tb = cur_ref.shape[0]
    ext = jnp.concatenate([prev_ref[...], cur_ref[...], next_ref[...]], axis=0)
    n = tb + 16
    g = pl.program_id(0) * tb - 8 + lax.broadcasted_iota(jnp.int32, (n, 1), 0)
    is_lat = g < n_lat
    length = jnp.where(is_lat, seq, ctx_len)
    p = jnp.where(is_lat, g & (seq - 1), (g - n_lat) & (ctx_len - 1))

    def shifted(a, j):
        r = pltpu.roll(a, (-j) % n, axis=0)
        ok = (p + j >= 0) & (p + j < length)
        return jnp.where(ok, r, 0.0)

    before1 = shifted(ext, -1)
    before2 = before1 + shifted(before1, -1)
    before4 = before2 + shifted(before2, -2)
    before8 = before4 + shifted(before4, -4)
    after2 = ext + shifted(ext, 1)
    after4 = after2 + shifted(after2, 2)
    after8 = after4 + shifted(after4, 4)
    lane_grp = lax.broadcasted_iota(jnp.int32, (1, MIX_PART), 1) // GROUP_DIM
    sl = slice(8, 8 + tb)
    tot = jnp.where(lane_grp == 0, (before1 + ext)[sl],
                    jnp.where(lane_grp == 1, (before2 + after2)[sl],
                              jnp.where(lane_grp == 2, (before4 + after4)[sl], (before8 + after8)[sl])))
    half = jnp.where(lane_grp == 0, POOL_WINDOWS[0] // 2,
                     jnp.where(lane_grp == 1, POOL_WINDOWS[1] // 2,
                               jnp.where(lane_grp == 2, POOL_WINDOWS[2] // 2, POOL_WINDOWS[3] // 2)))
    pc = p[sl]
    cnt = jnp.minimum(pc + half, length[sl]) - jnp.maximum(pc - half, 0)
    dlt = tot / cnt.astype(F32) - ext[sl]
    o_ref[...] = (_dot(dlt.astype(BF16), w_ref[...]) * s_ref[...]).astype(BF16)


def _pool_mixer(n_lat, seq, ctx_len, pa, w_bd, s):
    n = pa.shape[0]
    tb = 1024
    assert seq & (seq - 1) == 0 and ctx_len & (ctx_len - 1) == 0
    assert n_lat % tb == 0 and n % tb == 0
    nb8 = n // 8
    return pl.pallas_call(
        functools.partial(_pool_body, n_lat, seq, ctx_len),
        grid=(n // tb,),
        in_specs=[pl.BlockSpec((8, MIX_PART), lambda i: (jnp.maximum(i * (tb // 8) - 1, 0), 0)),
                  pl.BlockSpec((tb, MIX_PART), lambda i: (i, 0)),
                  pl.BlockSpec((8, MIX_PART), lambda i: (jnp.minimum((i + 1) * (tb // 8), nb8 - 1), 0)),
                  pl.BlockSpec(w_bd.shape, lambda i: (0, 0)),
                  pl.BlockSpec(s.shape, lambda i: (0, 0))],
        out_specs=pl.BlockSpec((tb, MIX_PART), lambda i: (i, 0)),
        out_shape=jax.ShapeDtypeStruct((n, MIX_PART), BF16),
        compiler_params=_cparams("arbitrary"),
        name="pool_mixer",
    )(pa, pa, pa, w_bd, s)


def _softmax_pv(parts):
    m = parts[0][0].max(axis=-1, keepdims=True)
    for s, _ in parts[1:]:
        m = jnp.maximum(m, s.max(axis=-1, keepdims=True))
    den = 0.0
    out = 0.0
    for s, v in parts:
        p = jnp.exp(s - m)
        den = den + p.sum(axis=-1, keepdims=True)
        out = out + _dot(p.astype(BF16), v)
    return out / den


def _na_body(rows, n_qt, q_ref, kl_ref, kc_ref, vl_ref, vc_ref, bias_ref, o_ref):
    qt = pl.program_id(1)
    q = q_ref[...]
    kc = kc_ref[...]
    vc = vc_ref[...]
    lane_grp = lax.broadcasted_iota(jnp.int32, (1, MIX_PART), 1) // GROUP_DIM
    zero = jnp.zeros_like(q)

    @pl.when(qt < n_qt)
    def _():
        k0 = jnp.clip(NA_Q_ROWS * qt - NA_WIN_ROWS // 2, 0, rows - NA_K_ROWS)
        off = pl.multiple_of(k0 * GRID_W, NA_Q_ROWS * GRID_W)
        kw = kl_ref[pl.ds(off, NA_K_ROWS * GRID_W), :]
        vw = vl_ref[pl.ds(off, NA_K_ROWS * GRID_W), :]
        o = jnp.zeros(q.shape, F32)
        for h in range(N_GROUPS):
            qh = jnp.where(lane_grp == h, q, zero)
            oh = _softmax_pv([(_dot_nt(qh, kw) + bias_ref[0, h], vw), (_dot_nt(qh, kc), vc)])
            o = jnp.where(lane_grp == h, oh, o)
        o_ref[...] = o.astype(BF16)

    @pl.when(qt == n_qt)
    def _():
        o = jnp.zeros(q.shape, F32)
        for h in range(N_GROUPS):
            qh = jnp.where(lane_grp == h, q, zero)
            oh = _softmax_pv([(_dot_nt(qh, kc), vc)])
            o = jnp.where(lane_grp == h, oh, o)
        o_ref[...] = o.astype(BF16)


def _attn_specs(n_batch, seq, ctx_len, width_q, width_k, width_v, width_o):
    n_qt = seq // ATTN_TILE
    lat_blocks = n_batch * n_qt
    assert ctx_len == ATTN_TILE

    def qmap(b, t):
        return (jnp.where(t < n_qt, b * n_qt + t, lat_blocks + b), 0)

    q_spec = pl.BlockSpec((ATTN_TILE, width_q), qmap)
    kl_spec = pl.BlockSpec((seq, width_k), lambda b, t: (b, 0))
    kc_spec = pl.BlockSpec((ctx_len, width_k), lambda b, t: (lat_blocks + b, 0))
    vl_spec = pl.BlockSpec((seq, width_v), lambda b, t: (b, 0))
    vc_spec = pl.BlockSpec((ctx_len, width_v), lambda b, t: (lat_blocks + b, 0))
    o_spec = pl.BlockSpec((ATTN_TILE, width_o), qmap)
    return n_qt, [q_spec, kl_spec, kc_spec, vl_spec, vc_spec], o_spec


def _na_mixer(l, n_batch, seq, ctx_len, qb, kb, vb, bias_tab):
    n = qb.shape[0]
    rows = seq // GRID_W
    n_qt, in_specs, o_spec = _attn_specs(n_batch, seq, ctx_len, MIX_PART, MIX_PART, MIX_PART, MIX_PART)
    assert ATTN_TILE == NA_Q_ROWS * GRID_W and rows >= NA_K_ROWS + NA_Q_ROWS

    def bias_map(b, t):
        kind = jnp.where(t == 0, 0, jnp.where(t >= n_qt - 1, 2, 1))
        return (l * 3 + kind, 0, 0, 0)

    in_specs.append(pl.BlockSpec((1,) + bias_tab.shape[1:], bias_map))
    return pl.pallas_call(
        functools.partial(_na_body, rows, n_qt),
        grid=(n_batch, n_qt + 1),
        in_specs=in_specs,
        out_specs=o_spec,
        out_shape=jax.ShapeDtypeStruct((n, MIX_PART), BF16),
        compiler_params=_cparams("arbitrary", "arbitrary"),
        name="neighborhood_attention",
    )(qb, kb, kb, vb, vb, bias_tab)


def _na_bias_tables(na_rpb, rows):
    L, H = na_rpb.shape[:2]
    kinds = ((0, 0), (NA_Q_ROWS, 0), (rows - NA_Q_ROWS, rows - NA_K_ROWS))
    qc = np.arange(GRID_W)
    kc = np.arange(GRID_W)
    wsc = np.clip(qc - NA_WIN_COLS // 2, 0, GRID_W - NA_WIN_COLS)
    col_ok = (kc[None, :] >= wsc[:, None]) & (kc[None, :] < wsc[:, None] + NA_WIN_COLS)
    padded = _pad_last(na_rpb, GRID_W - NA_WIN_COLS, GRID_W - NA_WIN_COLS)
    shifted = jnp.stack([padded[..., GRID_W - 1 - c:2 * GRID_W - 1 - c] for c in range(GRID_W)], axis=2)
    slabs = jnp.where(col_ok[:, None, :], shifted, NEG_INF)

    def masked(n):
        return jnp.full((L, H, GRID_W, n, GRID_W), NEG_INF, F32)

    tabs = []
    for r0, k0 in kinds:
        per_row = []
        for qr in range(NA_Q_ROWS):
            r = r0 + qr
            first = int(np.clip(r - NA_WIN_ROWS // 2, 0, rows - NA_WIN_ROWS)) - k0
            d0 = k0 + first - r + NA_WIN_ROWS - 1
            assert 0 <= first <= NA_K_ROWS - NA_WIN_ROWS and 0 <= d0 <= NA_WIN_ROWS - 1
            per_row.append(jnp.concatenate(
                [masked(first), slabs[:, :, :, d0:d0 + NA_WIN_ROWS, :], masked(NA_K_ROWS - NA_WIN_ROWS - first)],
                axis=3))
        tabs.append(jnp.stack(per_row, axis=2))
    return jnp.stack(tabs, axis=1).reshape(L * 3, H, NA_Q_ROWS * GRID_W, NA_K_ROWS * GRID_W)


def _gelu_tanh(x):
    return 0.5 * x * (1.0 + jnp.tanh(np.sqrt(2.0 / np.pi).astype(np.float32) * (x + 0.044715 * (x * x * x))))


def _sgu_body(pc_ref, gn_ref, ones_ref, w_ref, b_ref, o_ref):
    tm = pc_ref.shape[0]
    uv = _gelu_tanh(pc_ref[...])
    u = uv[:, :MIX_PART]
    v = uv[:, MIX_PART:]
    ms = jnp.dot(v * v, ones_ref[...], precision=HIGHEST, preferred_element_type=F32)
    vg = (v * lax.rsqrt(ms + EPS) * gn_ref[...]).astype(BF16)
    lane_grp = lax.broadcasted_iota(jnp.int32, (1, MIX_PART), 1) // GROUP_DIM
    w = w_ref[...]
    for c in range(tm // SGU_CHUNK):
        rs = slice(c * SGU_CHUNK, (c + 1) * SGU_CHUNK)
        r = _dot(w, vg[rs])
        mixed = r[:SGU_CHUNK]
        for g in range(1, N_GROUPS):
            mixed = jnp.where(lane_grp == g, r[g * SGU_CHUNK:(g + 1) * SGU_CHUNK], mixed)
        o_ref[rs, :] = (u[rs] * (mixed + b_ref[...])).astype(BF16)


def _sgu_mixer(pc, gn, ones_bd, w_all, b_exp):
    n = pc.shape[0]
    tm = TOKEN_TILE
    const = lambda i: (0, 0)
    return pl.pallas_call(
        _sgu_body,
        grid=(n // tm,),
        in_specs=[pl.BlockSpec((tm, 2 * MIX_PART), lambda i: (i, 0)),
                  pl.BlockSpec(gn.shape, const), pl.BlockSpec(ones_bd.shape, const),
                  pl.BlockSpec(w_all.shape, const), pl.BlockSpec(b_exp.shape, const)],
        out_specs=pl.BlockSpec((tm, MIX_PART), lambda i: (i, 0)),
        out_shape=jax.ShapeDtypeStruct((n, MIX_PART), BF16),
        compiler_params=_cparams("arbitrary"),
        name="spatial_gating",
    )(pc, gn, ones_bd, w_all, b_exp)


def _mla_body(n_qt, n_kv, q_ref, kl_ref, kc_ref, vl_ref, vc_ref, o_ref, sa_ref, sb_ref):
    qt = pl.program_id(1)
    heads = [slice(h * MLA_SLOT, (h + 1) * MLA_SLOT) for h in range(N_GROUPS)]

    def scores(t, h, dst):
        rows = slice(t * MLA_KV_TILE, (t + 1) * MLA_KV_TILE)
        dst[h] = _dot_nt(kl_ref[rows, heads[h]], q_ref[:, heads[h]])

    def consume(t, h, src, m, acc):
        s = src[h]
        mn = jnp.maximum(m, s.max(axis=0, keepdims=True))
        p = jnp.exp2(s - mn).astype(BF16)
        return mn, jnp.exp2(m - mn) * acc + _dot(vl_ref[heads[h], t * MLA_KV_TILE:(t + 1) * MLA_KV_TILE], p)

    def attend(n_tiles):
        ctx_scores = [_dot_nt(kc_ref[:, hs], q_ref[:, hs]) for hs in heads]
        if n_tiles:
            for h in range(N_GROUPS):
                scores(0, h, sa_ref)
        state = []
        for hs, s in zip(heads, ctx_scores):
            m = s.max(axis=0, keepdims=True)
            state.append((m, _dot(vc_ref[hs, :], jnp.exp2(s - m).astype(BF16))))
        for t in range(n_tiles):
            cur, nxt = (sa_ref, sb_ref) if t % 2 == 0 else (sb_ref, sa_ref)
            out = []
            for h, (m, acc) in enumerate(state):
                if t + 1 < n_tiles:
                    scores(t + 1, h, nxt)
                out.append(consume(t, h, cur, m, acc))
            state = out
        out_t = jnp.concatenate([acc[:MLA_V] / acc[MLA_V:MLA_V + 1] for _, acc in state], axis=0)
        o_ref[...] = out_t.T.astype(BF16)

    @pl.when(qt < n_qt)
    def _():
        attend(n_kv)

    @pl.when(qt == n_qt)
    def _():
        attend(0)


def _mla_mixer(n_batch, seq, ctx_len, qd, kd, vdt):
    n = qd.shape[0]
    slots = N_GROUPS * MLA_SLOT
    n_qt, in_specs, o_spec = _attn_specs(n_batch, seq, ctx_len, slots, slots, slots, N_GROUPS * MLA_V)
    lat_blocks = n_batch * n_qt
    in_specs[3] = pl.BlockSpec((slots, seq), lambda b, t: (0, b))
    in_specs[4] = pl.BlockSpec((slots, ctx_len), lambda b, t: (0, lat_blocks + b))
    n_kv = seq // MLA_KV_TILE
    assert seq % MLA_KV_TILE == 0 and n_kv <= MLA_MAX_UNROLLED_TILES
    return pl.pallas_call(
        functools.partial(_mla_body, n_qt, n_kv),
        grid=(n_batch, n_qt + 1),
        in_specs=in_specs,
        out_specs=o_spec,
        out_shape=jax.ShapeDtypeStruct((n, N_GROUPS * MLA_V), BF16),
        scratch_shapes=[pltpu.VMEM((N_GROUPS, MLA_KV_TILE, ATTN_TILE), F32)] * 2,
        compiler_params=_cparams("arbitrary", "arbitrary"),
        name="latent_attention",
    )(qd, kd, kd, vdt, vdt)


def _outproj_body(ya_ref, yb_ref, yc_ref, yd_ref, x_ref, m_ref, g2_ref, w_ref, rw_ref, rb_ref,
                  xo_ref, hx_ref, r_ref):
    tm = x_ref.shape[0]
    m = m_ref[0]
    y = _dot(ya_ref[...], w_ref[0:MIX_PART, :])
    for k, ref in enumerate((yb_ref, yc_ref, yd_ref), start=1):
        y = y + _dot(ref[...], w_ref[k * MIX_PART:(k + 1) * MIX_PART, :])
    x = x_ref[...] + m[2:3] * y
    xo_ref[...] = x
    h = _rms(x, g2_ref[...]) * (1.0 + m[4:5]) + m[3:4]

    scores = jax.nn.sigmoid(_dot_nt(rw_ref[...], h, precision=HIGHEST))
    biased = scores + rb_ref[...]
    E = EXPERTS_PER_GROUP
    bk = [biased[8 * k:8 * k + 8] for k in range(E)]
    sk = [scores[8 * k:8 * k + 8] for k in range(E)]
    gs = None
    for a in range(E):
        for b in range(a + 1, E):
            pair = bk[a] + bk[b]
            gs = pair if gs is None else jnp.maximum(gs, pair)
    best = gs[0:1]
    idx = jnp.zeros((1, tm), jnp.int32)
    for g in range(1, N_EXPERT_GROUPS):
        better = gs[g:g + 1] > best
        idx = jnp.where(better, g, idx)
        best = jnp.where(better, gs[g:g + 1], best)
    in_grp = lax.broadcasted_iota(jnp.int32, (8, tm), 0) == idx
    wk = []
    for k in range(E):
        rank = jnp.zeros((8, tm), jnp.int32)
        for j in range(E):
            if j != k:
                ahead = (bk[j] > bk[k]) | ((bk[j] == bk[k]) & (j < k))
                rank = rank + ahead.astype(jnp.int32)
        wk.append(jnp.where((rank < 2) & in_grp, sk[k], 0.0).sum(axis=0, keepdims=True))
    den = wk[0] + wk[1] + wk[2] + wk[3]
    ri = lax.broadcasted_iota(jnp.int32, (8, tm), 0)
    out = jnp.where(ri == E, idx.astype(F32), 0.0)
    for k in range(E):
        out = jnp.where(ri == k, wk[k] / den, out)
    r_ref[...] = out

    for s in range(FEAT_ROWS):
        hx_ref[:, s, :] = h[:, s * LANES:(s + 1) * LANES]
    hx_ref[:, FEAT_ROWS, :] = jnp.concatenate([out, jnp.zeros((LANES - 8, tm), F32)], axis=0).T
    hx_ref[:, FEAT_ROWS + 1:, :] = jnp.zeros((tm, SLAB_ROWS - FEAT_ROWS - 1, LANES), F32)


def _out_projection(l, n_lat_tiles_per_batch, n_batch, ys, x, mod_r, g2, w, rw, rb):
    n, d = x.shape
    tm = TOKEN_TILE
    row = lambda i: (i, 0)
    const = lambda i: (0, 0)
    modrow = lambda i: (l * 8 + jnp.minimum(i // n_lat_tiles_per_batch, n_batch), 0, 0)
    in_specs = [pl.BlockSpec((tm, MIX_PART), row)] * 4 + [
        pl.BlockSpec((tm, d), row), pl.BlockSpec((1, N_MOD, d), modrow), pl.BlockSpec((1, d), const),
        pl.BlockSpec(w.shape, const), pl.BlockSpec(rw.shape, const), pl.BlockSpec(rb.shape, const)]
    return pl.pallas_call(
        _outproj_body,
        grid=(n // tm,),
        in_specs=in_specs,
        out_specs=[pl.BlockSpec((tm, d), row), pl.BlockSpec((tm, SLAB_ROWS, LANES), lambda i: (i, 0, 0)),
                   pl.BlockSpec((8, tm), lambda i: (0, i))],
        out_shape=[jax.ShapeDtypeStruct((n, d), F32), jax.ShapeDtypeStruct((n, SLAB_ROWS, LANES), F32),
                   jax.ShapeDtypeStruct((8, n), F32)],
        compiler_params=_cparams("arbitrary"),
        name="out_projection_routing",
    )(*ys, x, mod_r, g2, w, rw, rb)


def _moe_body(tg_ref, tv_ref, src_ref, dst_ref, hx_hbm, wg_ref, wu_ref, wd_ref, y_hbm,
              hbuf, ybuf, wgb, wub, wdb, gsem, ssem):
    i = pl.program_id(0)
    n_t = pl.num_programs(0)
    tm = hbuf.shape[1]
    slot = i % 2
    last = n_t - 1
    grp = tg_ref[i]
    prev_valid = tv_ref[jnp.maximum(i - 1, 0)] > 0

    def gather_copy(r, t, sl):
        return pltpu.make_async_copy(hx_hbm.at[src_ref[t * tm + r]], hbuf.at[sl, r], gsem.at[sl])

    def scatter_copy(r, t, sl):
        return pltpu.make_async_copy(ybuf.at[sl, r], y_hbm.at[dst_ref[t * tm + r]], ssem.at[sl])

    def start_all(copy, t, sl):
        def body(j, c):
            copy(2 * j, t, sl).start(priority=0)
            copy(2 * j + 1, t, sl).start(priority=1)
            return c
        lax.fori_loop(0, tm // 2, body, 0, unroll=4)

    def wait_gather(sl):
        pltpu.make_async_copy(hx_hbm.at[pl.ds(0, tm)], hbuf.at[sl], gsem.at[sl]).wait()

    def wait_scatter(sl):
        pltpu.make_async_copy(ybuf.at[sl], y_hbm.at[pl.ds(0, tm)], ssem.at[sl]).wait()

    @pl.when(i == 0)
    def _():
        start_all(gather_copy, 0, 0)

    @pl.when((i == 0) | (grp != tg_ref[jnp.maximum(i - 1, 0)]))
    def _():
        wgb[...] = wg_ref[0].astype(BF16)
        wub[...] = wu_ref[0].astype(BF16)
        wdb[...] = wd_ref[0].astype(BF16)

    @pl.when((i == 0) | prev_valid)
    def _():
        wait_gather(slot)

    @pl.when(i >= 2)
    def _():
        wait_scatter(slot)

    def expert_tile(with_scatter):
        nxt = jnp.minimum(i + 1, last)
        n_batches = 3 * EXPERTS_PER_GROUP
        s_bounds = [min(tm, (b * tm) // (n_batches - 4)) for b in range(n_batches + 1)]
        g_bounds = [min(tm, (b * tm) // (n_batches - 2)) for b in range(n_batches + 1)]

        def start_batch(b):
            if with_scatter:
                for r in range(s_bounds[b], s_bounds[b + 1]):
                    scatter_copy(r, i - 1, 1 - slot).start(priority=1)
            for r in range(g_bounds[b], g_bounds[b + 1]):
                gather_copy(r, nxt, 1 - slot).start(priority=0)

        h = jnp.concatenate([hbuf[slot, :, s, :] for s in range(FEAT_ROWS)], axis=1).astype(BF16)
        gates = hbuf[slot, :, FEAT_ROWS, :]
        acc = jnp.zeros((tm, FEAT_ROWS * LANES), F32)
        for k in range(EXPERTS_PER_GROUP):
            start_batch(3 * k)
            a = _dot(h, wgb[k])
            start_batch(3 * k + 1)
            u = _dot(h, wub[k])
            hid = _silu(a) * u * gates[:, k:k + 1]
            start_batch(3 * k + 2)
            acc = acc + _dot(hid.astype(BF16), wdb[k])
        for s in range(FEAT_ROWS):
            ybuf[slot, :, s, :] = acc[:, s * LANES:(s + 1) * LANES]

    valid = tv_ref[i] > 0

    @pl.when(valid & (i == 0))
    def _():
        expert_tile(False)

    @pl.when(valid & (i > 0))
    def _():
        expert_tile(True)

    @pl.when(jnp.logical_not(valid))
    def _():
        ybuf[slot] = jnp.zeros(ybuf.shape[1:], F32)

        @pl.when(i > 0)
        def _():
            start_all(scatter_copy, i - 1, 1 - slot)

    @pl.when(i == last)
    def _():
        start_all(scatter_copy, i, slot)
        wait_scatter(slot)

        @pl.when(i > 0)
        def _():
            wait_scatter(1 - slot)

        @pl.when(valid)
        def _():
            wait_gather(1 - slot)


def _grouped_experts(l, tile_grp, tile_ok, src, dst, hx, w_gate, w_up, w_down):
    n_pad = src.shape[0]
    tm = MOE_TILE
    E = EXPERTS_PER_GROUP
    d = FEAT_ROWS * LANES
    wmap = lambda i, tg, tv, sr, ds: (l, tg[i], 0, 0)
    grid_spec = pltpu.PrefetchScalarGridSpec(
        num_scalar_prefetch=4,
        grid=(n_pad // tm,),
        in_specs=[pl.BlockSpec(memory_space=pl.ANY),
                  pl.BlockSpec((1, E, d, D_EXPERT), wmap),
                  pl.BlockSpec((1, E, d, D_EXPERT), wmap),
                  pl.BlockSpec((1, E, D_EXPERT, d), wmap)],
        out_specs=pl.BlockSpec(memory_space=pl.ANY),
        scratch_shapes=[pltpu.VMEM((2, tm, SLAB_ROWS, LANES), F32), pltpu.VMEM((2, tm, FEAT_ROWS, LANES), F32),
                        pltpu.VMEM((E, d, D_EXPERT), BF16), pltpu.VMEM((E, d, D_EXPERT), BF16),
                        pltpu.VMEM((E, D_EXPERT, d), BF16),
                        pltpu.SemaphoreType.DMA((2,)), pltpu.SemaphoreType.DMA((2,))])
    return pl.pallas_call(
        _moe_body,
        grid_spec=grid_spec,
        out_shape=jax.ShapeDtypeStruct((n_pad, FEAT_ROWS, LANES), F32),
        compiler_params=_cparams("arbitrary"),
        name="grouped_experts",
    )(tile_grp, tile_ok, src, dst, hx, w_gate, w_up, w_down)


def _owner_body(pos_ref, owner_ref):
    def clear(s, c):
        owner_ref[s] = 0
        return c

    def put(t, c):
        owner_ref[pos_ref[t]] = t + 1
        return c

    lax.fori_loop(0, owner_ref.shape[0], clear, 0, unroll=8)
    lax.fori_loop(0, pos_ref.shape[0], put, 0, unroll=8)


def _slot_owner(pos, n_pad):
    return pl.pallas_call(
        _owner_body,
        in_specs=[pl.BlockSpec(memory_space=pltpu.SMEM)],
        out_specs=pl.BlockSpec(memory_space=pltpu.SMEM),
        out_shape=jax.ShapeDtypeStruct((n_pad,), jnp.int32),
        name="slot_owner",
    )(pos)


def _dispatch_plan(route, n_pad):
    n = route.shape[1]
    tm = MOE_TILE
    grp = route[EXPERTS_PER_GROUP].astype(jnp.int32)
    onehot = (grp[:, None] == jnp.arange(N_EXPERT_GROUPS)[None, :]).astype(jnp.int32)
    csum = jnp.cumsum(onehot, axis=0)
    counts = csum[-1]
    rank = jnp.sum(csum * onehot, axis=1) - 1
    padded = ((counts + tm - 1) // tm) * tm
    ends = jnp.cumsum(padded)
    pos = jnp.sum((ends - padded)[None, :] * onehot, axis=1) + rank
    owner = _slot_owner(pos, n_pad)
    used = owner > 0
    src = jnp.maximum(owner - 1, 0)
    dst = jnp.where(used, owner - 1, n - 1 + jnp.cumsum(1 - used.astype(jnp.int32)))
    tile_start = jnp.arange(n_pad // tm, dtype=jnp.int32) * tm
    tile_grp = jnp.minimum(jnp.sum(tile_start[:, None] >= ends[None, :], axis=1), N_EXPERT_GROUPS - 1)
    tile_ok = (tile_start < ends[-1]).astype(jnp.int32)
    return tile_grp.astype(jnp.int32), tile_ok, src, dst


def _slab_rows(ref):
    return jnp.concatenate([ref[:, s, :] for s in range(FEAT_ROWS)], axis=1)


def _final_body(x_ref, y_ref, m_ref, g_ref, o_ref):
    x = x_ref[...] + m_ref[0, 5:6, :] * _slab_rows(y_ref)
    o_ref[...] = _rms(x, g_ref[...])


def _final_norm(l, n_lat, n_lat_tiles_per_batch, x, y, mod_r, g):
    d = x.shape[1]
    tm = TOKEN_TILE
    row = lambda i: (i, 0)
    return pl.pallas_call(
        _final_body,
        grid=(n_lat // tm,),
        in_specs=[pl.BlockSpec((tm, d), row), pl.BlockSpec((tm, FEAT_ROWS, LANES), lambda i: (i, 0, 0)),
                  pl.BlockSpec((1, N_MOD, d), lambda i: (l * 8 + i // n_lat_tiles_per_batch, 0, 0)),
                  pl.BlockSpec((1, d), lambda i: (0, 0))],
        out_specs=pl.BlockSpec((tm, d), row),
        out_shape=jax.ShapeDtypeStruct((n_lat, d), F32),
        compiler_params=_cparams("arbitrary"),
        name="final_norm",
    )(x, y, mod_r, g)


def _rope_tables(seq, n_batch, n_ctx_tokens):
    t = np.arange(seq)
    nf = MLA_ROPE // 4
    inv = jnp.asarray(ROPE_BASE, F32) ** (-jnp.arange(nf, dtype=F32) / nf)
    rang = jnp.asarray(t // GRID_W, F32)[:, None] * inv
    cang = jnp.asarray(t % GRID_W, F32)[:, None] * inv
    cr, sr, cc, sc = jnp.cos(rang), jnp.sin(rang), jnp.cos(cang), jnp.sin(cang)
    one = jnp.ones((seq, MLA_NOPE), F32)
    zero_tail = jnp.zeros((seq, MLA_SLOT - MLA_NOPE - MLA_ROPE), F32)
    cosp = jnp.concatenate([one, cr, cr, cc, cc, zero_tail], axis=1)
    sinp = jnp.concatenate([0 * one, -sr, sr, -sc, sc, zero_tail], axis=1)
    ctx_cos = jnp.concatenate([jnp.ones((n_ctx_tokens, MLA_NOPE + MLA_ROPE), F32),
                               jnp.zeros((n_ctx_tokens, MLA_SLOT - MLA_NOPE - MLA_ROPE), F32)], axis=1)
    cosp = jnp.concatenate([jnp.tile(cosp, (n_batch, 1)), ctx_cos], axis=0)
    sinp = jnp.concatenate([jnp.tile(sinp, (n_batch, 1)), jnp.zeros((n_ctx_tokens, MLA_SLOT), F32)], axis=0)
    return cosp, sinp


_ROPE_SWAP = np.concatenate([np.arange(8, 16), np.arange(0, 8), np.arange(24, 32), np.arange(16, 24)])


def _pad_last(a, before, after):
    return jnp.pad(a, [(0, 0)] * (a.ndim - 1) + [(before, after)])


def _layouts(w_in, w_out, pool_w, sgu_w, sgu_b, mla_w_uq, mla_w_ukv, router_w, router_b):
    L = w_in.shape[0]
    tail = MLA_SLOT - MLA_NOPE - MLA_ROPE
    kr = w_in[:, :, OFF_KR:]
    w = jnp.concatenate([w_in[:, :, :OFF_KR], _pad_last(kr, MLA_NOPE, tail),
                         _pad_last(kr[:, :, _ROPE_SWAP], MLA_NOPE, tail)], axis=-1).astype(BF16)
    uq = mla_w_uq.reshape(L, MLA_Q_RANK, N_GROUPS, MLA_NOPE + MLA_ROPE)
    q1 = _pad_last(uq, 0, tail).reshape(L, MLA_Q_RANK, N_GROUPS * MLA_SLOT)
    q2 = _pad_last(uq[..., MLA_NOPE:][..., _ROPE_SWAP], MLA_NOPE, tail).reshape(L, MLA_Q_RANK, N_GROUPS * MLA_SLOT)
    wq = jnp.concatenate([q1, q2], axis=-1).astype(BF16)
    ukv = mla_w_ukv.reshape(L, MLA_KV_RANK, N_GROUPS, MLA_NOPE + MLA_V)
    kn = _pad_last(ukv[..., :MLA_NOPE], 0, MLA_SLOT - MLA_NOPE).reshape(L, MLA_KV_RANK, N_GROUPS * MLA_SLOT)
    vv = _pad_last(ukv[..., MLA_NOPE:], 0, MLA_SLOT - MLA_V).reshape(L, MLA_KV_RANK, N_GROUPS * MLA_SLOT)
    wkv = kn.astype(BF16)
    wvt = jnp.swapaxes(vv, 1, 2).astype(BF16)
    eye = jnp.eye(N_GROUPS, dtype=F32)
    pool_bd = jnp.einsum('lgcd,gh->lgchd', pool_w, eye).reshape(L, MIX_PART, MIX_PART).astype(BF16)
    ones_bd = jnp.asarray(np.kron(np.eye(N_GROUPS), np.full((GROUP_DIM, GROUP_DIM), 1.0 / GROUP_DIM)), F32)
    sgu_all = sgu_w.reshape(L, N_GROUPS * SGU_CHUNK, SGU_CHUNK).astype(BF16)
    sgu_bias = jnp.repeat(jnp.swapaxes(sgu_b, 1, 2), GROUP_DIM, axis=2)
    rw = router_w.T.reshape(N_EXPERT_GROUPS, EXPERTS_PER_GROUP, -1).transpose(1, 0, 2)
    rw = jnp.pad(rw, ((0, 0), (0, 8 - N_EXPERT_GROUPS), (0, 0))).reshape(8 * EXPERTS_PER_GROUP, -1)
    rb = router_b.reshape(N_EXPERT_GROUPS, EXPERTS_PER_GROUP).T
    rb = jnp.pad(rb, ((0, 0), (0, 8 - N_EXPERT_GROUPS))).reshape(8 * EXPERTS_PER_GROUP, 1)
    return w, wq, wkv, wvt, w_out.astype(BF16), pool_bd, ones_bd, sgu_all, sgu_bias, rw, rb


def kernel(x, c, ctx, c_ctx, ada_w, ada_b, norm1_g, norm2_g, w_in, w_out, pool_w, pool_s, na_rpb,
           sgu_norm_g, sgu_w, sgu_b, mla_q_norm_g, mla_w_uq, mla_kv_norm_g, mla_w_ukv,
           router_w, router_b, moe_w_gate, moe_w_up, moe_w_down, final_g):
    B, S, D = x.shape
    CTX = ctx.shape[1]
    L = ada_w.shape[0]
    n_lat, n_ctx = B * S, B * CTX
    n_all = n_lat + n_ctx
    assert B < 8 and S % TOKEN_TILE == 0 and n_ctx % TOKEN_TILE == 0
    tiles_per_batch = S // TOKEN_TILE
    n_pad = n_all + N_EXPERT_GROUPS * MOE_TILE

    c_all = jnp.concatenate([c, c_ctx[None, :], jnp.zeros((8 - B - 1, D), F32)], axis=0)
    mod_r = _modulation(c_all, ada_w, ada_b).reshape(L * 8, N_MOD, D)
    w, wq, wkv, wvt, wo, pool_bd, ones_bd, sgu_all, sgu_bias, rw, rb = _layouts(
        w_in, w_out, pool_w, sgu_w, sgu_b, mla_w_uq, mla_w_ukv, router_w, router_b)
    bias_tab = _na_bias_tables(na_rpb, S // GRID_W)
    cosp, sinp = _rope_tables(S, B, n_ctx)

    xs = (x.reshape(n_lat, D), ctx.reshape(n_ctx, D))
    y = None
    for l in range(L):
        xs, pa, qb, kb, vb, pc, qd, kd, vd = _in_projection(
            l, tiles_per_batch, B, xs, y, mod_r, norm1_g[l][None], w[l], wq[l], mla_q_norm_g[l][None],
            wkv[l], wvt[l], mla_kv_norm_g[l][None], cosp, sinp)
        ya = _pool_mixer(n_lat, S, CTX, pa, pool_bd[l], pool_s[l][None])
        yb = _na_mixer(l, B, S, CTX, qb, kb, vb, bias_tab)
        yc = _sgu_mixer(pc, sgu_norm_g[l][None], ones_bd, sgu_all[l], sgu_bias[l])
        yd = _mla_mixer(B, S, CTX, qd, kd, vd)
        xs, hx, route = _out_projection(l, tiles_per_batch, B, (ya, yb, yc, yd), xs, mod_r, norm2_g[l][None],
                                        wo[l], rw, rb)
        tile_grp, tile_ok, src, dst = _dispatch_plan(route, n_pad)
        y = _grouped_experts(l, tile_grp, tile_ok, src, dst, hx, moe_w_gate, moe_w_up, moe_w_down)
    out = _final_norm(L - 1, n_lat, tiles_per_batch, xs, y, mod_r, final_g[None])
    return out.reshape(B, S, D)
```

```python
import functools

import numpy as np
import jax
import jax.numpy as jnp
from jax import lax
from jax.experimental import pallas as pl
from jax.experimental.pallas import tpu as pltpu

F32 = jnp.float32
BF16 = jnp.bfloat16
HIGHEST = lax.Precision.HIGHEST

GRID_W = 64
POOL_WINDOWS = (2, 4, 8, 16)
GROUP_DIM = 64
N_GROUPS = 4
MIX_PART = N_GROUPS * GROUP_DIM
NA_WIN_ROWS = 8
NA_WIN_COLS = 16
NA_Q_ROWS = 4
NA_K_ROWS = NA_Q_ROWS + NA_WIN_ROWS
SGU_CHUNK = 128
MLA_Q_RANK = 256
MLA_KV_RANK = 128
MLA_NOPE = 64
MLA_ROPE = 32
MLA_V = 64
MLA_SLOT = 128
MLA_SCALE = (MLA_NOPE + MLA_ROPE) ** -0.5
LOG2_E = float(np.log2(np.e))
ROPE_BASE = 10000.0
N_EXPERTS = 16
N_EXPERT_GROUPS = 4
EXPERTS_PER_GROUP = 4
D_EXPERT = 256
N_MOD = 6
EPS = 1e-6
NEG_INF = -1e30

OFF_B = MIX_PART
OFF_C = OFF_B + 3 * MIX_PART
OFF_D = OFF_C + 2 * MIX_PART
OFF_KR = OFF_D + MLA_Q_RANK + MLA_KV_RANK
W_IN_COLS = OFF_KR + 2 * MLA_SLOT

LANES = 128
FEAT_ROWS = 8
SLAB_ROWS = 16

TOKEN_TILE = 512
ATTN_TILE = 256
MLA_KV_TILE = 512
MLA_MAX_UNROLLED_TILES = 16
MOE_TILE = 512
VMEM_LIMIT = 56 * 1024 * 1024


def _cparams(*sem):
    return pltpu.CompilerParams(dimension_semantics=sem, vmem_limit_bytes=VMEM_LIMIT)


def _dot(a, b):
    return jnp.dot(a, b, preferred_element_type=F32)


def _dot_nt(a, b, precision=None):
    return lax.dot_general(a, b, (((1,), (1,)), ((), ())), precision=precision,
                           preferred_element_type=F32)


def _rms(x, g):
    return x * lax.rsqrt(jnp.mean(x * x, axis=-1, keepdims=True) + EPS) * g


def _silu(x):
    return x * jax.nn.sigmoid(x)


def _mod_body(c_ref, w_ref, b_ref, o_ref):
    o_ref[0] = jnp.dot(_silu(c_ref[...]), w_ref[0], precision=HIGHEST,
                       preferred_element_type=F32) + b_ref[0]


def _modulation(c_all, ada_w, ada_b):
    L, D, ND = ada_w.shape
    tn = 1536
    return pl.pallas_call(
        _mod_body,
        grid=(L, ND // tn),
        in_specs=[pl.BlockSpec((8, D), lambda l, j: (0, 0)),
                  pl.BlockSpec((1, D, tn), lambda l, j: (l, 0, j)),
                  pl.BlockSpec((1, 1, tn), lambda l, j: (l, 0, j))],
        out_specs=pl.BlockSpec((1, 8, tn), lambda l, j: (l, 0, j)),
        out_shape=jax.ShapeDtypeStruct((L, 8, ND), F32),
        compiler_params=_cparams("arbitrary", "arbitrary"),
        name="modulation",
    )(c_all, ada_w, ada_b.reshape(L, 1, ND))


def _inproj_body(has_res, n_lat_tiles, *refs):
    if has_res:
        (x_ref, y_ref, mp_ref, m_ref, g1_ref, w_ref, wq_ref, gq_ref, wkv_ref, wvt_ref, gkv_ref, cos_ref, sin_ref,
         xo_ref, pa_ref, qb_ref, kb_ref, vb_ref, pc_ref, qd_ref, kd_ref, vd_ref) = refs
        x = x_ref[...] + mp_ref[0, 5:6, :] * _slab_rows(y_ref)
    else:
        (xl_ref, xc_ref, m_ref, g1_ref, w_ref, wq_ref, gq_ref, wkv_ref, wvt_ref, gkv_ref, cos_ref, sin_ref,
         xo_ref, pa_ref, qb_ref, kb_ref, vb_ref, pc_ref, qd_ref, kd_ref, vd_ref) = refs
        x = jnp.where(pl.program_id(0) < n_lat_tiles, xl_ref[...], xc_ref[...])
    xo_ref[...] = x
    m = m_ref[0]
    hb = (_rms(x, g1_ref[...]) * (1.0 + m[1:2]) + m[0:1]).astype(BF16)

    def proj(a, b):
        return _dot(hb, w_ref[:, a:b])

    cq = proj(OFF_D, OFF_D + MLA_Q_RANK)
    ckv = proj(OFF_D + MLA_Q_RANK, OFF_KR)
    kr_plain = proj(OFF_KR, OFF_KR + MLA_SLOT)
    kr_swapped = proj(OFF_KR + MLA_SLOT, W_IN_COLS)

    pa_ref[...] = proj(0, OFF_B)
    qb_ref[...] = (proj(OFF_B, OFF_B + MIX_PART) * (GROUP_DIM ** -0.5)).astype(BF16)
    kb_ref[...] = proj(OFF_B + MIX_PART, OFF_B + 2 * MIX_PART).astype(BF16)
    vb_ref[...] = proj(OFF_B + 2 * MIX_PART, OFF_C).astype(BF16)
    pc_ref[...] = proj(OFF_C, OFF_D)

    cosp = cos_ref[...]
    sinp = sin_ref[...]
    qn = _rms(cq, gq_ref[...]).astype(BF16)
    qq = _dot(qn, wq_ref[...])
    half = N_GROUPS * MLA_SLOT
    for h in range(N_GROUPS):
        a = h * MLA_SLOT
        q = qq[:, a:a + MLA_SLOT] * cosp + qq[:, half + a:half + a + MLA_SLOT] * sinp
        qd_ref[:, a:a + MLA_SLOT] = (q * (MLA_SCALE * LOG2_E)).astype(BF16)
    kvn = _rms(ckv, gkv_ref[...]).astype(BF16)
    kk = _dot(kvn, wkv_ref[...])
    kr = kr_plain * cosp + kr_swapped * sinp
    for h in range(N_GROUPS):
        a = h * MLA_SLOT
        kd_ref[:, a:a + MLA_SLOT] = (kk[:, a:a + MLA_SLOT] + kr).astype(BF16)
    ones_row = (lax.broadcasted_iota(jnp.int32, (half, 1), 0) % MLA_SLOT == MLA_V).astype(F32)
    vd_ref[...] = (_dot_nt(wvt_ref[...], kvn) + ones_row).astype(BF16)


def _in_projection(l, n_lat_tiles_per_batch, n_batch, x, y, mod_r, g1, w, wq, gq, wkv, wvt, gkv, cosp, sinp):
    tm = TOKEN_TILE
    has_res = y is not None
    n_lat_tiles = n_lat_tiles_per_batch * n_batch
    if has_res:
        n, d = x.shape
    else:
        n, d = x[0].shape[0] + x[1].shape[0], x[0].shape[1]
        assert x[0].shape[0] == n_lat_tiles * tm

    def row(i):
        return (i, 0)

    def modrow(layer):
        return lambda i: (layer * 8 + jnp.minimum(i // n_lat_tiles_per_batch, n_batch), 0, 0)

    def const2(i):
        return (0, 0)

    tok = lambda c: pl.BlockSpec((tm, c), row)
    mod_spec = lambda layer: pl.BlockSpec((1, N_MOD, d), modrow(layer))
    if has_res:
        in_specs = [tok(d), pl.BlockSpec((tm, FEAT_ROWS, LANES), lambda i: (i, 0, 0)), mod_spec(l - 1)]
        args = [x, y, mod_r]
    else:
        in_specs = [pl.BlockSpec((tm, d), lambda i: (jnp.minimum(i, n_lat_tiles - 1), 0)),
                    pl.BlockSpec((tm, d), lambda i: (jnp.maximum(i - n_lat_tiles, 0), 0))]
        args = list(x)
    in_specs += [mod_spec(l), pl.BlockSpec((1, d), const2), pl.BlockSpec(w.shape, const2),
                 pl.BlockSpec(wq.shape, const2), pl.BlockSpec(gq.shape, const2),
                 pl.BlockSpec(wkv.shape, const2), pl.BlockSpec(wvt.shape, const2), pl.BlockSpec(gkv.shape, const2),
                 tok(MLA_SLOT), tok(MLA_SLOT)]
    args += [mod_r, g1, w, wq, gq, wkv, wvt, gkv, cosp, sinp]
    outs = [(d, F32), (MIX_PART, F32), (MIX_PART, BF16), (MIX_PART, BF16), (MIX_PART, BF16), (2 * MIX_PART, F32),
            (N_GROUPS * MLA_SLOT, BF16), (N_GROUPS * MLA_SLOT, BF16)]
    slots = N_GROUPS * MLA_SLOT
    return pl.pallas_call(
        functools.partial(_inproj_body, has_res, n_lat_tiles),
        grid=(n // tm,),
        in_specs=in_specs,
        out_specs=[tok(c) for c, _ in outs] + [pl.BlockSpec((slots, tm), lambda i: (0, i))],
        out_shape=[jax.ShapeDtypeStruct((n, c), t) for c, t in outs] + [jax.ShapeDtypeStruct((slots, n), BF16)],
        compiler_params=_cparams("arbitrary"),
        name="in_projection",
    )(*args)


def _pool_body(n_lat, seq, ctx_len, prev_ref, cur_ref, next_ref, w_ref, s_ref, o_ref):
    tb = cur_ref.shape[0]
    ext = jnp.concatenate([prev_ref[...], cur_ref[...], next_ref[...]], axis=0)
    n = tb + 16
    g = pl.program_id(0) * tb - 8 + lax.broadcasted_iota(jnp.int32, (n, 1), 0)
    is_lat = g < n_lat
    length = jnp.where(is_lat, seq, ctx_len)
    p = jnp.where(is_lat, g & (seq - 1), (g - n_lat) & (ctx_len - 1))

    def shifted(a, j):
        r = pltpu.roll(a, (-j) % n, axis=0)
        ok = (p + j >= 0) & (p + j < length)
        return jnp.where(ok, r, 0.0)

    before1 = shifted(ext, -1)
    before2 = before1 + shifted(before1, -1)
    before4 = before2 + shifted(before2, -2)
    before8 = before4 + shifted(before4, -4)
    after2 = ext + shifted(ext, 1)
    after4 = after2 + shifted(after2, 2)
    after8 = after4 + shifted(after4, 4)
    lane_grp = lax.broadcasted_iota(jnp.int32, (1, MIX_PART), 1) // GROUP_DIM
    sl = slice(8, 8 + tb)
    tot = jnp.where(lane_grp == 0, (before1 + ext)[sl],
                    jnp.where(lane_grp == 1, (before2 + after2)[sl],
                              jnp.where(lane_grp == 2, (before4 + after4)[sl], (before8 + after8)[sl])))
    half = jnp.where(lane_grp == 0, POOL_WINDOWS[0] // 2,
                     jnp.where(lane_grp == 1, POOL_WINDOWS[1] // 2,
                               jnp.where(lane_grp == 2, POOL_WINDOWS[2] // 2, POOL_WINDOWS[3] // 2)))
    pc = p[sl]
    cnt = jnp.minimum(pc + half, length[sl]) - jnp.maximum(pc - half, 0)
    dlt = tot / cnt.astype(F32) - ext[sl]
    o_ref[...] = (_dot(dlt.astype(BF16), w_ref[...]) * s_ref[...]).astype(BF16)


def _pool_mixer(n_lat, seq, ctx_len, pa, w_bd, s):
    n = pa.shape[0]
    tb = 1024
    assert seq & (seq - 1) == 0 and ctx_len & (ctx_len - 1) == 0
    assert n_lat % tb == 0 and n % tb == 0
    nb8 = n // 8
    return pl.pallas_call(
        functools.partial(_pool_body, n_lat, seq, ctx_len),
        grid=(n // tb,),
        in_specs=[pl.BlockSpec((8, MIX_PART), lambda i: (jnp.maximum(i * (tb // 8) - 1, 0), 0)),
                  pl.BlockSpec((tb, MIX_PART), lambda i: (i, 0)),
                  pl.BlockSpec((8, MIX_PART), lambda i: (jnp.minimum((i + 1) * (tb // 8), nb8 - 1), 0)),
                  pl.BlockSpec(w_bd.shape, lambda i: (0, 0)),
                  pl.BlockSpec(s.shape, lambda i: (0, 0))],
        out_specs=pl.BlockSpec((tb, MIX_PART), lambda i: (i, 0)),
        out_shape=jax.ShapeDtypeStruct((n, MIX_PART), BF16),
        compiler_params=_cparams("arbitrary"),
        name="pool_mixer",
    )(pa, pa, pa, w_bd, s)


def _softmax_pv(parts):
    m = parts[0][0].max(axis=-1, keepdims=True)
    for s, _ in parts[1:]:
        m = jnp.maximum(m, s.max(axis=-1, keepdims=True))
    den = 0.0
    out = 0.0
    for s, v in parts:
        p = jnp.exp(s - m)
        den = den + p.sum(axis=-1, keepdims=True)
        out = out + _dot(p.astype(BF16), v)
    return out / den


def _na_body(rows, n_qt, q_ref, kl_ref, kc_ref, vl_ref, vc_ref, bias_ref, o_ref):
    qt = pl.program_id(1)
    q = q_ref[...]
    kc = kc_ref[...]
    vc = vc_ref[...]
    lane_grp = lax.broadcasted_iota(jnp.int32, (1, MIX_PART), 1) // GROUP_DIM
    zero = jnp.zeros_like(q)

    @pl.when(qt < n_qt)
    def _():
        k0 = jnp.clip(NA_Q_ROWS * qt - NA_WIN_ROWS // 2, 0, rows - NA_K_ROWS)
        off = pl.multiple_of(k0 * GRID_W, NA_Q_ROWS * GRID_W)
        kw = kl_ref[pl.ds(off, NA_K_ROWS * GRID_W), :]
        vw = vl_ref[pl.ds(off, NA_K_ROWS * GRID_W), :]
        def head_scores(h):
            qh = jnp.where(lane_grp == h, q, zero)
            return _dot_nt(qh, kw), _dot_nt(qh, kc)

        o = jnp.zeros(q.shape, F32)
        nxt = head_scores(0)
        for h in range(N_GROUPS):
            s_win, s_ctx = nxt
            if h + 1 < N_GROUPS:
                nxt = head_scores(h + 1)
            oh = _softmax_pv([(s_win + bias_ref[0, h], vw), (s_ctx, vc)])
            o = jnp.where(lane_grp == h, oh, o)
        o_ref[...] = o.astype(BF16)

    @pl.when(qt == n_qt)
    def _():
        o = jnp.zeros(q.shape, F32)
        for h in range(N_GROUPS):
            qh = jnp.where(lane_grp == h, q, zero)
            oh = _softmax_pv([(_dot_nt(qh, kc), vc)])
            o = jnp.where(lane_grp == h, oh, o)
        o_ref[...] = o.astype(BF16)


def _attn_specs(n_batch, seq, ctx_len, width_q, width_k, width_v, width_o):
    n_qt = seq // ATTN_TILE
    lat_blocks = n_batch * n_qt
    assert ctx_len == ATTN_TILE

    def qmap(b, t):
        return (jnp.where(t < n_qt, b * n_qt + t, lat_blocks + b), 0)

    q_spec = pl.BlockSpec((ATTN_TILE, width_q), qmap)
    kl_spec = pl.BlockSpec((seq, width_k), lambda b, t: (b, 0))
    kc_spec = pl.BlockSpec((ctx_len, width_k), lambda b, t: (lat_blocks + b, 0))
    vl_spec = pl.BlockSpec((seq, width_v), lambda b, t: (b, 0))
    vc_spec = pl.BlockSpec((ctx_len, width_v), lambda b, t: (lat_blocks + b, 0))
    o_spec = pl.BlockSpec((ATTN_TILE, width_o), qmap)
    return n_qt, [q_spec, kl_spec, kc_spec, vl_spec, vc_spec], o_spec


def _na_mixer(l, n_batch, seq, ctx_len, qb, kb, vb, bias_tab):
    n = qb.shape[0]
    rows = seq // GRID_W
    n_qt, in_specs, o_spec = _attn_specs(n_batch, seq, ctx_len, MIX_PART, MIX_PART, MIX_PART, MIX_PART)
    assert ATTN_TILE == NA_Q_ROWS * GRID_W and rows >= NA_K_ROWS + NA_Q_ROWS

    def bias_map(b, t):
        kind = jnp.where(t == 0, 0, jnp.where(t >= n_qt - 1, 2, 1))
        return (l * 3 + kind, 0, 0, 0)

    in_specs.append(pl.BlockSpec((1,) + bias_tab.shape[1:], bias_map))
    return pl.pallas_call(
        functools.partial(_na_body, rows, n_qt),
        grid=(n_batch, n_qt + 1),
        in_specs=in_specs,
        out_specs=o_spec,
        out_shape=jax.ShapeDtypeStruct((n, MIX_PART), BF16),
        compiler_params=_cparams("arbitrary", "arbitrary"),
        name="neighborhood_attention",
    )(qb, kb, kb, vb, vb, bias_tab)


def _na_bias_tables(na_rpb, rows):
    L, H = na_rpb.shape[:2]
    kinds = ((0, 0), (NA_Q_ROWS, 0), (rows - NA_Q_ROWS, rows - NA_K_ROWS))
    qc = np.arange(GRID_W)
    kc = np.arange(GRID_W)
    wsc = np.clip(qc - NA_WIN_COLS // 2, 0, GRID_W - NA_WIN_COLS)
    col_ok = (kc[None, :] >= wsc[:, None]) & (kc[None, :] < wsc[:, None] + NA_WIN_COLS)
    padded = _pad_last(na_rpb, GRID_W - NA_WIN_COLS, GRID_W - NA_WIN_COLS)
    shifted = jnp.stack([padded[..., GRID_W - 1 - c:2 * GRID_W - 1 - c] for c in range(GRID_W)], axis=2)
    slabs = jnp.where(col_ok[:, None, :], shifted, NEG_INF)

    def masked(n):
        return jnp.full((L, H, GRID_W, n, GRID_W), NEG_INF, F32)

    tabs = []
    for r0, k0 in kinds:
        per_row = []
        for qr in range(NA_Q_ROWS):
            r = r0 + qr
            first = int(np.clip(r - NA_WIN_ROWS // 2, 0, rows - NA_WIN_ROWS)) - k0
            d0 = k0 + first - r + NA_WIN_ROWS - 1
            assert 0 <= first <= NA_K_ROWS - NA_WIN_ROWS and 0 <= d0 <= NA_WIN_ROWS - 1
            per_row.append(jnp.concatenate(
                [masked(first), slabs[:, :, :, d0:d0 + NA_WIN_ROWS, :], masked(NA_K_ROWS - NA_WIN_ROWS - first)],
                axis=3))
        tabs.append(jnp.stack(per_row, axis=2))
    return jnp.stack(tabs, axis=1).reshape(L * 3, H, NA_Q_ROWS * GRID_W, NA_K_ROWS * GRID_W)


def _gelu_tanh(x):
    return 0.5 * x * (1.0 + jnp.tanh(np.sqrt(2.0 / np.pi).astype(np.float32) * (x + 0.044715 * (x * x * x))))


def _sgu_body(pc_ref, gn_ref, ones_ref, w_ref, b_ref, o_ref):
    tm = pc_ref.shape[0]
    uv = _gelu_tanh(pc_ref[...])
    u = uv[:, :MIX_PART]
    v = uv[:, MIX_PART:]
    ms = jnp.dot(v * v, ones_ref[...], precision=HIGHEST, preferred_element_type=F32)
    vg = (v * lax.rsqrt(ms + EPS) * gn_ref[...]).astype(BF16)
    lane_grp = lax.broadcasted_iota(jnp.int32, (1, MIX_PART), 1) // GROUP_DIM
    w = w_ref[...]
    for c in range(tm // SGU_CHUNK):
        rs = slice(c * SGU_CHUNK, (c + 1) * SGU_CHUNK)
        r = _dot(w, vg[rs])
        mixed = r[:SGU_CHUNK]
        for g in range(1, N_GROUPS):
            mixed = jnp.where(lane_grp == g, r[g * SGU_CHUNK:(g + 1) * SGU_CHUNK], mixed)
        o_ref[rs, :] = (u[rs] * (mixed + b_ref[...])).astype(BF16)


def _sgu_mixer(pc, gn, ones_bd, w_all, b_exp):
    n = pc.shape[0]
    tm = TOKEN_TILE
    const = lambda i: (0, 0)
    return pl.pallas_call(
        _sgu_body,
        grid=(n // tm,),
        in_specs=[pl.BlockSpec((tm, 2 * MIX_PART), lambda i: (i, 0)),
                  pl.BlockSpec(gn.shape, const), pl.BlockSpec(ones_bd.shape, const),
                  pl.BlockSpec(w_all.shape, const), pl.BlockSpec(b_exp.shape, const)],
        out_specs=pl.BlockSpec((tm, MIX_PART), lambda i: (i, 0)),
        out_shape=jax.ShapeDtypeStruct((n, MIX_PART), BF16),
        compiler_params=_cparams("arbitrary"),
        name="spatial_gating",
    )(pc, gn, ones_bd, w_all, b_exp)


def _mla_body(n_qt, n_kv, q_ref, kl_ref, kc_ref, vl_ref, vc_ref, o_ref, sa_ref, sb_ref):
    qt = pl.program_id(1)
    heads = [slice(h * MLA_SLOT, (h + 1) * MLA_SLOT) for h in range(N_GROUPS)]

    def scores(t, h, dst):
        rows = slice(t * MLA_KV_TILE, (t + 1) * MLA_KV_TILE)
        dst[h] = _dot_nt(kl_ref[rows, heads[h]], q_ref[:, heads[h]])

    def consume(t, h, src, m, acc):
        s = src[h]
        mn = jnp.maximum(m, s.max(axis=0, keepdims=True))
        p = jnp.exp2(s - mn).astype(BF16)
        return mn, jnp.exp2(m - mn) * acc + _dot(vl_ref[heads[h], t * MLA_KV_TILE:(t + 1) * MLA_KV_TILE], p)

    def attend(n_tiles):
        ctx_scores = [_dot_nt(kc_ref[:, hs], q_ref[:, hs]) for hs in heads]
        if n_tiles:
            for h in range(N_GROUPS):
                scores(0, h, sa_ref)
        state = []
        for hs, s in zip(heads, ctx_scores):
            m = s.max(axis=0, keepdims=True)
            state.append((m, _dot(vc_ref[hs, :], jnp.exp2(s - m).astype(BF16))))
        for t in range(n_tiles):
            cur, nxt = (sa_ref, sb_ref) if t % 2 == 0 else (sb_ref, sa_ref)
            out = []
            for h, (m, acc) in enumerate(state):
                if t + 1 < n_tiles:
                    scores(t + 1, h, nxt)
                out.append(consume(t, h, cur, m, acc))
            state = out
        out_t = jnp.concatenate([acc[:MLA_V] / acc[MLA_V:MLA_V + 1] for _, acc in state], axis=0)
        o_ref[...] = out_t.T.astype(BF16)

    @pl.when(qt < n_qt)
    def _():
        attend(n_kv)

    @pl.when(qt == n_qt)
    def _():
        attend(0)


def _mla_mixer(n_batch, seq, ctx_len, qd, kd, vdt):
    n = qd.shape[0]
    slots = N_GROUPS * MLA_SLOT
    n_qt, in_specs, o_spec = _attn_specs(n_batch, seq, ctx_len, slots, slots, slots, N_GROUPS * MLA_V)
    lat_blocks = n_batch * n_qt
    in_specs[3] = pl.BlockSpec((slots, seq), lambda b, t: (0, b))
    in_specs[4] = pl.BlockSpec((slots, ctx_len), lambda b, t: (0, lat_blocks + b))
    n_kv = seq // MLA_KV_TILE
    assert seq % MLA_KV_TILE == 0 and n_kv <= MLA_MAX_UNROLLED_TILES
    return pl.pallas_call(
        functools.partial(_mla_body, n_qt, n_kv),
        grid=(n_batch, n_qt + 1),
        in_specs=in_specs,
        out_specs=o_spec,
        out_shape=jax.ShapeDtypeStruct((n, N_GROUPS * MLA_V), BF16),
        scratch_shapes=[pltpu.VMEM((N_GROUPS, MLA_KV_TILE, ATTN_TILE), F32)] * 2,
        compiler_params=_cparams("arbitrary", "arbitrary"),
        name="latent_attention",
    )(qd, kd, kd, vdt, vdt)


def _outproj_body(ya_ref, yb_ref, yc_ref, yd_ref, x_ref, m_ref, g2_ref, w_ref, rw_ref, rb_ref,
                  xo_ref, hx_ref, r_ref):
    tm = x_ref.shape[0]
    m = m_ref[0]
    y = _dot(ya_ref[...], w_ref[0:MIX_PART, :])
    for k, ref in enumerate((yb_ref, yc_ref, yd_ref), start=1):
        y = y + _dot(ref[...], w_ref[k * MIX_PART:(k + 1) * MIX_PART, :])
    x = x_ref[...] + m[2:3] * y
    xo_ref[...] = x
    h = _rms(x, g2_ref[...]) * (1.0 + m[4:5]) + m[3:4]

    scores = jax.nn.sigmoid(_dot_nt(rw_ref[...], h, precision=HIGHEST))
    biased = scores + rb_ref[...]
    E = EXPERTS_PER_GROUP
    bk = [biased[8 * k:8 * k + 8] for k in range(E)]
    sk = [scores[8 * k:8 * k + 8] for k in range(E)]
    gs = None
    for a in range(E):
        for b in range(a + 1, E):
            pair = bk[a] + bk[b]
            gs = pair if gs is None else jnp.maximum(gs, pair)
    best = gs[0:1]
    idx = jnp.zeros((1, tm), jnp.int32)
    for g in range(1, N_EXPERT_GROUPS):
        better = gs[g:g + 1] > best
        idx = jnp.where(better, g, idx)
        best = jnp.where(better, gs[g:g + 1], best)
    in_grp = lax.broadcasted_iota(jnp.int32, (8, tm), 0) == idx
    wk = []
    for k in range(E):
        rank = jnp.zeros((8, tm), jnp.int32)
        for j in range(E):
            if j != k:
                ahead = (bk[j] > bk[k]) | ((bk[j] == bk[k]) & (j < k))
                rank = rank + ahead.astype(jnp.int32)
        wk.append(jnp.where((rank < 2) & in_grp, sk[k], 0.0).sum(axis=0, keepdims=True))
    den = wk[0] + wk[1] + wk[2] + wk[3]
    ri = lax.broadcasted_iota(jnp.int32, (8, tm), 0)
    out = jnp.where(ri == E, idx.astype(F32), 0.0)
    for k in range(E):
        out = jnp.where(ri == k, wk[k] / den, out)
    r_ref[...] = out

    for s in range(FEAT_ROWS):
        hx_ref[:, s, :] = h[:, s * LANES:(s + 1) * LANES]
    hx_ref[:, FEAT_ROWS, :] = jnp.concatenate([out, jnp.zeros((LANES - 8, tm), F32)], axis=0).T
    hx_ref[:, FEAT_ROWS + 1:, :] = jnp.zeros((tm, SLAB_ROWS - FEAT_ROWS - 1, LANES), F32)


def _out_projection(l, n_lat_tiles_per_batch, n_batch, ys, x, mod_r, g2, w, rw, rb):
    n, d = x.shape
    tm = TOKEN_TILE
    row = lambda i: (i, 0)
    const = lambda i: (0, 0)
    modrow = lambda i: (l * 8 + jnp.minimum(i // n_lat_tiles_per_batch, n_batch), 0, 0)
    in_specs = [pl.BlockSpec((tm, MIX_PART), row)] * 4 + [
        pl.BlockSpec((tm, d), row), pl.BlockSpec((1, N_MOD, d), modrow), pl.BlockSpec((1, d), const),
        pl.BlockSpec(w.shape, const), pl.BlockSpec(rw.shape, const), pl.BlockSpec(rb.shape, const)]
    return pl.pallas_call(
        _outproj_body,
        grid=(n // tm,),
        in_specs=in_specs,
        out_specs=[pl.BlockSpec((tm, d), row), pl.BlockSpec((tm, SLAB_ROWS, LANES), lambda i: (i, 0, 0)),
                   pl.BlockSpec((8, tm), lambda i: (0, i))],
        out_shape=[jax.ShapeDtypeStruct((n, d), F32), jax.ShapeDtypeStruct((n, SLAB_ROWS, LANES), F32),
                   jax.ShapeDtypeStruct((8, n), F32)],
        compiler_params=_cparams("arbitrary"),
        name="out_projection_routing",
    )(*ys, x, mod_r, g2, w, rw, rb)


def _moe_body(tg_ref, tv_ref, src_ref, dst_ref, hx_hbm, wg_ref, wu_ref, wd_ref, y_hbm,
              hbuf, ybuf, wgb, wub, wdb, gsem, ssem):
    i = pl.program_id(0)
    n_t = pl.num_programs(0)
    tm = hbuf.shape[1]
    slot = i % 2
    last = n_t - 1
    grp = tg_ref[i]
    prev_valid = tv_ref[jnp.maximum(i - 1, 0)] > 0

    def gather_copy(r, t, sl):
        return pltpu.make_async_copy(hx_hbm.at[src_ref[t * tm + r]], hbuf.at[sl, r], gsem.at[sl])

    def scatter_copy(r, t, sl):
        return pltpu.make_async_copy(ybuf.at[sl, r], y_hbm.at[dst_ref[t * tm + r]], ssem.at[sl])

    def start_all(copy, t, sl):
        def body(j, c):
            copy(2 * j, t, sl).start(priority=0)
            copy(2 * j + 1, t, sl).start(priority=1)
            return c
        lax.fori_loop(0, tm // 2, body, 0, unroll=4)

    def wait_gather(sl):
        pltpu.make_async_copy(hx_hbm.at[pl.ds(0, tm)], hbuf.at[sl], gsem.at[sl]).wait()

    def wait_scatter(sl):
        pltpu.make_async_copy(ybuf.at[sl], y_hbm.at[pl.ds(0, tm)], ssem.at[sl]).wait()

    @pl.when(i == 0)
    def _():
        start_all(gather_copy, 0, 0)

    @pl.when((i == 0) | (grp != tg_ref[jnp.maximum(i - 1, 0)]))
    def _():
        wgb[...] = wg_ref[0].astype(BF16)
        wub[...] = wu_ref[0].astype(BF16)
        wdb[...] = wd_ref[0].astype(BF16)

    @pl.when((i == 0) | prev_valid)
    def _():
        wait_gather(slot)

    @pl.when(i >= 2)
    def _():
        wait_scatter(slot)

    def expert_tile(with_scatter):
        nxt = jnp.minimum(i + 1, last)
        n_batches = 3 * EXPERTS_PER_GROUP
        s_bounds = [min(tm, (b * tm) // (n_batches - 4)) for b in range(n_batches + 1)]
        g_bounds = [min(tm, (b * tm) // (n_batches - 2)) for b in range(n_batches + 1)]

        def start_batch(b):
            if with_scatter:
                for r in range(s_bounds[b], s_bounds[b + 1]):
                    scatter_copy(r, i - 1, 1 - slot).start(priority=1)
            for r in range(g_bounds[b], g_bounds[b + 1]):
                gather_copy(r, nxt, 1 - slot).start(priority=0)

        h = jnp.concatenate([hbuf[slot, :, s, :] for s in range(FEAT_ROWS)], axis=1).astype(BF16)
        gates = hbuf[slot, :, FEAT_ROWS, :]
        acc = jnp.zeros((tm, FEAT_ROWS * LANES), F32)
        for k in range(EXPERTS_PER_GROUP):
            start_batch(3 * k)
            a = _dot(h, wgb[k])
            start_batch(3 * k + 1)
            u = _dot(h, wub[k])
            hid = _silu(a) * u * gates[:, k:k + 1]
            start_batch(3 * k + 2)
            acc = acc + _dot(hid.astype(BF16), wdb[k])
        for s in range(FEAT_ROWS):
            ybuf[slot, :, s, :] = acc[:, s * LANES:(s + 1) * LANES]

    valid = tv_ref[i] > 0

    @pl.when(valid & (i == 0))
    def _():
        expert_tile(False)

    @pl.when(valid & (i > 0))
    def _():
        expert_tile(True)

    @pl.when(jnp.logical_not(valid))
    def _():
        ybuf[slot] = jnp.zeros(ybuf.shape[1:], F32)

        @pl.when(i > 0)
        def _():
            start_all(scatter_copy, i - 1, 1 - slot)

    @pl.when(i == last)
    def _():
        start_all(scatter_copy, i, slot)
        wait_scatter(slot)

        @pl.when(i > 0)
        def _():
            wait_scatter(1 - slot)

        @pl.when(valid)
        def _():
            wait_gather(1 - slot)


def _grouped_experts(l, tile_grp, tile_ok, src, dst, hx, w_gate, w_up, w_down):
    n_pad = src.shape[0]
    tm = MOE_TILE
    E = EXPERTS_PER_GROUP
    d = FEAT_ROWS * LANES
    wmap = lambda i, tg, tv, sr, ds: (l, tg[i], 0, 0)
    grid_spec = pltpu.PrefetchScalarGridSpec(
        num_scalar_prefetch=4,
        grid=(n_pad // tm,),
        in_specs=[pl.BlockSpec(memory_space=pl.ANY),
                  pl.BlockSpec((1, E, d, D_EXPERT), wmap),
                  pl.BlockSpec((1, E, d, D_EXPERT), wmap),
                  pl.BlockSpec((1, E, D_EXPERT, d), wmap)],
        out_specs=pl.BlockSpec(memory_space=pl.ANY),
        scratch_shapes=[pltpu.VMEM((2, tm, SLAB_ROWS, LANES), F32), pltpu.VMEM((2, tm, FEAT_ROWS, LANES), F32),
                        pltpu.VMEM((E, d, D_EXPERT), BF16), pltpu.VMEM((E, d, D_EXPERT), BF16),
                        pltpu.VMEM((E, D_EXPERT, d), BF16),
                        pltpu.SemaphoreType.DMA((2,)), pltpu.SemaphoreType.DMA((2,))])
    return pl.pallas_call(
        _moe_body,
        grid_spec=grid_spec,
        out_shape=jax.ShapeDtypeStruct((n_pad, FEAT_ROWS, LANES), F32),
        compiler_params=_cparams("arbitrary"),
        name="grouped_experts",
    )(tile_grp, tile_ok, src, dst, hx, w_gate, w_up, w_down)


def _owner_body(pos_ref, owner_ref):
    def clear(s, c):
        owner_ref[s] = 0
        return c

    def put(t, c):
        owner_ref[pos_ref[t]] = t + 1
        return c

    lax.fori_loop(0, owner_ref.shape[0], clear, 0, unroll=8)
    lax.fori_loop(0, pos_ref.shape[0], put, 0, unroll=8)


def _slot_owner(pos, n_pad):
    return pl.pallas_call(
        _owner_body,
        in_specs=[pl.BlockSpec(memory_space=pltpu.SMEM)],
        out_specs=pl.BlockSpec(memory_space=pltpu.SMEM),
        out_shape=jax.ShapeDtypeStruct((n_pad,), jnp.int32),
        name="slot_owner",
    )(pos)


def _dispatch_plan(route, n_pad):
    n = route.shape[1]
    tm = MOE_TILE
    grp = route[EXPERTS_PER_GROUP].astype(jnp.int32)
    onehot = (grp[:, None] == jnp.arange(N_EXPERT_GROUPS)[None, :]).astype(jnp.int32)
    csum = jnp.cumsum(onehot, axis=0)
    counts = csum[-1]
    rank = jnp.sum(csum * onehot, axis=1) - 1
    padded = ((counts + tm - 1) // tm) * tm
    ends = jnp.cumsum(padded)
    pos = jnp.sum((ends - padded)[None, :] * onehot, axis=1) + rank
    owner = _slot_owner(pos, n_pad)
    used = owner > 0
    src = jnp.maximum(owner - 1, 0)
    dst = jnp.where(used, owner - 1, n - 1 + jnp.cumsum(1 - used.astype(jnp.int32)))
    tile_start = jnp.arange(n_pad // tm, dtype=jnp.int32) * tm
    tile_grp = jnp.minimum(jnp.sum(tile_start[:, None] >= ends[None, :], axis=1), N_EXPERT_GROUPS - 1)
    tile_ok = (tile_start < ends[-1]).astype(jnp.int32)
    return tile_grp.astype(jnp.int32), tile_ok, src, dst


def _slab_rows(ref):
    return jnp.concatenate([ref[:, s, :] for s in range(FEAT_ROWS)], axis=1)


def _final_body(x_ref, y_ref, m_ref, g_ref, o_ref):
    x = x_ref[...] + m_ref[0, 5:6, :] * _slab_rows(y_ref)
    o_ref[...] = _rms(x, g_ref[...])


def _final_norm(l, n_lat, n_lat_tiles_per_batch, x, y, mod_r, g):
    d = x.shape[1]
    tm = TOKEN_TILE
    row = lambda i: (i, 0)
    return pl.pallas_call(
        _final_body,
        grid=(n_lat // tm,),
        in_specs=[pl.BlockSpec((tm, d), row), pl.BlockSpec((tm, FEAT_ROWS, LANES), lambda i: (i, 0, 0)),
                  pl.BlockSpec((1, N_MOD, d), lambda i: (l * 8 + i // n_lat_tiles_per_batch, 0, 0)),
                  pl.BlockSpec((1, d), lambda i: (0, 0))],
        out_specs=pl.BlockSpec((tm, d), row),
        out_shape=jax.ShapeDtypeStruct((n_lat, d), F32),
        compiler_params=_cparams("arbitrary"),
        name="final_norm",
    )(x, y, mod_r, g)


def _rope_tables(seq, n_batch, n_ctx_tokens):
    t = np.arange(seq)
    nf = MLA_ROPE // 4
    inv = jnp.asarray(ROPE_BASE, F32) ** (-jnp.arange(nf, dtype=F32) / nf)
    rang = jnp.asarray(t // GRID_W, F32)[:, None] * inv
    cang = jnp.asarray(t % GRID_W, F32)[:, None] * inv
    cr, sr, cc, sc = jnp.cos(rang), jnp.sin(rang), jnp.cos(cang), jnp.sin(cang)
    one = jnp.ones((seq, MLA_NOPE), F32)
    zero_tail = jnp.zeros((seq, MLA_SLOT - MLA_NOPE - MLA_ROPE), F32)
    cosp = jnp.concatenate([one, cr, cr, cc, cc, zero_tail], axis=1)
    sinp = jnp.concatenate([0 * one, -sr, sr, -sc, sc, zero_tail], axis=1)
    ctx_cos = jnp.concatenate([jnp.ones((n_ctx_tokens, MLA_NOPE + MLA_ROPE), F32),
                               jnp.zeros((n_ctx_tokens, MLA_SLOT - MLA_NOPE - MLA_ROPE), F32)], axis=1)
    cosp = jnp.concatenate([jnp.tile(cosp, (n_batch, 1)), ctx_cos], axis=0)
    sinp = jnp.concatenate([jnp.tile(sinp, (n_batch, 1)), jnp.zeros((n_ctx_tokens, MLA_SLOT), F32)], axis=0)
    return cosp, sinp


_ROPE_SWAP = np.concatenate([np.arange(8, 16), np.arange(0, 8), np.arange(24, 32), np.arange(16, 24)])


def _pad_last(a, before, after):
    return jnp.pad(a, [(0, 0)] * (a.ndim - 1) + [(before, after)])


def _layouts(w_in, w_out, pool_w, sgu_w, sgu_b, mla_w_uq, mla_w_ukv, router_w, router_b):
    L = w_in.shape[0]
    tail = MLA_SLOT - MLA_NOPE - MLA_ROPE
    kr = w_in[:, :, OFF_KR:]
    w = jnp.concatenate([w_in[:, :, :OFF_KR], _pad_last(kr, MLA_NOPE, tail),
                         _pad_last(kr[:, :, _ROPE_SWAP], MLA_NOPE, tail)], axis=-1).astype(BF16)
    uq = mla_w_uq.reshape(L, MLA_Q_RANK, N_GROUPS, MLA_NOPE + MLA_ROPE)
    q1 = _pad_last(uq, 0, tail).reshape(L, MLA_Q_RANK, N_GROUPS * MLA_SLOT)
    q2 = _pad_last(uq[..., MLA_NOPE:][..., _ROPE_SWAP], MLA_NOPE, tail).reshape(L, MLA_Q_RANK, N_GROUPS * MLA_SLOT)
    wq = jnp.concatenate([q1, q2], axis=-1).astype(BF16)
    ukv = mla_w_ukv.reshape(L, MLA_KV_RANK, N_GROUPS, MLA_NOPE + MLA_V)
    kn = _pad_last(ukv[..., :MLA_NOPE], 0, MLA_SLOT - MLA_NOPE).reshape(L, MLA_KV_RANK, N_GROUPS * MLA_SLOT)
    vv = _pad_last(ukv[..., MLA_NOPE:], 0, MLA_SLOT - MLA_V).reshape(L, MLA_KV_RANK, N_GROUPS * MLA_SLOT)
    wkv = kn.astype(BF16)
    wvt = jnp.swapaxes(vv, 1, 2).astype(BF16)
    eye = jnp.eye(N_GROUPS, dtype=F32)
    pool_bd = jnp.einsum('lgcd,gh->lgchd', pool_w, eye).reshape(L, MIX_PART, MIX_PART).astype(BF16)
    ones_bd = jnp.asarray(np.kron(np.eye(N_GROUPS), np.full((GROUP_DIM, GROUP_DIM), 1.0 / GROUP_DIM)), F32)
    sgu_all = sgu_w.reshape(L, N_GROUPS * SGU_CHUNK, SGU_CHUNK).astype(BF16)
    sgu_bias = jnp.repeat(jnp.swapaxes(sgu_b, 1, 2), GROUP_DIM, axis=2)
    rw = router_w.T.reshape(N_EXPERT_GROUPS, EXPERTS_PER_GROUP, -1).transpose(1, 0, 2)
    rw = jnp.pad(rw, ((0, 0), (0, 8 - N_EXPERT_GROUPS), (0, 0))).reshape(8 * EXPERTS_PER_GROUP, -1)
    rb = router_b.reshape(N_EXPERT_GROUPS, EXPERTS_PER_GROUP).T
    rb = jnp.pad(rb, ((0, 0), (0, 8 - N_EXPERT_GROUPS))).reshape(8 * EXPERTS_PER_GROUP, 1)
    return w, wq, wkv, wvt, w_out.astype(BF16), pool_bd, ones_bd, sgu_all, sgu_bias, rw, rb


def kernel(x, c, ctx, c_ctx, ada_w, ada_b, norm1_g, norm2_g, w_in, w_out, pool_w, pool_s, na_rpb,
           sgu_norm_g, sgu_w, sgu_b, mla_q_norm_g, mla_w_uq, mla_kv_norm_g, mla_w_ukv,
           router_w, router_b, moe_w_gate, moe_w_up, moe_w_down, final_g):
    B, S, D = x.shape
    CTX = ctx.shape[1]
    L = ada_w.shape[0]
    n_lat, n_ctx = B * S, B * CTX
    n_all = n_lat + n_ctx
    assert B < 8 and S % TOKEN_TILE == 0 and n_ctx % TOKEN_TILE == 0
    tiles_per_batch = S // TOKEN_TILE
    n_pad = n_all + N_EXPERT_GROUPS * MOE_TILE

    c_all = jnp.concatenate([c, c_ctx[None, :], jnp.zeros((8 - B - 1, D), F32)], axis=0)
    mod_r = _modulation(c_all, ada_w, ada_b).reshape(L * 8, N_MOD, D)
    w, wq, wkv, wvt, wo, pool_bd, ones_bd, sgu_all, sgu_bias, rw, rb = _layouts(
        w_in, w_out, pool_w, sgu_w, sgu_b, mla_w_uq, mla_w_ukv, router_w, router_b)
    bias_tab = _na_bias_tables(na_rpb, S // GRID_W)
    cosp, sinp = _rope_tables(S, B, n_ctx)

    xs = (x.reshape(n_lat, D), ctx.reshape(n_ctx, D))
    y = None
    for l in range(L):
        xs, pa, qb, kb, vb, pc, qd, kd, vd = _in_projection(
            l, tiles_per_batch, B, xs, y, mod_r, norm1_g[l][None], w[l], wq[l], mla_q_norm_g[l][None],
            wkv[l], wvt[l], mla_kv_norm_g[l][None], cosp, sinp)
        ya = _pool_mixer(n_lat, S, CTX, pa, pool_bd[l], pool_s[l][None])
        yb = _na_mixer(l, B, S, CTX, qb, kb, vb, bias_tab)
        yc = _sgu_mixer(pc, sgu_norm_g[l][None], ones_bd, sgu_all[l], sgu_bias[l])
        yd = _mla_mixer(B, S, CTX, qd, kd, vd)
        xs, hx, route = _out_projection(l, tiles_per_batch, B, (ya, yb, yc, yd), xs, mod_r, norm2_g[l][None],
                                        wo[l], rw, rb)
        tile_grp, tile_ok, src, dst = _dispatch_plan(route, n_pad)
        y = _grouped_experts(l, tile_grp, tile_ok, src, dst, hx, moe_w_gate, moe_w_up, moe_w_down)
    out = _final_norm(L - 1, n_lat, tiles_per_batch, xs, y, mod_r, final_g[None])
    return out.reshape(B, S, D)
```

```python
import functools

import numpy as np
import jax
import jax.numpy as jnp
from jax import lax
from jax.experimental import pallas as pl
from jax.experimental.pallas import tpu as pltpu

F32 = jnp.float32
BF16 = jnp.bfloat16
HIGHEST = lax.Precision.HIGHEST

GRID_W = 64
POOL_WINDOWS = (2, 4, 8, 16)
GROUP_DIM = 64
N_GROUPS = 4
MIX_PART = N_GROUPS * GROUP_DIM
NA_WIN_ROWS = 8
NA_WIN_COLS = 16
NA_Q_ROWS = 4
NA_K_ROWS = NA_Q_ROWS + NA_WIN_ROWS
SGU_CHUNK = 128
MLA_Q_RANK = 256
MLA_KV_RANK = 128
MLA_NOPE = 64
MLA_ROPE = 32
MLA_V = 64
MLA_SLOT = 128
MLA_SCALE = (MLA_NOPE + MLA_ROPE) ** -0.5
LOG2_E = float(np.log2(np.e))
ROPE_BASE = 10000.0
N_EXPERTS = 16
N_EXPERT_GROUPS = 4
EXPERTS_PER_GROUP = 4
D_EXPERT = 256
N_MOD = 6
EPS = 1e-6
NEG_INF = -1e30

OFF_B = MIX_PART
OFF_C = OFF_B + 3 * MIX_PART
OFF_D = OFF_C + 2 * MIX_PART
OFF_KR = OFF_D + MLA_Q_RANK + MLA_KV_RANK
W_IN_COLS = OFF_KR + 2 * MLA_SLOT

LANES = 128
FEAT_ROWS = 8
SLAB_ROWS = 16

TOKEN_TILE = 512
ATTN_TILE = 256
MLA_KV_TILE = 512
MLA_MAX_UNROLLED_TILES = 16
MOE_TILE = 512
VMEM_LIMIT = 56 * 1024 * 1024


def _cparams(*sem):
    return pltpu.CompilerParams(dimension_semantics=sem, vmem_limit_bytes=VMEM_LIMIT)


def _dot(a, b):
    return jnp.dot(a, b, preferred_element_type=F32)


def _dot_nt(a, b, precision=None):
    return lax.dot_general(a, b, (((1,), (1,)), ((), ())), precision=precision,
                           preferred_element_type=F32)


def _rms(x, g):
    return x * lax.rsqrt(jnp.mean(x * x, axis=-1, keepdims=True) + EPS) * g


def _silu(x):
    return x * jax.nn.sigmoid(x)


def _mod_body(c_ref, w_ref, b_ref, o_ref):
    o_ref[0] = jnp.dot(_silu(c_ref[...]), w_ref[0], precision=HIGHEST,
                       preferred_element_type=F32) + b_ref[0]


def _modulation(c_all, ada_w, ada_b):
    L, D, ND = ada_w.shape
    tn = 1536
    return pl.pallas_call(
        _mod_body,
        grid=(L, ND // tn),
        in_specs=[pl.BlockSpec((8, D), lambda l, j: (0, 0)),
                  pl.BlockSpec((1, D, tn), lambda l, j: (l, 0, j)),
                  pl.BlockSpec((1, 1, tn), lambda l, j: (l, 0, j))],
        out_specs=pl.BlockSpec((1, 8, tn), lambda l, j: (l, 0, j)),
        out_shape=jax.ShapeDtypeStruct((L, 8, ND), F32),
        compiler_params=_cparams("arbitrary", "arbitrary"),
        name="modulation",
    )(c_all, ada_w, ada_b.reshape(L, 1, ND))


def _inproj_body(has_res, n_lat_tiles, *refs):
    if has_res:
        (x_ref, y_ref, mp_ref, m_ref, g1_ref, w_ref, wq_ref, gq_ref, wkv_ref, wvt_ref, gkv_ref, cos_ref, sin_ref,
         xo_ref, pa_ref, qb_ref, kb_ref, vb_ref, pc_ref, qd_ref, kd_ref, vd_ref) = refs
        x = x_ref[...] + mp_ref[0, 5:6, :] * _slab_rows(y_ref)
    else:
        (xl_ref, xc_ref, m_ref, g1_ref, w_ref, wq_ref, gq_ref, wkv_ref, wvt_ref, gkv_ref, cos_ref, sin_ref,
         xo_ref, pa_ref, qb_ref, kb_ref, vb_ref, pc_ref, qd_ref, kd_ref, vd_ref) = refs
        x = jnp.where(pl.program_id(0) < n_lat_tiles, xl_ref[...], xc_ref[...])
    xo_ref[...] = x
    m = m_ref[0]
    hb = (_rms(x, g1_ref[...]) * (1.0 + m[1:2]) + m[0:1]).astype(BF16)

    def proj(a, b):
        return _dot(hb, w_ref[:, a:b])

    cq = proj(OFF_D, OFF_D + MLA_Q_RANK)
    ckv = proj(OFF_D + MLA_Q_RANK, OFF_KR)
    kr_plain = proj(OFF_KR, OFF_KR + MLA_SLOT)
    kr_swapped = proj(OFF_KR + MLA_SLOT, W_IN_COLS)

    pa_ref[...] = proj(0, OFF_B)
    qb_ref[...] = (proj(OFF_B, OFF_B + MIX_PART) * (GROUP_DIM ** -0.5)).astype(BF16)
    kb_ref[...] = proj(OFF_B + MIX_PART, OFF_B + 2 * MIX_PART).astype(BF16)
    vb_ref[...] = proj(OFF_B + 2 * MIX_PART, OFF_C).astype(BF16)
    pc_ref[...] = proj(OFF_C, OFF_D)

    cosp = cos_ref[...]
    sinp = sin_ref[...]
    qn = _rms(cq, gq_ref[...]).astype(BF16)
    qq = _dot(qn, wq_ref[...])
    half = N_GROUPS * MLA_SLOT
    for h in range(N_GROUPS):
        a = h * MLA_SLOT
        q = qq[:, a:a + MLA_SLOT] * cosp + qq[:, half + a:half + a + MLA_SLOT] * sinp
        qd_ref[:, a:a + MLA_SLOT] = (q * (MLA_SCALE * LOG2_E)).astype(BF16)
    kvn = _rms(ckv, gkv_ref[...]).astype(BF16)
    kk = _dot(kvn, wkv_ref[...])
    kr = kr_plain * cosp + kr_swapped * sinp
    for h in range(N_GROUPS):
        a = h * MLA_SLOT
        kd_ref[:, a:a + MLA_SLOT] = (kk[:, a:a + MLA_SLOT] + kr).astype(BF16)
    ones_row = (lax.broadcasted_iota(jnp.int32, (half, 1), 0) % MLA_SLOT == MLA_V).astype(F32)
    vd_ref[...] = (_dot_nt(wvt_ref[...], kvn) + ones_row).astype(BF16)


def _in_projection(l, n_lat_tiles_per_batch, n_batch, x, y, mod_r, g1, w, wq, gq, wkv, wvt, gkv, cosp, sinp):
    tm = TOKEN_TILE
    has_res = y is not None
    n_lat_tiles = n_lat_tiles_per_batch * n_batch
    if has_res:
        n, d = x.shape
    else:
        n, d = x[0].shape[0] + x[1].shape[0], x[0].shape[1]
        assert x[0].shape[0] == n_lat_tiles * tm

    def row(i):
        return (i, 0)

    def modrow(layer):
        return lambda i: (layer * 8 + jnp.minimum(i // n_lat_tiles_per_batch, n_batch), 0, 0)

    def const2(i):
        return (0, 0)

    tok = lambda c: pl.BlockSpec((tm, c), row)
    mod_spec = lambda layer: pl.BlockSpec((1, N_MOD, d), modrow(layer))
    if has_res:
        in_specs = [tok(d), pl.BlockSpec((tm, FEAT_ROWS, LANES), lambda i: (i, 0, 0)), mod_spec(l - 1)]
        args = [x, y, mod_r]
    else:
        in_specs = [pl.BlockSpec((tm, d), lambda i: (jnp.minimum(i, n_lat_tiles - 1), 0)),
                    pl.BlockSpec((tm, d), lambda i: (jnp.maximum(i - n_lat_tiles, 0), 0))]
        args = list(x)
    in_specs += [mod_spec(l), pl.BlockSpec((1, d), const2), pl.BlockSpec(w.shape, const2),
                 pl.BlockSpec(wq.shape, const2), pl.BlockSpec(gq.shape, const2),
                 pl.BlockSpec(wkv.shape, const2), pl.BlockSpec(wvt.shape, const2), pl.BlockSpec(gkv.shape, const2),
                 tok(MLA_SLOT), tok(MLA_SLOT)]
    args += [mod_r, g1, w, wq, gq, wkv, wvt, gkv, cosp, sinp]
    outs = [(d, F32), (MIX_PART, F32), (MIX_PART, BF16), (MIX_PART, BF16), (MIX_PART, BF16), (2 * MIX_PART, F32),
            (N_GROUPS * MLA_SLOT, BF16), (N_GROUPS * MLA_SLOT, BF16)]
    slots = N_GROUPS * MLA_SLOT
    return pl.pallas_call(
        functools.partial(_inproj_body, has_res, n_lat_tiles),
        grid=(n // tm,),
        in_specs=in_specs,
        out_specs=[tok(c) for c, _ in outs] + [pl.BlockSpec((slots, tm), lambda i: (0, i))],
        out_shape=[jax.ShapeDtypeStruct((n, c), t) for c, t in outs] + [jax.ShapeDtypeStruct((slots, n), BF16)],
        compiler_params=_cparams("arbitrary"),
        name="in_projection",
    )(*args)


def _pool_body(n_lat, seq, ctx_len, prev_ref, cur_ref, next_ref, w_ref, s_ref, o_ref):
    tb = cur_ref.shape[0]
    ext = jnp.concatenate([prev_ref[...], cur_ref[...], next_ref[...]], axis=0)
    n = tb + 16
    g = pl.program_id(0) * tb - 8 + lax.broadcasted_iota(jnp.int32, (n, 1), 0)
    is_lat = g < n_lat
    length = jnp.where(is_lat, seq, ctx_len)
    p = jnp.where(is_lat, g & (seq - 1), (g - n_lat) & (ctx_len - 1))

    def shifted(a, j):
        r = pltpu.roll(a, (-j) % n, axis=0)
        ok = (p + j >= 0) & (p + j < length)
        return jnp.where(ok, r, 0.0)

    before1 = shifted(ext, -1)
    before2 = before1 + shifted(before1, -1)
    before4 = before2 + shifted(before2, -2)
    before8 = before4 + shifted(before4, -4)
    after2 = ext + shifted(ext, 1)
    after4 = after2 + shifted(after2, 2)
    after8 = after4 + shifted(after4, 4)
    lane_grp = lax.broadcasted_iota(jnp.int32, (1, MIX_PART), 1) // GROUP_DIM
    sl = slice(8, 8 + tb)
    tot = jnp.where(lane_grp == 0, (before1 + ext)[sl],
                    jnp.where(lane_grp == 1, (before2 + after2)[sl],
                              jnp.where(lane_grp == 2, (before4 + after4)[sl], (before8 + after8)[sl])))
    half = jnp.where(lane_grp == 0, POOL_WINDOWS[0] // 2,
                     jnp.where(lane_grp == 1, POOL_WINDOWS[1] // 2,
                               jnp.where(lane_grp == 2, POOL_WINDOWS[2] // 2, POOL_WINDOWS[3] // 2)))
    pc = p[sl]
    cnt = jnp.minimum(pc + half, length[sl]) - jnp.maximum(pc - half, 0)
    dlt = tot / cnt.astype(F32) - ext[sl]
    o_ref[...] = (_dot(dlt.astype(BF16), w_ref[...]) * s_ref[...]).astype(BF16)


def _pool_mixer(n_lat, seq, ctx_len, pa, w_bd, s):
    n = pa.shape[0]
    tb = 1024
    assert seq & (seq - 1) == 0 and ctx_len & (ctx_len - 1) == 0
    assert n_lat % tb == 0 and n % tb == 0
    nb8 = n // 8
    return pl.pallas_call(
        functools.partial(_pool_body, n_lat, seq, ctx_len),
        grid=(n // tb,),
        in_specs=[pl.BlockSpec((8, MIX_PART), lambda i: (jnp.maximum(i * (tb // 8) - 1, 0), 0)),
                  pl.BlockSpec((tb, MIX_PART), lambda i: (i, 0)),
                  pl.BlockSpec((8, MIX_PART), lambda i: (jnp.minimum((i + 1) * (tb // 8), nb8 - 1), 0)),
                  pl.BlockSpec(w_bd.shape, lambda i: (0, 0)),
                  pl.BlockSpec(s.shape, lambda i: (0, 0))],
        out_specs=pl.BlockSpec((tb, MIX_PART), lambda i: (i, 0)),
        out_shape=jax.ShapeDtypeStruct((n, MIX_PART), BF16),
        compiler_params=_cparams("arbitrary"),
        name="pool_mixer",
    )(pa, pa, pa, w_bd, s)


def _softmax_pv(parts):
    m = parts[0][0].max(axis=-1, keepdims=True)
    for s, _ in parts[1:]:
        m = jnp.maximum(m, s.max(axis=-1, keepdims=True))
    den = 0.0
    out = 0.0
    for s, v in parts:
        p = jnp.exp(s - m)
        den = den + p.sum(axis=-1, keepdims=True)
        out = out + _dot(p.astype(BF16), v)
    return out / den


def _na_body(rows, n_qt, q_ref, kl_ref, kc_ref, vl_ref, vc_ref, bias_ref, o_ref):
    qt = pl.program_id(1)
    q = q_ref[...]
    kc = kc_ref[...]
    vc = vc_ref[...]
    lane_grp = lax.broadcasted_iota(jnp.int32, (1, MIX_PART), 1) // GROUP_DIM
    zero = jnp.zeros_like(q)

    @pl.when(qt < n_qt)
    def _():
        k0 = jnp.clip(NA_Q_ROWS * qt - NA_WIN_ROWS // 2, 0, rows - NA_K_ROWS)
        off = pl.multiple_of(k0 * GRID_W, NA_Q_ROWS * GRID_W)
        kw = kl_ref[pl.ds(off, NA_K_ROWS * GRID_W), :]
        vw = vl_ref[pl.ds(off, NA_K_ROWS * GRID_W), :]
        def head_scores(h):
            qh = jnp.where(lane_grp == h, q, zero)
            return _dot_nt(qh, kw), _dot_nt(qh, kc)

        o = jnp.zeros(q.shape, F32)
        nxt = head_scores(0)
        for h in range(N_GROUPS):
            s_win, s_ctx = nxt
            if h + 1 < N_GROUPS:
                nxt = head_scores(h + 1)
            oh = _softmax_pv([(s_win + bias_ref[0, h], vw), (s_ctx, vc)])
            o = jnp.where(lane_grp == h, oh, o)
        o_ref[...] = o.astype(BF16)

    @pl.when(qt == n_qt)
    def _():
        o = jnp.zeros(q.shape, F32)
        for h in range(N_GROUPS):
            qh = jnp.where(lane_grp == h, q, zero)
            oh = _softmax_pv([(_dot_nt(qh, kc), vc)])
            o = jnp.where(lane_grp == h, oh, o)
        o_ref[...] = o.astype(BF16)


def _attn_specs(n_batch, seq, ctx_len, width_q, width_k, width_v, width_o):
    n_qt = seq // ATTN_TILE
    lat_blocks = n_batch * n_qt
    assert ctx_len == ATTN_TILE

    def qmap(b, t):
        return (jnp.where(t < n_qt, b * n_qt + t, lat_blocks + b), 0)

    q_spec = pl.BlockSpec((ATTN_TILE, width_q), qmap)
    kl_spec = pl.BlockSpec((seq, width_k), lambda b, t: (b, 0))
    kc_spec = pl.BlockSpec((ctx_len, width_k), lambda b, t: (lat_blocks + b, 0))
    vl_spec = pl.BlockSpec((seq, width_v), lambda b, t: (b, 0))
    vc_spec = pl.BlockSpec((ctx_len, width_v), lambda b, t: (lat_blocks + b, 0))
    o_spec = pl.BlockSpec((ATTN_TILE, width_o), qmap)
    return n_qt, [q_spec, kl_spec, kc_spec, vl_spec, vc_spec], o_spec


def _na_mixer(l, n_batch, seq, ctx_len, qb, kb, vb, bias_tab):
    n = qb.shape[0]
    rows = seq // GRID_W
    n_qt, in_specs, o_spec = _attn_specs(n_batch, seq, ctx_len, MIX_PART, MIX_PART, MIX_PART, MIX_PART)
    assert ATTN_TILE == NA_Q_ROWS * GRID_W and rows >= NA_K_ROWS + NA_Q_ROWS

    def bias_map(b, t):
        kind = jnp.where(t == 0, 0, jnp.where(t >= n_qt - 1, 2, 1))
        return (l * 3 + kind, 0, 0, 0)

    in_specs.append(pl.BlockSpec((1,) + bias_tab.shape[1:], bias_map))
    return pl.pallas_call(
        functools.partial(_na_body, rows, n_qt),
        grid=(n_batch, n_qt + 1),
        in_specs=in_specs,
        out_specs=o_spec,
        out_shape=jax.ShapeDtypeStruct((n, MIX_PART), BF16),
        compiler_params=_cparams("arbitrary", "arbitrary"),
        name="neighborhood_attention",
    )(qb, kb, kb, vb, vb, bias_tab)


def _na_bias_tables(na_rpb, rows):
    L, H = na_rpb.shape[:2]
    kinds = ((0, 0), (NA_Q_ROWS, 0), (rows - NA_Q_ROWS, rows - NA_K_ROWS))
    qc = np.arange(GRID_W)
    kc = np.arange(GRID_W)
    wsc = np.clip(qc - NA_WIN_COLS // 2, 0, GRID_W - NA_WIN_COLS)
    col_ok = (kc[None, :] >= wsc[:, None]) & (kc[None, :] < wsc[:, None] + NA_WIN_COLS)
    padded = _pad_last(na_rpb, GRID_W - NA_WIN_COLS, GRID_W - NA_WIN_COLS)
    shifted = jnp.stack([padded[..., GRID_W - 1 - c:2 * GRID_W - 1 - c] for c in range(GRID_W)], axis=2)
    slabs = jnp.where(col_ok[:, None, :], shifted, NEG_INF)

    def masked(n):
        return jnp.full((L, H, GRID_W, n, GRID_W), NEG_INF, F32)

    tabs = []
    for r0, k0 in kinds:
        per_row = []
        for qr in range(NA_Q_ROWS):
            r = r0 + qr
            first = int(np.clip(r - NA_WIN_ROWS // 2, 0, rows - NA_WIN_ROWS)) - k0
            d0 = k0 + first - r + NA_WIN_ROWS - 1
            assert 0 <= first <= NA_K_ROWS - NA_WIN_ROWS and 0 <= d0 <= NA_WIN_ROWS - 1
            per_row.append(jnp.concatenate(
                [masked(first), slabs[:, :, :, d0:d0 + NA_WIN_ROWS, :], masked(NA_K_ROWS - NA_WIN_ROWS - first)],
                axis=3))
        tabs.append(jnp.stack(per_row, axis=2))
    return jnp.stack(tabs, axis=1).reshape(L * 3, H, NA_Q_ROWS * GRID_W, NA_K_ROWS * GRID_W)


def _gelu_tanh(x):
    return 0.5 * x * (1.0 + jnp.tanh(np.sqrt(2.0 / np.pi).astype(np.float32) * (x + 0.044715 * (x * x * x))))


def _sgu_body(pc_ref, gn_ref, ones_ref, w_ref, b_ref, o_ref):
    tm = pc_ref.shape[0]
    uv = _gelu_tanh(pc_ref[...])
    u = uv[:, :MIX_PART]
    v = uv[:, MIX_PART:]
    ms = jnp.dot(v * v, ones_ref[...], precision=HIGHEST, preferred_element_type=F32)
    vg = (v * lax.rsqrt(ms + EPS) * gn_ref[...]).astype(BF16)
    lane_grp = lax.broadcasted_iota(jnp.int32, (1, MIX_PART), 1) // GROUP_DIM
    w = w_ref[...]
    for c in range(tm // SGU_CHUNK):
        rs = slice(c * SGU_CHUNK, (c + 1) * SGU_CHUNK)
        r = _dot(w, vg[rs])
        mixed = r[:SGU_CHUNK]
        for g in range(1, N_GROUPS):
            mixed = jnp.where(lane_grp == g, r[g * SGU_CHUNK:(g + 1) * SGU_CHUNK], mixed)
        o_ref[rs, :] = (u[rs] * (mixed + b_ref[...])).astype(BF16)


def _sgu_mixer(pc, gn, ones_bd, w_all, b_exp):
    n = pc.shape[0]
    tm = TOKEN_TILE
    const = lambda i: (0, 0)
    return pl.pallas_call(
        _sgu_body,
        grid=(n // tm,),
        in_specs=[pl.BlockSpec((tm, 2 * MIX_PART), lambda i: (i, 0)),
                  pl.BlockSpec(gn.shape, const), pl.BlockSpec(ones_bd.shape, const),
                  pl.BlockSpec(w_all.shape, const), pl.BlockSpec(b_exp.shape, const)],
        out_specs=pl.BlockSpec((tm, MIX_PART), lambda i: (i, 0)),
        out_shape=jax.ShapeDtypeStruct((n, MIX_PART), BF16),
        compiler_params=_cparams("arbitrary"),
        name="spatial_gating",
    )(pc, gn, ones_bd, w_all, b_exp)


def _mla_body(n_qt, n_kv, q_ref, kl_ref, kc_ref, vl_ref, vc_ref, o_ref, sa_ref, sb_ref):
    qt = pl.program_id(1)
    heads = [slice(h * MLA_SLOT, (h + 1) * MLA_SLOT) for h in range(N_GROUPS)]

    def scores(t, h, dst):
        rows = slice(t * MLA_KV_TILE, (t + 1) * MLA_KV_TILE)
        dst[h] = _dot_nt(kl_ref[rows, heads[h]], q_ref[:, heads[h]])

    def consume(t, h, src, m, acc):
        s = src[h]
        mn = jnp.maximum(m, s.max(axis=0, keepdims=True))
        p = jnp.exp2(s - mn).astype(BF16)
        return mn, jnp.exp2(m - mn) * acc + _dot(vl_ref[heads[h], t * MLA_KV_TILE:(t + 1) * MLA_KV_TILE], p)

    def attend(n_tiles):
        ctx_scores = [_dot_nt(kc_ref[:, hs], q_ref[:, hs]) for hs in heads]
        if n_tiles:
            for h in range(N_GROUPS):
                scores(0, h, sa_ref)
        state = []
        for hs, s in zip(heads, ctx_scores):
            m = s.max(axis=0, keepdims=True)
            state.append((m, _dot(vc_ref[hs, :], jnp.exp2(s - m).astype(BF16))))
        for t in range(n_tiles):
            cur, nxt = (sa_ref, sb_ref) if t % 2 == 0 else (sb_ref, sa_ref)
            out = []
            for h, (m, acc) in enumerate(state):
                if t + 1 < n_tiles:
                    scores(t + 1, h, nxt)
                out.append(consume(t, h, cur, m, acc))
            state = out
        out_t = jnp.concatenate([acc[:MLA_V] / acc[MLA_V:MLA_V + 1] for _, acc in state], axis=0)
        o_ref[...] = out_t.T.astype(BF16)

    @pl.when(qt < n_qt)
    def _():
        attend(n_kv)

    @pl.when(qt == n_qt)
    def _():
        attend(0)


def _mla_mixer(n_batch, seq, ctx_len, qd, kd, vdt):
    n = qd.shape[0]
    slots = N_GROUPS * MLA_SLOT
    n_qt, in_specs, o_spec = _attn_specs(n_batch, seq, ctx_len, slots, slots, slots, N_GROUPS * MLA_V)
    lat_blocks = n_batch * n_qt
    in_specs[3] = pl.BlockSpec((slots, seq), lambda b, t: (0, b))
    in_specs[4] = pl.BlockSpec((slots, ctx_len), lambda b, t: (0, lat_blocks + b))
    n_kv = seq // MLA_KV_TILE
    assert seq % MLA_KV_TILE == 0 and n_kv <= MLA_MAX_UNROLLED_TILES
    return pl.pallas_call(
        functools.partial(_mla_body, n_qt, n_kv),
        grid=(n_batch, n_qt + 1),
        in_specs=in_specs,
        out_specs=o_spec,
        out_shape=jax.ShapeDtypeStruct((n, N_GROUPS * MLA_V), BF16),
        scratch_shapes=[pltpu.VMEM((N_GROUPS, MLA_KV_TILE, ATTN_TILE), F32)] * 2,
        compiler_params=_cparams("arbitrary", "arbitrary"),
        name="latent_attention",
    )(qd, kd, kd, vdt, vdt)


def _outproj_body(ya_ref, yb_ref, yc_ref, yd_ref, x_ref, m_ref, g2_ref, w_ref, rw_ref, rb_ref,
                  xo_ref, hx_ref, r_ref):
    tm = x_ref.shape[0]
    m = m_ref[0]
    mixed = jnp.concatenate([ya_ref[...], yb_ref[...], yc_ref[...], yd_ref[...]], axis=1)
    x = x_ref[...] + m[2:3] * _dot(mixed, w_ref[...])
    xo_ref[...] = x
    h = _rms(x, g2_ref[...]) * (1.0 + m[4:5]) + m[3:4]

    scores = jax.nn.sigmoid(_dot_nt(rw_ref[...], h, precision=HIGHEST))
    biased = scores + rb_ref[...]
    E = EXPERTS_PER_GROUP
    bk = [biased[8 * k:8 * k + 8] for k in range(E)]
    sk = [scores[8 * k:8 * k + 8] for k in range(E)]
    gs = None
    for a in range(E):
        for b in range(a + 1, E):
            pair = bk[a] + bk[b]
            gs = pair if gs is None else jnp.maximum(gs, pair)
    best = gs[0:1]
    idx = jnp.zeros((1, tm), jnp.int32)
    for g in range(1, N_EXPERT_GROUPS):
        better = gs[g:g + 1] > best
        idx = jnp.where(better, g, idx)
        best = jnp.where(better, gs[g:g + 1], best)
    in_grp = lax.broadcasted_iota(jnp.int32, (8, tm), 0) == idx
    wk = []
    for k in range(E):
        rank = jnp.zeros((8, tm), jnp.int32)
        for j in range(E):
            if j != k:
                ahead = (bk[j] > bk[k]) | ((bk[j] == bk[k]) & (j < k))
                rank = rank + ahead.astype(jnp.int32)
        wk.append(jnp.where((rank < 2) & in_grp, sk[k], 0.0).sum(axis=0, keepdims=True))
    den = wk[0] + wk[1] + wk[2] + wk[3]
    ri = lax.broadcasted_iota(jnp.int32, (8, tm), 0)
    out = jnp.where(ri == E, idx.astype(F32), 0.0)
    for k in range(E):
        out = jnp.where(ri == k, wk[k] / den, out)
    r_ref[...] = out

    for s in range(FEAT_ROWS):
        hx_ref[:, s, :] = h[:, s * LANES:(s + 1) * LANES]
    hx_ref[:, FEAT_ROWS, :] = jnp.concatenate([out, jnp.zeros((LANES - 8, tm), F32)], axis=0).T
    hx_ref[:, FEAT_ROWS + 1:, :] = jnp.zeros((tm, SLAB_ROWS - FEAT_ROWS - 1, LANES), F32)


def _out_projection(l, n_lat_tiles_per_batch, n_batch, ys, x, mod_r, g2, w, rw, rb):
    n, d = x.shape
    tm = TOKEN_TILE
    row = lambda i: (i, 0)
    const = lambda i: (0, 0)
    modrow = lambda i: (l * 8 + jnp.minimum(i // n_lat_tiles_per_batch, n_batch), 0, 0)
    in_specs = [pl.BlockSpec((tm, MIX_PART), row)] * 4 + [
        pl.BlockSpec((tm, d), row), pl.BlockSpec((1, N_MOD, d), modrow), pl.BlockSpec((1, d), const),
        pl.BlockSpec(w.shape, const), pl.BlockSpec(rw.shape, const), pl.BlockSpec(rb.shape, const)]
    return pl.pallas_call(
        _outproj_body,
        grid=(n // tm,),
        in_specs=in_specs,
        out_specs=[pl.BlockSpec((tm, d), row), pl.BlockSpec((tm, SLAB_ROWS, LANES), lambda i: (i, 0, 0)),
                   pl.BlockSpec((8, tm), lambda i: (0, i))],
        out_shape=[jax.ShapeDtypeStruct((n, d), F32), jax.ShapeDtypeStruct((n, SLAB_ROWS, LANES), F32),
                   jax.ShapeDtypeStruct((8, n), F32)],
        compiler_params=_cparams("arbitrary"),
        name="out_projection_routing",
    )(*ys, x, mod_r, g2, w, rw, rb)


def _moe_body(tg_ref, tv_ref, src_ref, dst_ref, hx_hbm, wg_ref, wu_ref, wd_ref, y_hbm,
              hbuf, ybuf, wgb, wub, wdb, gsem, ssem):
    i = pl.program_id(0)
    n_t = pl.num_programs(0)
    tm = hbuf.shape[1]
    slot = i % 2
    last = n_t - 1
    grp = tg_ref[i]
    prev_valid = tv_ref[jnp.maximum(i - 1, 0)] > 0

    def gather_copy(r, t, sl):
        return pltpu.make_async_copy(hx_hbm.at[src_ref[t * tm + r]], hbuf.at[sl, r], gsem.at[sl])

    def scatter_copy(r, t, sl):
        return pltpu.make_async_copy(ybuf.at[sl, r], y_hbm.at[dst_ref[t * tm + r]], ssem.at[sl])

    def start_all(copy, t, sl):
        def body(j, c):
            copy(2 * j, t, sl).start(priority=0)
            copy(2 * j + 1, t, sl).start(priority=1)
            return c
        lax.fori_loop(0, tm // 2, body, 0, unroll=4)

    def wait_gather(sl):
        pltpu.make_async_copy(hx_hbm.at[pl.ds(0, tm)], hbuf.at[sl], gsem.at[sl]).wait()

    def wait_scatter(sl):
        pltpu.make_async_copy(ybuf.at[sl], y_hbm.at[pl.ds(0, tm)], ssem.at[sl]).wait()

    @pl.when(i == 0)
    def _():
        start_all(gather_copy, 0, 0)

    @pl.when((i == 0) | (grp != tg_ref[jnp.maximum(i - 1, 0)]))
    def _():
        wgb[...] = wg_ref[0].astype(BF16)
        wub[...] = wu_ref[0].astype(BF16)
        wdb[...] = wd_ref[0].astype(BF16)

    @pl.when((i == 0) | prev_valid)
    def _():
        wait_gather(slot)

    @pl.when(i >= 2)
    def _():
        wait_scatter(slot)

    def expert_tile(with_scatter):
        nxt = jnp.minimum(i + 1, last)
        n_batches = 3 * EXPERTS_PER_GROUP
        s_bounds = [min(tm, (b * tm) // (n_batches - 4)) for b in range(n_batches + 1)]
        g_bounds = [min(tm, (b * tm) // (n_batches - 2)) for b in range(n_batches + 1)]

        def start_batch(b):
            if with_scatter:
                for r in range(s_bounds[b], s_bounds[b + 1]):
                    scatter_copy(r, i - 1, 1 - slot).start(priority=1)
            for r in range(g_bounds[b], g_bounds[b + 1]):
                gather_copy(r, nxt, 1 - slot).start(priority=0)

        h = jnp.concatenate([hbuf[slot, :, s, :] for s in range(FEAT_ROWS)], axis=1).astype(BF16)
        gates = hbuf[slot, :, FEAT_ROWS, :]
        acc = jnp.zeros((tm, FEAT_ROWS * LANES), F32)
        for k in range(EXPERTS_PER_GROUP):
            start_batch(3 * k)
            a = _dot(h, wgb[k])
            start_batch(3 * k + 1)
            u = _dot(h, wub[k])
            hid = _silu(a) * u * gates[:, k:k + 1]
            start_batch(3 * k + 2)
            acc = acc + _dot(hid.astype(BF16), wdb[k])
        for s in range(FEAT_ROWS):
            ybuf[slot, :, s, :] = acc[:, s * LANES:(s + 1) * LANES]

    valid = tv_ref[i] > 0

    @pl.when(valid & (i == 0))
    def _():
        expert_tile(False)

    @pl.when(valid & (i > 0))
    def _():
        expert_tile(True)

    @pl.when(jnp.logical_not(valid))
    def _():
        ybuf[slot] = jnp.zeros(ybuf.shape[1:], F32)

        @pl.when(i > 0)
        def _():
            start_all(scatter_copy, i - 1, 1 - slot)

    @pl.when(i == last)
    def _():
        start_all(scatter_copy, i, slot)
        wait_scatter(slot)

        @pl.when(i > 0)
        def _():
            wait_scatter(1 - slot)

        @pl.when(valid)
        def _():
            wait_gather(1 - slot)


def _grouped_experts(l, tile_grp, tile_ok, src, dst, hx, w_gate, w_up, w_down):
    n_pad = src.shape[0]
    tm = MOE_TILE
    E = EXPERTS_PER_GROUP
    d = FEAT_ROWS * LANES
    wmap = lambda i, tg, tv, sr, ds: (l, tg[i], 0, 0)
    grid_spec = pltpu.PrefetchScalarGridSpec(
        num_scalar_prefetch=4,
        grid=(n_pad // tm,),
        in_specs=[pl.BlockSpec(memory_space=pl.ANY),
                  pl.BlockSpec((1, E, d, D_EXPERT), wmap),
                  pl.BlockSpec((1, E, d, D_EXPERT), wmap),
                  pl.BlockSpec((1, E, D_EXPERT, d), wmap)],
        out_specs=pl.BlockSpec(memory_space=pl.ANY),
        scratch_shapes=[pltpu.VMEM((2, tm, SLAB_ROWS, LANES), F32), pltpu.VMEM((2, tm, FEAT_ROWS, LANES), F32),
                        pltpu.VMEM((E, d, D_EXPERT), BF16), pltpu.VMEM((E, d, D_EXPERT), BF16),
                        pltpu.VMEM((E, D_EXPERT, d), BF16),
                        pltpu.SemaphoreType.DMA((2,)), pltpu.SemaphoreType.DMA((2,))])
    return pl.pallas_call(
        _moe_body,
        grid_spec=grid_spec,
        out_shape=jax.ShapeDtypeStruct((n_pad, FEAT_ROWS, LANES), F32),
        compiler_params=_cparams("arbitrary"),
        name="grouped_experts",
    )(tile_grp, tile_ok, src, dst, hx, w_gate, w_up, w_down)


def _owner_body(pos_ref, owner_ref):
    def clear(s, c):
        owner_ref[s] = 0
        return c

    def put(t, c):
        owner_ref[pos_ref[t]] = t + 1
        return c

    lax.fori_loop(0, owner_ref.shape[0], clear, 0, unroll=8)
    lax.fori_loop(0, pos_ref.shape[0], put, 0, unroll=8)


def _slot_owner(pos, n_pad):
    return pl.pallas_call(
        _owner_body,
        in_specs=[pl.BlockSpec(memory_space=pltpu.SMEM)],
        out_specs=pl.BlockSpec(memory_space=pltpu.SMEM),
        out_shape=jax.ShapeDtypeStruct((n_pad,), jnp.int32),
        name="slot_owner",
    )(pos)


def _dispatch_plan(route, n_pad):
    n = route.shape[1]
    tm = MOE_TILE
    grp = route[EXPERTS_PER_GROUP].astype(jnp.int32)
    onehot = (grp[:, None] == jnp.arange(N_EXPERT_GROUPS)[None, :]).astype(jnp.int32)
    csum = jnp.cumsum(onehot, axis=0)
    counts = csum[-1]
    rank = jnp.sum(csum * onehot, axis=1) - 1
    padded = ((counts + tm - 1) // tm) * tm
    ends = jnp.cumsum(padded)
    pos = jnp.sum((ends - padded)[None, :] * onehot, axis=1) + rank
    owner = _slot_owner(pos, n_pad)
    used = owner > 0
    src = jnp.maximum(owner - 1, 0)
    dst = jnp.where(used, owner - 1, n - 1 + jnp.cumsum(1 - used.astype(jnp.int32)))
    tile_start = jnp.arange(n_pad // tm, dtype=jnp.int32) * tm
    tile_grp = jnp.minimum(jnp.sum(tile_start[:, None] >= ends[None, :], axis=1), N_EXPERT_GROUPS - 1)
    tile_ok = (tile_start < ends[-1]).astype(jnp.int32)
    return tile_grp.astype(jnp.int32), tile_ok, src, dst


def _slab_rows(ref):
    return jnp.concatenate([ref[:, s, :] for s in range(FEAT_ROWS)], axis=1)


def _final_body(x_ref, y_ref, m_ref, g_ref, o_ref):
    x = x_ref[...] + m_ref[0, 5:6, :] * _slab_rows(y_ref)
    o_ref[...] = _rms(x, g_ref[...])


def _final_norm(l, n_lat, n_lat_tiles_per_batch, x, y, mod_r, g):
    d = x.shape[1]
    tm = TOKEN_TILE
    row = lambda i: (i, 0)
    return pl.pallas_call(
        _final_body,
        grid=(n_lat // tm,),
        in_specs=[pl.BlockSpec((tm, d), row), pl.BlockSpec((tm, FEAT_ROWS, LANES), lambda i: (i, 0, 0)),
                  pl.BlockSpec((1, N_MOD, d), lambda i: (l * 8 + i // n_lat_tiles_per_batch, 0, 0)),
                  pl.BlockSpec((1, d), lambda i: (0, 0))],
        out_specs=pl.BlockSpec((tm, d), row),
        out_shape=jax.ShapeDtypeStruct((n_lat, d), F32),
        compiler_params=_cparams("arbitrary"),
        name="final_norm",
    )(x, y, mod_r, g)


def _rope_tables(seq, n_batch, n_ctx_tokens):
    t = np.arange(seq)
    nf = MLA_ROPE // 4
    inv = jnp.asarray(ROPE_BASE, F32) ** (-jnp.arange(nf, dtype=F32) / nf)
    rang = jnp.asarray(t // GRID_W, F32)[:, None] * inv
    cang = jnp.asarray(t % GRID_W, F32)[:, None] * inv
    cr, sr, cc, sc = jnp.cos(rang), jnp.sin(rang), jnp.cos(cang), jnp.sin(cang)
    one = jnp.ones((seq, MLA_NOPE), F32)
    zero_tail = jnp.zeros((seq, MLA_SLOT - MLA_NOPE - MLA_ROPE), F32)
    cosp = jnp.concatenate([one, cr, cr, cc, cc, zero_tail], axis=1)
    sinp = jnp.concatenate([0 * one, -sr, sr, -sc, sc, zero_tail], axis=1)
    ctx_cos = jnp.concatenate([jnp.ones((n_ctx_tokens, MLA_NOPE + MLA_ROPE), F32),
                               jnp.zeros((n_ctx_tokens, MLA_SLOT - MLA_NOPE - MLA_ROPE), F32)], axis=1)
    cosp = jnp.concatenate([jnp.tile(cosp, (n_batch, 1)), ctx_cos], axis=0)
    sinp = jnp.concatenate([jnp.tile(sinp, (n_batch, 1)), jnp.zeros((n_ctx_tokens, MLA_SLOT), F32)], axis=0)
    return cosp, sinp


_ROPE_SWAP = np.concatenate([np.arange(8, 16), np.arange(0, 8), np.arange(24, 32), np.arange(16, 24)])


def _pad_last(a, before, after):
    return jnp.pad(a, [(0, 0)] * (a.ndim - 1) + [(before, after)])


def _layouts(w_in, w_out, pool_w, sgu_w, sgu_b, mla_w_uq, mla_w_ukv, router_w, router_b):
    L = w_in.shape[0]
    tail = MLA_SLOT - MLA_NOPE - MLA_ROPE
    kr = w_in[:, :, OFF_KR:]
    w = jnp.concatenate([w_in[:, :, :OFF_KR], _pad_last(kr, MLA_NOPE, tail),
                         _pad_last(kr[:, :, _ROPE_SWAP], MLA_NOPE, tail)], axis=-1).astype(BF16)
    uq = mla_w_uq.reshape(L, MLA_Q_RANK, N_GROUPS, MLA_NOPE + MLA_ROPE)
    q1 = _pad_last(uq, 0, tail).reshape(L, MLA_Q_RANK, N_GROUPS * MLA_SLOT)
    q2 = _pad_last(uq[..., MLA_NOPE:][..., _ROPE_SWAP], MLA_NOPE, tail).reshape(L, MLA_Q_RANK, N_GROUPS * MLA_SLOT)
    wq = jnp.concatenate([q1, q2], axis=-1).astype(BF16)
    ukv = mla_w_ukv.reshape(L, MLA_KV_RANK, N_GROUPS, MLA_NOPE + MLA_V)
    kn = _pad_last(ukv[..., :MLA_NOPE], 0, MLA_SLOT - MLA_NOPE).reshape(L, MLA_KV_RANK, N_GROUPS * MLA_SLOT)
    vv = _pad_last(ukv[..., MLA_NOPE:], 0, MLA_SLOT - MLA_V).reshape(L, MLA_KV_RANK, N_GROUPS * MLA_SLOT)
    wkv = kn.astype(BF16)
    wvt = jnp.swapaxes(vv, 1, 2).astype(BF16)
    eye = jnp.eye(N_GROUPS, dtype=F32)
    pool_bd = jnp.einsum('lgcd,gh->lgchd', pool_w, eye).reshape(L, MIX_PART, MIX_PART).astype(BF16)
    ones_bd = jnp.asarray(np.kron(np.eye(N_GROUPS), np.full((GROUP_DIM, GROUP_DIM), 1.0 / GROUP_DIM)), F32)
    sgu_all = sgu_w.reshape(L, N_GROUPS * SGU_CHUNK, SGU_CHUNK).astype(BF16)
    sgu_bias = jnp.repeat(jnp.swapaxes(sgu_b, 1, 2), GROUP_DIM, axis=2)
    rw = router_w.T.reshape(N_EXPERT_GROUPS, EXPERTS_PER_GROUP, -1).transpose(1, 0, 2)
    rw = jnp.pad(rw, ((0, 0), (0, 8 - N_EXPERT_GROUPS), (0, 0))).reshape(8 * EXPERTS_PER_GROUP, -1)
    rb = router_b.reshape(N_EXPERT_GROUPS, EXPERTS_PER_GROUP).T
    rb = jnp.pad(rb, ((0, 0), (0, 8 - N_EXPERT_GROUPS))).reshape(8 * EXPERTS_PER_GROUP, 1)
    return w, wq, wkv, wvt, w_out.astype(BF16), pool_bd, ones_bd, sgu_all, sgu_bias, rw, rb


def kernel(x, c, ctx, c_ctx, ada_w, ada_b, norm1_g, norm2_g, w_in, w_out, pool_w, pool_s, na_rpb,
           sgu_norm_g, sgu_w, sgu_b, mla_q_norm_g, mla_w_uq, mla_kv_norm_g, mla_w_ukv,
           router_w, router_b, moe_w_gate, moe_w_up, moe_w_down, final_g):
    B, S, D = x.shape
    CTX = ctx.shape[1]
    L = ada_w.shape[0]
    n_lat, n_ctx = B * S, B * CTX
    n_all = n_lat + n_ctx
    assert B < 8 and S % TOKEN_TILE == 0 and n_ctx % TOKEN_TILE == 0
    tiles_per_batch = S // TOKEN_TILE
    n_pad = n_all + N_EXPERT_GROUPS * MOE_TILE

    c_all = jnp.concatenate([c, c_ctx[None, :], jnp.zeros((8 - B - 1, D), F32)], axis=0)
    mod_r = _modulation(c_all, ada_w, ada_b).reshape(L * 8, N_MOD, D)
    w, wq, wkv, wvt, wo, pool_bd, ones_bd, sgu_all, sgu_bias, rw, rb = _layouts(
        w_in, w_out, pool_w, sgu_w, sgu_b, mla_w_uq, mla_w_ukv, router_w, router_b)
    bias_tab = _na_bias_tables(na_rpb, S // GRID_W)
    cosp, sinp = _rope_tables(S, B, n_ctx)

    xs = (x.reshape(n_lat, D), ctx.reshape(n_ctx, D))
    y = None
    for l in range(L):
        xs, pa, qb, kb, vb, pc, qd, kd, vd = _in_projection(
            l, tiles_per_batch, B, xs, y, mod_r, norm1_g[l][None], w[l], wq[l], mla_q_norm_g[l][None],
            wkv[l], wvt[l], mla_kv_norm_g[l][None], cosp, sinp)
        ya = _pool_mixer(n_lat, S, CTX, pa, pool_bd[l], pool_s[l][None])
        yb = _na_mixer(l, B, S, CTX, qb, kb, vb, bias_tab)
        yc = _sgu_mixer(pc, sgu_norm_g[l][None], ones_bd, sgu_all[l], sgu_bias[l])
        yd = _mla_mixer(B, S, CTX, qd, kd, vd)
        xs, hx, route = _out_projection(l, tiles_per_batch, B, (ya, yb, yc, yd), xs, mod_r, norm2_g[l][None],
                                        wo[l], rw, rb)
        tile_grp, tile_ok, src, dst = _dispatch_plan(route, n_pad)
        y = _grouped_experts(l, tile_grp, tile_ok, src, dst, hx, moe_w_gate, moe_w_up, moe_w_down)
    out = _final_norm(L - 1, n_lat, tiles_per_batch, xs, y, mod_r, final_g[None])
    return out.reshape(B, S, D)
```
